```python
import math
import jax, jax.numpy as jnp
from jax import lax
import numpy as np

D_MODEL = 2048
BATCH = 2
SEQ = 4096
DEPTH = 4

GRID_W = 64
CTX_LEN = 256
HEAD_DIM = 128
A_HEADS = 6
A_KV = 2
WINDOW = 128
WB = 128
B_HEADS = 6
B_KV = 2
QB = 128
C_HEADS = 4
C_Q_LORA = 512
C_KV_LORA = 256
C_NOPE = 128
C_ROPE = 64
C_V = 128
D_MIX = A_HEADS * HEAD_DIM + B_HEADS * HEAD_DIM + C_HEADS * C_V
IN_WIDTHS = (A_HEADS * HEAD_DIM, A_KV * HEAD_DIM, A_KV * HEAD_DIM,
             B_HEADS * HEAD_DIM, B_KV * HEAD_DIM, B_KV * HEAD_DIM,
             C_Q_LORA, C_KV_LORA, C_ROPE)
D_IN = sum(IN_WIDTHS)
D_FF = ((8 * D_MODEL // 3 + 255) // 256) * 256
ROPE_THETA = 10000.0
EPS = 1e-6
NEG = -1e30

kernel_name = "hybrid_parallel_heads_dit_trunk"


def rmsnorm(x, g):
    xf = x.astype(jnp.float32)
    y = xf * lax.rsqrt(jnp.mean(xf * xf, axis=-1, keepdims=True) + EPS)
    return (y * g.astype(jnp.float32)).astype(x.dtype)


def modulate(h, shift, scale):
    return h * (1 + scale) + shift


def heads(t, n):
    return t.reshape(*t.shape[:-1], n, t.shape[-1] // n)


def _rot_half(x, pos):
    n = x.shape[-1] // 2
    freqs = jnp.power(jnp.float32(ROPE_THETA), -jnp.arange(n, dtype=jnp.float32) / n)
    ang = pos.astype(jnp.float32)[:, None] * freqs[None, :]
    cos = jnp.cos(ang)[:, None, :]
    sin = jnp.sin(ang)[:, None, :]
    x1, x2 = x[..., :n], x[..., n:]
    return jnp.concatenate([x1 * cos - x2 * sin, x2 * cos + x1 * sin], axis=-1).astype(x.dtype)


def axial_rope(x, row, col):
    half = x.shape[-1] // 2
    return jnp.concatenate([_rot_half(x[..., :half], row), _rot_half(x[..., half:], col)], axis=-1)


def ctx_attn(q, k, v, scale, sink=None):
    B, L, H, dk = q.shape
    G = k.shape[2]
    R = H // G
    qg = q.reshape(B, L, G, R, dk)
    s = jnp.einsum('bqgrd,bkgd->bgrqk', qg, k).astype(jnp.float32) * scale
    if sink is not None:
        sk = jnp.broadcast_to(sink.astype(jnp.float32).reshape(G, R)[None, :, :, None, None], (B, G, R, L, 1))
        s = jnp.concatenate([s, sk], axis=-1)
    p = jax.nn.softmax(s, axis=-1)[..., :L]
    o = jnp.einsum('bgrqk,bkgd->bqgrd', p.astype(v.dtype), v)
    return o.reshape(B, L, H, v.shape[-1])


def window_attn(q, k, v, kc, vc, sink, scale):
    B, S, H, d = q.shape
    G = k.shape[2]
    R = H // G
    nb = S // WB
    pad = ((0, 0), (WB, WB), (0, 0), (0, 0))
    kp = jnp.pad(k, pad).reshape(B, nb + 2, WB, G, d)
    vp = jnp.pad(v, pad).reshape(B, nb + 2, WB, G, v.shape[-1])
    kband = jnp.concatenate([kp[:, :-2], kp[:, 1:-1], kp[:, 2:]], axis=2)
    vband = jnp.concatenate([vp[:, :-2], vp[:, 1:-1], vp[:, 2:]], axis=2)
    qb = q.reshape(B, nb, WB, G, R, d)
    s_band = jnp.einsum('bnqgrd,bnkgd->bngrqk', qb, kband).astype(jnp.float32) * scale
    qpos = jnp.arange(S).reshape(nb, WB)
    kpos = (jnp.arange(nb) * WB - WB)[:, None] + jnp.arange(3 * WB)[None, :]
    diff = kpos[:, None, :] - qpos[:, :, None]
    valid = (jnp.abs(diff) <= WINDOW) & (kpos >= 0)[:, None, :] & (kpos < S)[:, None, :]
    s_band = jnp.where(valid[None, :, None, None], s_band, NEG)
    s_ctx = jnp.einsum('bnqgrd,bkgd->bngrqk', qb, kc).astype(jnp.float32) * scale
    L = kc.shape[1]
    s_sink = jnp.broadcast_to(sink.astype(jnp.float32).reshape(G, R)[None, None, :, :, None, None],
                              (B, nb, G, R, WB, 1))
    p = jax.nn.softmax(jnp.concatenate([s_band, s_ctx, s_sink], axis=-1), axis=-1)
    p_band = p[..., :3 * WB].astype(v.dtype)
    p_ctx = p[..., 3 * WB:3 * WB + L].astype(v.dtype)
    o = (jnp.einsum('bngrqk,bnkgd->bnqgrd', p_band, vband)
         + jnp.einsum('bngrqk,bkgd->bnqgrd', p_ctx, vc))
    return o.reshape(B, S, H, v.shape[-1])


def dense_attn_blocks(q, k, v, kc, vc, scale):
    B, S, H, dk = q.shape
    G = k.shape[2]
    R = H // G
    dv = v.shape[-1]
    kk = jnp.concatenate([kc, k], axis=1)
    vv = jnp.concatenate([vc, v], axis=1)
    qb = q.reshape(B, S // QB, QB, G, R, dk).transpose(1, 0, 2, 3, 4, 5)

    def one_block(qblk):
        s = jnp.einsum('bqgrd,bkgd->bgrqk', qblk, kk).astype(jnp.float32) * scale
        p = jax.nn.softmax(s, axis=-1)
        return jnp.einsum('bgrqk,bkgd->bqgrd', p.astype(vv.dtype), vv)

    o = lax.map(one_block, qb)
    return o.transpose(1, 0, 2, 3, 4, 5).reshape(B, S, H, dv)


def split_cols(p):
    offs = []
    acc = 0
    for w in IN_WIDTHS[:-1]:
        acc += w
        offs.append(acc)
    return jnp.split(p, offs, axis=-1)


def mla_qkv(cq, ckv, ckr, g_cq, w_uq, g_ckv, w_ukv, row, col, rotate):
    q = heads(rmsnorm(cq, g_cq) @ w_uq, C_HEADS)
    q_nope, q_rope = q[..., :C_NOPE], q[..., C_NOPE:]
    kv = heads(rmsnorm(ckv, g_ckv) @ w_ukv, C_HEADS)
    k_nope, v = kv[..., :C_NOPE], kv[..., C_NOPE:]
    k_rope = ckr[..., None, :]
    if rotate:
        q_rope = axial_rope(q_rope, row, col)
        k_rope = axial_rope(k_rope, row, col)
    k_rope = jnp.broadcast_to(k_rope, (*k_nope.shape[:-1], C_ROPE))
    return (jnp.concatenate([q_nope, q_rope], axis=-1),
            jnp.concatenate([k_nope, k_rope], axis=-1), v)


def token_mixers(h, hc, row, col, w_in, sink_a, g_qn_b, g_kn_b, g_cq, w_uq, g_ckv, w_ukv, w_out, with_ctx):
    B, S, _ = h.shape
    L = hc.shape[1]
    aq, ak, av, bq, bk, bv, cq, ckv, ckr = split_cols(h @ w_in)
    aqc, akc, avc, bqc, bkc, bvc, cqc, ckvc, ckrc = split_cols(hc @ w_in)
    sc_ab = 1.0 / math.sqrt(HEAD_DIM)
    sc_c = 1.0 / math.sqrt(C_NOPE + C_ROPE)

    qa = axial_rope(heads(aq, A_HEADS), row, col)
    ka = axial_rope(heads(ak, A_KV), row, col)
    va = heads(av, A_KV)
    kac, vac = heads(akc, A_KV), heads(avc, A_KV)
    oa = window_attn(qa, ka, va, kac, vac, sink_a, sc_ab)

    qb = axial_rope(rmsnorm(heads(bq, B_HEADS), g_qn_b), row, col)
    kb = axial_rope(rmsnorm(heads(bk, B_KV), g_kn_b), row, col)
    vb = heads(bv, B_KV)
    kbc, vbc = rmsnorm(heads(bkc, B_KV), g_kn_b), heads(bvc, B_KV)
    ob = dense_attn_blocks(qb, kb, vb, kbc, vbc, sc_ab)

    qc, kc_, vc_ = mla_qkv(cq, ckv, ckr, g_cq, w_uq, g_ckv, w_ukv, row, col, True)
    qcc, kcc, vcc = mla_qkv(cqc, ckvc, ckrc, g_cq, w_uq, g_ckv, w_ukv, row, col, False)
    oc = dense_attn_blocks(qc, kc_, vc_, kcc, vcc, sc_c)

    out = jnp.concatenate([oa.reshape(B, S, -1), ob.reshape(B, S, -1), oc.reshape(B, S, -1)], axis=-1) @ w_out
    if not with_ctx:
        return out, None
    oac = ctx_attn(heads(aqc, A_HEADS), kac, vac, sc_ab, sink_a)
    obc = ctx_attn(rmsnorm(heads(bqc, B_HEADS), g_qn_b), kbc, vbc, sc_ab)
    occ = ctx_attn(qcc, kcc, vcc, sc_c)
    out_c = jnp.concatenate([oac.reshape(B, L, -1), obc.reshape(B, L, -1), occ.reshape(B, L, -1)], axis=-1) @ w_out
    return out, out_c


def swiglu(h, w1, w3, w2):
    return (jax.nn.silu(h @ w1) * (h @ w3)) @ w2


def setup_inputs(seed: int = 0) -> dict:
    key = jax.random.key(seed)
    ks = jax.random.split(key, 24)
    f32 = jnp.float32

    def nrm(k, shape, scale):
        return jax.random.normal(k, shape, f32) * scale

    def gain(k, shape):
        return 1.0 + 0.05 * jax.random.normal(k, shape, f32)

    D = D_MODEL
    return {
        "x": nrm(ks[0], (BATCH, SEQ, D), 1.0),
        "c": nrm(ks[1], (BATCH, D), 1.0),
        "ctx": nrm(ks[2], (BATCH, CTX_LEN, D), 1.0),
        "c_ctx": nrm(ks[3], (D,), 1.0),
        "w_mod": nrm(ks[4], (DEPTH, D, 6 * D), 0.5 * D ** -0.5),
        "b_mod": nrm(ks[5], (DEPTH, 6 * D), 0.02),
        "g_mix_pre": gain(ks[6], (DEPTH, D)),
        "g_mix_post": gain(ks[7], (DEPTH, D)),
        "g_ffn_pre": gain(ks[8], (DEPTH, D)),
        "g_ffn_post": gain(ks[9], (DEPTH, D)),
        "w_in": nrm(ks[10], (DEPTH, D, D_IN), D ** -0.5),
        "sink_a": nrm(ks[11], (DEPTH, A_HEADS), 0.5),
        "g_qn_b": gain(ks[12], (DEPTH, HEAD_DIM)),
        "g_kn_b": gain(ks[13], (DEPTH, HEAD_DIM)),
        "g_cq": gain(ks[14], (DEPTH, C_Q_LORA)),
        "w_uq": nrm(ks[15], (DEPTH, C_Q_LORA, C_HEADS * (C_NOPE + C_ROPE)), C_Q_LORA ** -0.5),
        "g_ckv": gain(ks[16], (DEPTH, C_KV_LORA)),
        "w_ukv": nrm(ks[17], (DEPTH, C_KV_LORA, C_HEADS * (C_NOPE + C_V)), C_KV_LORA ** -0.5),
        "w_out": nrm(ks[18], (DEPTH, D_MIX, D), D_MIX ** -0.5),
        "w_ffn1": nrm(ks[19], (DEPTH, D, D_FF), D ** -0.5),
        "w_ffn3": nrm(ks[20], (DEPTH, D, D_FF), D ** -0.5),
        "w_ffn2": nrm(ks[21], (DEPTH, D_FF, D), D_FF ** -0.5),
    }


def reference(x, c, ctx, c_ctx, w_mod, b_mod, g_mix_pre, g_mix_post, g_ffn_pre, g_ffn_post,
              w_in, sink_a, g_qn_b, g_kn_b, g_cq, w_uq, g_ckv, w_ukv, w_out, w_ffn1, w_ffn3, w_ffn2):
    B, S, D = x.shape
    rows = S // GRID_W
    row = jnp.repeat(jnp.arange(rows, dtype=jnp.int32), GRID_W)
    col = jnp.tile(jnp.arange(GRID_W, dtype=jnp.int32), rows)
    xc = ctx
    sc_in = jax.nn.silu(c)
    scc_in = jax.nn.silu(c_ctx)
    for l in range(DEPTH):
        last = l == DEPTH - 1
        mod = (sc_in @ w_mod[l] + b_mod[l]).reshape(B, 6, D)[:, :, None, :]
        modc = (scc_in @ w_mod[l] + b_mod[l]).reshape(6, D)
        h = modulate(rmsnorm(x, g_mix_pre[l]), mod[:, 0], mod[:, 1])
        hc = modulate(rmsnorm(xc, g_mix_pre[l]), modc[0], modc[1])
        o, oc = token_mixers(h, hc, row, col, w_in[l], sink_a[l], g_qn_b[l], g_kn_b[l],
                             g_cq[l], w_uq[l], g_ckv[l], w_ukv[l], w_out[l], not last)
        x = x + mod[:, 2] * rmsnorm(o, g_mix_post[l])
        h = modulate(rmsnorm(x, g_ffn_pre[l]), mod[:, 3], mod[:, 4])
        x = x + mod[:, 5] * rmsnorm(swiglu(h, w_ffn1[l], w_ffn3[l], w_ffn2[l]), g_ffn_post[l])
        if not last:
            xc = xc + modc[2] * rmsnorm(oc, g_mix_post[l])
            hc = modulate(rmsnorm(xc, g_ffn_pre[l]), modc[3], modc[4])
            xc = xc + modc[5] * rmsnorm(swiglu(hc, w_ffn1[l], w_ffn3[l], w_ffn2[l]), g_ffn_post[l])
    return x
```

```python
import functools
import math

import jax
import jax.numpy as jnp
from jax import lax
from jax.experimental import pallas as pl
from jax.experimental.pallas import tpu as pltpu

GRID_W = 64
HEAD_DIM = 128
A_HEADS, A_KV = 6, 2
B_HEADS, B_KV = 6, 2
WINDOW = 128
C_HEADS = 4
C_Q_LORA, C_KV_LORA = 512, 256
C_NOPE, C_ROPE, C_V = 128, 64, 128
C_QK_PAD = 256
ROPE_THETA = 10000.0
EPS = 1e-6
NEG = -1e30
LOG2E = math.log2(math.e)

LANES = 128
MOD_ROWS = 8
TM = 512
TQ = 256
TF = 512
TN_MOD = 1024
MIB = 1024 * 1024

F32 = jnp.float32
BF16 = jnp.bfloat16


def _rmsnorm(x, g):
    return x * lax.rsqrt(jnp.mean(x * x, axis=-1, keepdims=True) + EPS) * g


def _dot(a, b):
    return jnp.dot(a, b, preferred_element_type=F32)


def _dot_nt(a, b):
    return lax.dot_general(a, b, (((1,), (1,)), ((), ())), preferred_element_type=F32)


def _mod_kernel(c_ref, w_ref, b_ref, o_ref):
    cv = c_ref[...]
    s = cv * jax.nn.sigmoid(cv)
    o_ref[0] = _dot(s.astype(BF16), w_ref[0].astype(BF16)) + b_ref[0]


def _mod_call(cv, w_mod, b_mod):
    depth, d, n = w_mod.shape
    return pl.pallas_call(
        _mod_kernel,
        grid=(depth, n // TN_MOD),
        in_specs=[
            pl.BlockSpec((MOD_ROWS, d), lambda l, j: (0, 0)),
            pl.BlockSpec((1, d, TN_MOD), lambda l, j: (l, 0, j)),
            pl.BlockSpec((1, 1, TN_MOD), lambda l, j: (l, 0, j)),
        ],
        out_specs=pl.BlockSpec((1, MOD_ROWS, TN_MOD), lambda l, j: (l, 0, j)),
        out_shape=jax.ShapeDtypeStruct((depth, MOD_ROWS, n), F32),
        compiler_params=pltpu.CompilerParams(
            dimension_semantics=("arbitrary", "arbitrary"),
            vmem_limit_bytes=40 * MIB),
        name="adaln_mod",
    )(cv, w_mod, b_mod.reshape(depth, 1, n))


def _qkv_kernel(x_ref, mod_ref, gpre_ref, win_ref, cos_ref, sin_ref, cos64_ref, sin64_ref,
                gqn_ref, gkn_ref, gcq_ref, gckv_ref, wuq_ref, wukv_ref,
                qa_ref, ka_ref, va_ref, qb_ref, kb_ref, vb_ref, qc_ref, kc_ref, vc_ref):
    m = mod_ref[0]
    h = _rmsnorm(x_ref[...], gpre_ref[...]) * (1.0 + m[1:2]) + m[0:1]
    p = _dot(h.astype(BF16), win_ref[...])

    cos, sin = cos_ref[...], sin_ref[...]
    cos64, sin64 = cos64_ref[...], sin64_ref[...]
    lane = lax.broadcasted_iota(jnp.int32, (TM, LANES), 1)
    low32 = (lane & 32) == 0
    low16 = (lane & 16) == 0

    def rope128(t):
        partner = jnp.where(low32, pltpu.roll(t, LANES - 32, 1), pltpu.roll(t, 32, 1))
        return t * cos + partner * sin

    def rope64(t):
        partner = jnp.where(low16, pltpu.roll(t, LANES - 16, 1), pltpu.roll(t, 16, 1))
        return t * cos64 + partner * sin64

    def col(i):
        return slice(i * LANES, (i + 1) * LANES)

    off = 0
    for i in range(A_HEADS):
        qa_ref[:, col(i)] = rope128(p[:, col(off + i)]).astype(BF16)
    off += A_HEADS
    for i in range(A_KV):
        ka_ref[:, col(i)] = rope128(p[:, col(off + i)]).astype(BF16)
    off += A_KV
    for i in range(A_KV):
        va_ref[:, col(i)] = p[:, col(off + i)].astype(BF16)
    off += A_KV

    gqn, gkn = gqn_ref[...], gkn_ref[...]
    for i in range(B_HEADS):
        qb_ref[:, col(i)] = rope128(_rmsnorm(p[:, col(off + i)], gqn)).astype(BF16)
    off += B_HEADS
    for i in range(B_KV):
        kb_ref[:, col(i)] = rope128(_rmsnorm(p[:, col(off + i)], gkn)).astype(BF16)
    off += B_KV
    for i in range(B_KV):
        vb_ref[:, col(i)] = p[:, col(off + i)].astype(BF16)
    off += B_KV

    c0 = off * LANES
    cq = _rmsnorm(p[:, c0:c0 + C_Q_LORA], gcq_ref[...])
    q = _dot(cq.astype(BF16), wuq_ref[...])
    c0 += C_Q_LORA
    ckv = _rmsnorm(p[:, c0:c0 + C_KV_LORA], gckv_ref[...])
    kv = _dot(ckv.astype(BF16), wukv_ref[...])
    c0 += C_KV_LORA
    krope = rope64(p[:, c0:c0 + LANES]).astype(BF16)
    for i in range(C_HEADS):
        qc_ref[:, col(2 * i)] = q[:, col(2 * i)].astype(BF16)
        qc_ref[:, col(2 * i + 1)] = rope64(q[:, col(2 * i + 1)]).astype(BF16)
        kc_ref[:, col(2 * i)] = kv[:, col(i)].astype(BF16)
        kc_ref[:, col(2 * i + 1)] = krope
    vc_ref[...] = kv[:, C_HEADS * C_NOPE:].astype(BF16)


def _qkv_call(xa, mod, gpre, win, tabs, gqn, gkn, gcq, gckv, wuq, wukv, *, n_lat_tiles, tiles_per_seq, n_seq):
    t, d = xa.shape
    n_tiles = t // TM
    d_in = win.shape[1]

    def seg(i):
        return jnp.minimum(i // tiles_per_seq, n_seq)

    def rope_blk(i):
        return jnp.where(i < n_lat_tiles, i % tiles_per_seq, tiles_per_seq)

    const = lambda i: (0, 0)
    tab_spec = pl.BlockSpec((TM, LANES), lambda i: (rope_blk(i), 0))
    widths = (A_HEADS * HEAD_DIM, A_KV * HEAD_DIM, A_KV * HEAD_DIM,
              B_HEADS * HEAD_DIM, B_KV * HEAD_DIM, B_KV * HEAD_DIM,
              C_HEADS * C_QK_PAD, C_HEADS * C_QK_PAD, C_HEADS * C_V)
    return pl.pallas_call(
        _qkv_kernel,
        grid=(n_tiles,),
        in_specs=[
            pl.BlockSpec((TM, d), lambda i: (i, 0)),
            pl.BlockSpec((1, MOD_ROWS, d), lambda i: (seg(i), 0, 0)),
            pl.BlockSpec((1, d), const),
            pl.BlockSpec((d, d_in), const, pipeline_mode=pl.Buffered(1)),
            tab_spec, tab_spec, tab_spec, tab_spec,
            pl.BlockSpec((1, HEAD_DIM), const),
            pl.BlockSpec((1, HEAD_DIM), const),
            pl.BlockSpec((1, C_Q_LORA), const),
            pl.BlockSpec((1, C_KV_LORA), const),
            pl.BlockSpec(wuq.shape, const, pipeline_mode=pl.Buffered(1)),
            pl.BlockSpec(wukv.shape, const, pipeline_mode=pl.Buffered(1)),
        ],
        out_specs=[pl.BlockSpec((TM, w), lambda i: (i, 0)) for w in widths],
        out_shape=[jax.ShapeDtypeStruct((t, w), BF16) for w in widths],
        compiler_params=pltpu.CompilerParams(
            dimension_semantics=("arbitrary",),
            vmem_limit_bytes=56 * MIB),
        name="qkv_prep",
    )(xa, mod, gpre, win, *tabs, gqn, gkn, gcq, gckv, wuq, wukv)


def _attn_kernel(*refs, heads_per_group, dk, dv, scale, n_q_lat, seq, window, has_sink):
    if has_sink:
        sink_ref, q_ref, kl_ref, vl_ref, kc_ref, vc_ref, o_ref = refs
    else:
        q_ref, kl_ref, vl_ref, kc_ref, vc_ref, o_ref = refs
    g = pl.program_id(1)
    qi = pl.program_id(2)
    c2 = scale * LOG2E
    band = TQ + 2 * WINDOW

    def one_head(r, latent):
        q = q_ref[:, r * dk:(r + 1) * dk]
        s_c = _dot_nt(q, kc_ref[...])
        m2 = jnp.max(s_c, axis=-1, keepdims=True) * c2
        if latent:
            if window:
                start = pl.multiple_of(jnp.clip(qi * TQ - WINDOW, 0, seq - band), LANES)
                k_l = kl_ref[pl.ds(start, band), :]
                v_l = vl_ref[pl.ds(start, band), :]
                s_l = _dot_nt(q, k_l)
                kpos = start + lax.broadcasted_iota(jnp.int32, (TQ, band), 1)
                qpos = qi * TQ + lax.broadcasted_iota(jnp.int32, (TQ, band), 0)
                s_l = jnp.where(jnp.abs(kpos - qpos) <= WINDOW, s_l, NEG)
            else:
                v_l = vl_ref[...]
                s_l = _dot_nt(q, kl_ref[...])
            m2 = jnp.maximum(m2, jnp.max(s_l, axis=-1, keepdims=True) * c2)
        if has_sink:
            sink2 = sink_ref[g * heads_per_group + r] * LOG2E
            m2 = jnp.maximum(m2, sink2)
        p_c = jnp.exp2(s_c * c2 - m2)
        denom = jnp.sum(p_c, axis=-1, keepdims=True)
        o = _dot(p_c.astype(BF16), vc_ref[...])
        if latent:
            p_l = jnp.exp2(s_l * c2 - m2)
            denom = denom + jnp.sum(p_l, axis=-1, keepdims=True)
            o = o + _dot(p_l.astype(BF16), v_l)
        if has_sink:
            denom = denom + jnp.exp2(sink2 - m2)
        o_ref[:, r * dv:(r + 1) * dv] = (o / denom).astype(BF16)

    @pl.when(qi < n_q_lat)
    def _():
        for r in range(heads_per_group):
            one_head(r, True)

    @pl.when(qi >= n_q_lat)
    def _():
        for r in range(heads_per_group):
            one_head(r, False)


def _attn_call(q, k, v, sink, *, n_groups, heads_per_group, dk, dv, scale, batch, seq, ctx_len,
               window, ctx_queries):
    t = q.shape[0]
    n_q_lat = seq // TQ
    n_q_ctx = ctx_len // TQ if ctx_queries else 0
    ctx_blk0 = batch * seq // ctx_len

    def q_map(b, g, qi):
        lat = b * n_q_lat + qi
        ctx = batch * n_q_lat + b * (ctx_len // TQ) + (qi - n_q_lat)
        return (jnp.where(qi < n_q_lat, lat, ctx), g)

    in_specs = [
        pl.BlockSpec((TQ, heads_per_group * dk), q_map),
        pl.BlockSpec((seq, dk), lambda b, g, qi: (b, g)),
        pl.BlockSpec((seq, dv), lambda b, g, qi: (b, g)),
        pl.BlockSpec((ctx_len, dk), lambda b, g, qi: (ctx_blk0 + b, g)),
        pl.BlockSpec((ctx_len, dv), lambda b, g, qi: (ctx_blk0 + b, g)),
    ]
    args = [q, k, v, k, v]
    has_sink = sink is not None
    if has_sink:
        in_specs = [pl.BlockSpec(memory_space=pltpu.SMEM)] + in_specs
        args = [sink] + args
    t_out = t if ctx_queries else batch * seq
    return pl.pallas_call(
        functools.partial(_attn_kernel, heads_per_group=heads_per_group, dk=dk, dv=dv, scale=scale,
                          n_q_lat=n_q_lat, seq=seq, window=window, has_sink=has_sink),
        grid=(batch, n_groups, n_q_lat + n_q_ctx),
        in_specs=in_specs,
        out_specs=pl.BlockSpec((TQ, heads_per_group * dv), q_map),
        out_shape=jax.ShapeDtypeStruct((t_out, n_groups * heads_per_group * dv), BF16),
        compiler_params=pltpu.CompilerParams(
            dimension_semantics=("arbitrary", "arbitrary", "arbitrary"),
            vmem_limit_bytes=48 * MIB),
        name="attn_window" if window else f"attn_dense_dk{dk}",
    )(*args)


def _outproj_kernel(oa_ref, ob_ref, oc_ref, x_ref, mod_ref, gpost_ref, wo_ref, xo_ref):
    wa, wb = oa_ref.shape[1], ob_ref.shape[1]
    o = _dot(oa_ref[...], wo_ref[0:wa, :])
    o = o + _dot(ob_ref[...], wo_ref[wa:wa + wb, :])
    o = o + _dot(oc_ref[...], wo_ref[wa + wb:, :])
    m = mod_ref[0]
    xo_ref[...] = x_ref[...] + m[2:3] * _rmsnorm(o, gpost_ref[...])


def _outproj_call(oa, ob, oc, xa, mod, gpost, wo, *, tiles_per_seq, n_seq):
    t = oa.shape[0]
    d = xa.shape[1]
    seg = lambda i: jnp.minimum(i // tiles_per_seq, n_seq)
    const = lambda i: (0, 0)
    return pl.pallas_call(
        _outproj_kernel,
        grid=(t // TM,),
        in_specs=[
            pl.BlockSpec((TM, oa.shape[1]), lambda i: (i, 0)),
            pl.BlockSpec((TM, ob.shape[1]), lambda i: (i, 0)),
            pl.BlockSpec((TM, oc.shape[1]), lambda i: (i, 0)),
            pl.BlockSpec((TM, d), lambda i: (i, 0)),
            pl.BlockSpec((1, MOD_ROWS, d), lambda i: (seg(i), 0, 0)),
            pl.BlockSpec((1, d), const),
            pl.BlockSpec(wo.shape, const, pipeline_mode=pl.Buffered(1)),
        ],
        out_specs=pl.BlockSpec((TM, d), lambda i: (i, 0)),
        out_shape=jax.ShapeDtypeStruct((t, d), F32),
        compiler_params=pltpu.CompilerParams(
            dimension_semantics=("arbitrary",),
            vmem_limit_bytes=48 * MIB),
        name="outproj",
    )(oa, ob, oc, xa, mod, gpost, wo)


def _ffn_kernel(x_ref, mod_ref, gpre_ref, gpost_ref, w1_ref, w3_ref, w2_ref, o_ref, h_scr, acc_scr):
    j = pl.program_id(1)

    @pl.when(j == 0)
    def _():
        m = mod_ref[0]
        h = _rmsnorm(x_ref[...], gpre_ref[...]) * (1.0 + m[4:5]) + m[3:4]
        h_scr[...] = h.astype(BF16)
        acc_scr[...] = jnp.zeros_like(acc_scr)

    h = h_scr[...]
    a = _dot(h, w1_ref[...])
    b = _dot(h, w3_ref[...])
    t = a * jax.nn.sigmoid(a) * b
    acc_scr[...] += _dot(t.astype(BF16), w2_ref[...])

    @pl.when(j == pl.num_programs(1) - 1)
    def _():
        m = mod_ref[0]
        o_ref[...] = x_ref[...] + m[5:6] * _rmsnorm(acc_scr[...], gpost_ref[...])


def _ffn_call(xa, mod, gpre, gpost, w1, w3, w2, *, tiles_per_seq, n_seq):
    t, d = xa.shape
    f = w1.shape[1]
    seg = lambda i: jnp.minimum(i // tiles_per_seq, n_seq)
    const = lambda i, j: (0, 0)
    return pl.pallas_call(
        _ffn_kernel,
        grid=(t // TM, f // TF),
        in_specs=[
            pl.BlockSpec((TM, d), lambda i, j: (i, 0)),
            pl.BlockSpec((1, MOD_ROWS, d), lambda i, j: (seg(i), 0, 0)),
            pl.BlockSpec((1, d), const),
            pl.BlockSpec((1, d), const),
            pl.BlockSpec((d, TF), lambda i, j: (0, j)),
            pl.BlockSpec((d, TF), lambda i, j: (0, j)),
            pl.BlockSpec((TF, d), lambda i, j: (j, 0)),
        ],
        out_specs=pl.BlockSpec((TM, d), lambda i, j: (i, 0)),
        out_shape=jax.ShapeDtypeStruct((t, d), F32),
        scratch_shapes=[pltpu.VMEM((TM, d), BF16), pltpu.VMEM((TM, d), F32)],
        compiler_params=pltpu.CompilerParams(
            dimension_semantics=("arbitrary", "arbitrary"),
            vmem_limit_bytes=48 * MIB),
        name="ffn_swiglu",
    )(xa, mod, gpre, gpost, w1, w3, w2)


def _rope_tables(seq):
    pos = jnp.arange(seq, dtype=jnp.int32)
    row = (pos // GRID_W).astype(F32)[:, None]
    colp = (pos % GRID_W).astype(F32)[:, None]

    def cs(n):
        freqs = jnp.power(jnp.float32(ROPE_THETA), -jnp.arange(n, dtype=F32) / n)[None, :]
        return (jnp.cos(row * freqs), jnp.sin(row * freqs), jnp.cos(colp * freqs), jnp.sin(colp * freqs))

    cr, sr, cc, sc = cs(HEAD_DIM // 4)
    cos128 = jnp.concatenate([cr, cr, cc, cc], axis=1)
    sin128 = jnp.concatenate([-sr, sr, -sc, sc], axis=1)
    cr, sr, cc, sc = cs(C_ROPE // 4)
    pad1 = jnp.ones((seq, LANES - C_ROPE), F32)
    pad0 = jnp.zeros((seq, LANES - C_ROPE), F32)
    cos64 = jnp.concatenate([cr, cr, cc, cc, pad1], axis=1)
    sin64 = jnp.concatenate([-sr, sr, -sc, sc, pad0], axis=1)
    ident_c = jnp.ones((TM, LANES), F32)
    ident_s = jnp.zeros((TM, LANES), F32)
    return tuple(jnp.concatenate([tab, ident], axis=0)
                 for tab, ident in ((cos128, ident_c), (sin128, ident_s), (cos64, ident_c), (sin64, ident_s)))


def kernel(x, c, ctx, c_ctx, w_mod, b_mod, g_mix_pre, g_mix_post, g_ffn_pre, g_ffn_post, w_in, sink_a,
           g_qn_b, g_kn_b, g_cq, w_uq, g_ckv, w_ukv, w_out, w_ffn1, w_ffn3, w_ffn2):
    batch, seq, d = x.shape
    ctx_len = ctx.shape[1]
    depth = w_mod.shape[0]
    n_lat, n_ctx = batch * seq, batch * ctx_len
    assert seq % TM == 0 and n_ctx % TM == 0 and seq % ctx_len == 0 and ctx_len % TQ == 0
    assert batch + 1 <= MOD_ROWS and seq % GRID_W == 0
    tiles_per_seq = seq // TM
    n_lat_tiles = n_lat // TM

    cv = jnp.zeros((MOD_ROWS, d), F32).at[:batch].set(c).at[batch].set(c_ctx)
    mod_all = _mod_call(cv, w_mod, b_mod)
    mods = mod_all[:, :batch + 1].reshape(depth, batch + 1, 6, d)
    mods = jnp.pad(mods, ((0, 0), (0, 0), (0, MOD_ROWS - 6), (0, 0)))

    tabs = _rope_tables(seq)

    d_in = w_in.shape[2]
    d_in_pad = -(-d_in // LANES) * LANES
    win = jnp.pad(w_in, ((0, 0), (0, 0), (0, d_in_pad - d_in))).astype(BF16)
    wuq = w_uq.reshape(depth, C_Q_LORA, C_HEADS, C_NOPE + C_ROPE)
    wuq = jnp.pad(wuq, ((0, 0), (0, 0), (0, 0), (0, C_QK_PAD - C_NOPE - C_ROPE)))
    wuq = wuq.reshape(depth, C_Q_LORA, C_HEADS * C_QK_PAD).astype(BF16)
    wukv = w_ukv.reshape(depth, C_KV_LORA, C_HEADS, 2, C_NOPE).transpose(0, 1, 3, 2, 4)
    wukv = wukv.reshape(depth, C_KV_LORA, 2 * C_HEADS * C_NOPE).astype(BF16)
    wo = w_out.astype(BF16)
    w1, w3, w2 = w_ffn1.astype(BF16), w_ffn3.astype(BF16), w_ffn2.astype(BF16)

    xa = jnp.concatenate([x.reshape(n_lat, d), ctx.reshape(n_ctx, d)], axis=0)
    sc_ab = 1.0 / math.sqrt(HEAD_DIM)
    sc_c = 1.0 / math.sqrt(C_NOPE + C_ROPE)
    row = lambda v: v.reshape(1, -1)

    for l in range(depth):
        last = l == depth - 1
        qa, ka, va, qb, kb, vb, qc, kc, vc = _qkv_call(
            xa, mods[l], row(g_mix_pre[l]), win[l], tabs, row(g_qn_b[l]), row(g_kn_b[l]),
            row(g_cq[l]), row(g_ckv[l]), wuq[l], wukv[l],
            n_lat_tiles=n_lat_tiles, tiles_per_seq=tiles_per_seq, n_seq=batch)
        common = dict(batch=batch, seq=seq, ctx_len=ctx_len, ctx_queries=not last)
        oa = _attn_call(qa, ka, va, sink_a[l], n_groups=A_KV, heads_per_group=A_HEADS // A_KV,
                        dk=HEAD_DIM, dv=HEAD_DIM, scale=sc_ab, window=True, **common)
        ob = _attn_call(qb, kb, vb, None, n_groups=B_KV, heads_per_group=B_HEADS // B_KV,
                        dk=HEAD_DIM, dv=HEAD_DIM, scale=sc_ab, window=False, **common)
        oc = _attn_call(qc, kc, vc, None, n_groups=C_HEADS, heads_per_group=1,
                        dk=C_QK_PAD, dv=C_V, scale=sc_c, window=False, **common)
        xa = _outproj_call(oa, ob, oc, xa, mods[l], row(g_mix_post[l]), wo[l],
                           tiles_per_seq=tiles_per_seq, n_seq=batch)
        xa = _ffn_call(xa, mods[l], row(g_ffn_pre[l]), row(g_ffn_post[l]), w1[l], w3[l], w2[l],
                       tiles_per_seq=tiles_per_seq, n_seq=batch)
    return xa.reshape(batch, seq, d)
```

```python
import functools
import math

import jax
import jax.numpy as jnp
from jax import lax
from jax.experimental import pallas as pl
from jax.experimental.pallas import tpu as pltpu

GRID_W = 64
HEAD_DIM = 128
A_HEADS, A_KV = 6, 2
B_HEADS, B_KV = 6, 2
WINDOW = 128
C_HEADS = 4
C_Q_LORA, C_KV_LORA = 512, 256
C_NOPE, C_ROPE, C_V = 128, 64, 128
C_QK_PAD = 256
ROPE_THETA = 10000.0
EPS = 1e-6
NEG = -1e30
LOG2E = math.log2(math.e)

LANES = 128
MOD_ROWS = 8
TM = 512
TQ = 256
TQ_GQA = 256
TQ_MLA = 1024
TK = 512
TF = 512
TN_MOD = 1024
MIB = 1024 * 1024

F32 = jnp.float32
BF16 = jnp.bfloat16


def _rmsnorm(x, g):
    return x * lax.rsqrt(jnp.mean(x * x, axis=-1, keepdims=True) + EPS) * g


def _dot(a, b):
    return jnp.dot(a, b, preferred_element_type=F32)


def _dot_nt(a, b):
    return lax.dot_general(a, b, (((1,), (1,)), ((), ())), preferred_element_type=F32)


def _col(i):
    return slice(i * LANES, (i + 1) * LANES)


def _mod_kernel(c_ref, w_ref, b_ref, o_ref):
    cv = c_ref[...]
    s = cv * jax.nn.sigmoid(cv)
    o_ref[...] = _dot(s.astype(BF16), w_ref[...].astype(BF16)) + b_ref[...]


def _mod_call(cv, w_mod, b_mod):
    depth, d, n = w_mod.shape
    return pl.pallas_call(
        _mod_kernel,
        grid=(depth, n // TN_MOD),
        in_specs=[
            pl.BlockSpec((MOD_ROWS, d), lambda l, j: (0, 0)),
            pl.BlockSpec((None, d, TN_MOD), lambda l, j: (l, 0, j)),
            pl.BlockSpec((None, 1, TN_MOD), lambda l, j: (l, 0, j)),
        ],
        out_specs=pl.BlockSpec((None, MOD_ROWS, TN_MOD), lambda l, j: (l, 0, j)),
        out_shape=jax.ShapeDtypeStruct((depth, MOD_ROWS, n), F32),
        compiler_params=pltpu.CompilerParams(
            dimension_semantics=("arbitrary", "arbitrary"),
            vmem_limit_bytes=40 * MIB),
        name="adaln_mod",
    )(cv, w_mod, b_mod.reshape(depth, 1, n))


def _qkv_kernel(x_ref, mod_ref, gpre_ref, win_ref, cos_ref, sin_ref, cos64_ref, sin64_ref,
                gqn_ref, gkn_ref, gcq_ref, gckv_ref, wuq_ref, wukv_ref,
                qa_ref, ka_ref, va_ref, qb_ref, kb_ref, vtb_ref, qc_ref, kc_ref, vtc_ref):
    m = mod_ref[...]
    h = _rmsnorm(x_ref[...], gpre_ref[...]) * (1.0 + m[1:2]) + m[0:1]
    p = _dot(h.astype(BF16), win_ref[...])

    cos, sin = cos_ref[...], sin_ref[...]
    cos64, sin64 = cos64_ref[...], sin64_ref[...]
    lane = lax.broadcasted_iota(jnp.int32, (TM, LANES), 1)
    low32 = (lane & 32) == 0
    low16 = (lane & 16) == 0

    def rope128(t):
        partner = jnp.where(low32, pltpu.roll(t, LANES - 32, 1), pltpu.roll(t, 32, 1))
        return t * cos + partner * sin

    def rope64(t):
        partner = jnp.where(low16, pltpu.roll(t, LANES - 16, 1), pltpu.roll(t, 16, 1))
        return t * cos64 + partner * sin64

    off = 0
    for i in range(A_HEADS):
        qa_ref[:, _col(i)] = rope128(p[:, _col(off + i)]).astype(BF16)
    off += A_HEADS
    for i in range(A_KV):
        ka_ref[:, _col(i)] = rope128(p[:, _col(off + i)]).astype(BF16)
    off += A_KV
    for i in range(A_KV):
        va_ref[:, _col(i)] = p[:, _col(off + i)].astype(BF16)
    off += A_KV

    gqn, gkn = gqn_ref[...], gkn_ref[...]
    for i in range(B_HEADS):
        qb_ref[:, _col(i)] = rope128(_rmsnorm(p[:, _col(off + i)], gqn)).astype(BF16)
    off += B_HEADS
    for i in range(B_KV):
        kb_ref[:, _col(i)] = rope128(_rmsnorm(p[:, _col(off + i)], gkn)).astype(BF16)
    off += B_KV
    for i in range(B_KV):
        vtb_ref[_col(i), :] = p[:, _col(off + i)].T.astype(BF16)
    off += B_KV

    c0 = off * LANES
    cq = _rmsnorm(p[:, c0:c0 + C_Q_LORA], gcq_ref[...])
    q = _dot(cq.astype(BF16), wuq_ref[...])
    c0 += C_Q_LORA
    ckv = _rmsnorm(p[:, c0:c0 + C_KV_LORA], gckv_ref[...])
    kv = _dot(ckv.astype(BF16), wukv_ref[...])
    c0 += C_KV_LORA
    krope = rope64(p[:, c0:c0 + LANES]).astype(BF16)
    for i in range(C_HEADS):
        qc_ref[:, _col(2 * i)] = q[:, _col(2 * i)].astype(BF16)
        qc_ref[:, _col(2 * i + 1)] = rope64(q[:, _col(2 * i + 1)]).astype(BF16)
        kc_ref[:, _col(2 * i)] = kv[:, _col(i)].astype(BF16)
        kc_ref[:, _col(2 * i + 1)] = krope
        vtc_ref[_col(i), :] = kv[:, _col(C_HEADS + i)].T.astype(BF16)


def _qkv_call(l, xa, mods, gpre, win, tabs, gqn, gkn, gcq, gckv, wuq, wukv, *, n_lat_tiles, tiles_per_seq, n_seq):
    t, d = xa.shape
    n_tiles = t // TM
    d_in = win.shape[2]

    def seg(i):
        return jnp.minimum(i // tiles_per_seq, n_seq)

    def rope_blk(i):
        return jnp.where(i < n_lat_tiles, i % tiles_per_seq, tiles_per_seq)

    layer = lambda i: (l, 0, 0)
    vec = lambda n: pl.BlockSpec((None, 1, n), layer)
    tab_spec = pl.BlockSpec((TM, LANES), lambda i: (rope_blk(i), 0))
    row_major = lambda w: (pl.BlockSpec((TM, w), lambda i: (i, 0)), jax.ShapeDtypeStruct((t, w), BF16))
    col_major = lambda w: (pl.BlockSpec((w, TM), lambda i: (0, i)), jax.ShapeDtypeStruct((w, t), BF16))
    outs = [row_major(A_HEADS * HEAD_DIM), row_major(A_KV * HEAD_DIM), row_major(A_KV * HEAD_DIM),
            row_major(B_HEADS * HEAD_DIM), row_major(B_KV * HEAD_DIM), col_major(B_KV * HEAD_DIM),
            row_major(C_HEADS * C_QK_PAD), row_major(C_HEADS * C_QK_PAD), col_major(C_HEADS * C_V)]
    return pl.pallas_call(
        _qkv_kernel,
        grid=(n_tiles,),
        in_specs=[
            pl.BlockSpec((TM, d), lambda i: (i, 0)),
            pl.BlockSpec((None, None, MOD_ROWS, d), lambda i: (l, seg(i), 0, 0)),
            vec(d),
            pl.BlockSpec((None, d, d_in), layer, pipeline_mode=pl.Buffered(1)),
            tab_spec, tab_spec, tab_spec, tab_spec,
            vec(HEAD_DIM), vec(HEAD_DIM), vec(C_Q_LORA), vec(C_KV_LORA),
            pl.BlockSpec((None,) + wuq.shape[1:], layer, pipeline_mode=pl.Buffered(1)),
            pl.BlockSpec((None,) + wukv.shape[1:], layer, pipeline_mode=pl.Buffered(1)),
        ],
        out_specs=[o[0] for o in outs],
        out_shape=[o[1] for o in outs],
        compiler_params=pltpu.CompilerParams(
            dimension_semantics=("arbitrary",),
            vmem_limit_bytes=56 * MIB),
        name="qkv_prep",
    )(xa, mods, gpre, win, *tabs, gqn, gkn, gcq, gckv, wuq, wukv)


def _attn_window_kernel(sink_ref, q_ref, kl_ref, vl_ref, kc_ref, vc_ref, o_ref, *,
                        layer, heads_per_group, dk, dv, scale, n_q_lat, seq):
    g = pl.program_id(1)
    qi = pl.program_id(2)
    c2 = scale * LOG2E
    band = TQ + 2 * WINDOW

    def one_head(r, latent):
        q = q_ref[:, r * dk:(r + 1) * dk]
        s_c = _dot_nt(q, kc_ref[...])
        m2 = jnp.max(s_c, axis=-1, keepdims=True) * c2
        if latent:
            start = pl.multiple_of(jnp.clip(qi * TQ - WINDOW, 0, seq - band), LANES)
            k_l = kl_ref[pl.ds(start, band), :]
            v_l = vl_ref[pl.ds(start, band), :]
            s_l = _dot_nt(q, k_l)
            kpos = start + lax.broadcasted_iota(jnp.int32, (TQ, band), 1)
            qpos = qi * TQ + lax.broadcasted_iota(jnp.int32, (TQ, band), 0)
            s_l = jnp.where(jnp.abs(kpos - qpos) <= WINDOW, s_l, NEG)
            m2 = jnp.maximum(m2, jnp.max(s_l, axis=-1, keepdims=True) * c2)
        sink2 = sink_ref[layer, g * heads_per_group + r] * LOG2E
        m2 = jnp.maximum(m2, sink2)
        p_c = jnp.exp2(s_c * c2 - m2)
        denom = jnp.sum(p_c, axis=-1, keepdims=True) + jnp.exp2(sink2 - m2)
        o = _dot(p_c.astype(BF16), vc_ref[...])
        if latent:
            p_l = jnp.exp2(s_l * c2 - m2)
            denom = denom + jnp.sum(p_l, axis=-1, keepdims=True)
            o = o + _dot(p_l.astype(BF16), v_l)
        o_ref[:, r * dv:(r + 1) * dv] = (o / denom).astype(BF16)

    @pl.when(qi < n_q_lat)
    def _():
        for r in range(heads_per_group):
            one_head(r, True)

    @pl.when(qi >= n_q_lat)
    def _():
        for r in range(heads_per_group):
            one_head(r, False)


def _attn_window_call(l, q, k, v, sink, *, n_groups, heads_per_group, dk, dv, scale, batch, seq, ctx_len,
                      ctx_queries):
    t = q.shape[0]
    n_q_lat = seq // TQ
    n_q_ctx = ctx_len // TQ if ctx_queries else 0
    ctx_blk0 = batch * seq // ctx_len

    def q_map(b, g, qi):
        lat = b * n_q_lat + qi
        ctx = batch * n_q_lat + b * (ctx_len // TQ) + (qi - n_q_lat)
        return (jnp.where(qi < n_q_lat, lat, ctx), g)

    t_out = t if ctx_queries else batch * seq
    return pl.pallas_call(
        functools.partial(_attn_window_kernel, layer=l, heads_per_group=heads_per_group, dk=dk, dv=dv,
                          scale=scale, n_q_lat=n_q_lat, seq=seq),
        grid=(batch, n_groups, n_q_lat + n_q_ctx),
        in_specs=[
            pl.BlockSpec(memory_space=pltpu.SMEM),
            pl.BlockSpec((TQ, heads_per_group * dk), q_map),
            pl.BlockSpec((seq, dk), lambda b, g, qi: (b, g)),
            pl.BlockSpec((seq, dv), lambda b, g, qi: (b, g)),
            pl.BlockSpec((ctx_len, dk), lambda b, g, qi: (ctx_blk0 + b, g)),
            pl.BlockSpec((ctx_len, dv), lambda b, g, qi: (ctx_blk0 + b, g)),
        ],
        out_specs=pl.BlockSpec((TQ, heads_per_group * dv), q_map),
        out_shape=jax.ShapeDtypeStruct((t_out, n_groups * heads_per_group * dv), BF16),
        compiler_params=pltpu.CompilerParams(
            dimension_semantics=("arbitrary", "arbitrary", "arbitrary"),
            vmem_limit_bytes=48 * MIB),
        name="attn_window",
    )(sink, q, k, v, k, v)


def _attn_dense_kernel(q_ref, kl_ref, vtl_ref, kc_ref, vtc_ref, o_ref, *, heads_per_group, dk, dv, scale, tq):
    c2 = scale * LOG2E
    seq = kl_ref.shape[0]
    if heads_per_group > 1:
        q = jnp.concatenate([q_ref[:, r * dk:(r + 1) * dk] for r in range(heads_per_group)], axis=0)
    else:
        q = q_ref[...]
    chunks = [(lambda: kc_ref[...], lambda: vtc_ref[...])]
    for j in range(seq // TK):
        chunks.append((functools.partial(lambda j: kl_ref[j * TK:(j + 1) * TK, :], j),
                       functools.partial(lambda j: vtl_ref[:, j * TK:(j + 1) * TK], j)))

    def scores(j):
        return _dot_nt(chunks[j][0](), q)

    n = len(chunks)
    s_next = scores(0)
    m = denom = acc = None
    for j in range(n):
        s = s_next
        if j + 1 < n:
            s_next = scores(j + 1)
        m_j = jnp.max(s, axis=0, keepdims=True) * c2
        m_new = m_j if j == 0 else jnp.maximum(m, m_j)
        p = jnp.exp2(s * c2 - m_new)
        pv = _dot(chunks[j][1](), p.astype(BF16))
        if j == 0:
            denom = jnp.sum(p, axis=0, keepdims=True)
            acc = pv
        else:
            alpha = jnp.exp2(m - m_new)
            denom = denom * alpha + jnp.sum(p, axis=0, keepdims=True)
            acc = acc * alpha + pv
        m = m_new
    o = acc / denom
    for r in range(heads_per_group):
        o_ref[:, r * dv:(r + 1) * dv] = o[:, r * tq:(r + 1) * tq].T.astype(BF16)


def _attn_dense_call(q, k, vt, *, n_groups, heads_per_group, dk, dv, scale, batch, seq, ctx_len, tq, t_out):
    n_q = seq // tq
    ctx_blk0 = batch * seq // ctx_len
    return pl.pallas_call(
        functools.partial(_attn_dense_kernel, heads_per_group=heads_per_group, dk=dk, dv=dv, scale=scale, tq=tq),
        grid=(batch, n_groups, n_q),
        in_specs=[
            pl.BlockSpec((tq, heads_per_group * dk), lambda b, g, qi: (b * n_q + qi, g)),
            pl.BlockSpec((seq, dk), lambda b, g, qi: (b, g)),
            pl.BlockSpec((dv, seq), lambda b, g, qi: (g, b)),
            pl.BlockSpec((ctx_len, dk), lambda b, g, qi: (ctx_blk0 + b, g)),
            pl.BlockSpec((dv, ctx_len), lambda b, g, qi: (g, ctx_blk0 + b)),
        ],
        out_specs=pl.BlockSpec((tq, heads_per_group * dv), lambda b, g, qi: (b * n_q + qi, g)),
        out_shape=jax.ShapeDtypeStruct((t_out, n_groups * heads_per_group * dv), BF16),
        compiler_params=pltpu.CompilerParams(
            dimension_semantics=("arbitrary", "arbitrary", "arbitrary"),
            vmem_limit_bytes=48 * MIB),
        name=f"attn_dense_dk{dk}",
    )(q, k, vt, k, vt)


def _attn_ctx_kernel(qb_ref, kb_ref, vtb_ref, qc_ref, kc_ref, vtc_ref, ob_in, oc_in, ob_ref, oc_ref, *,
                     scale_b, scale_c):
    del ob_in, oc_in

    def head(q, k, vt, scale):
        c2 = scale * LOG2E
        s = _dot_nt(q, k)
        p = jnp.exp2((s - jnp.max(s, axis=-1, keepdims=True)) * c2)
        o = _dot_nt(p.astype(BF16), vt)
        return (o / jnp.sum(p, axis=-1, keepdims=True)).astype(BF16)

    rep = B_HEADS // B_KV
    for hd in range(B_HEADS):
        g = hd // rep
        ob_ref[:, _col(hd)] = head(qb_ref[:, _col(hd)], kb_ref[:, _col(g)], vtb_ref[_col(g), :], scale_b)
    for hd in range(C_HEADS):
        qk = slice(hd * C_QK_PAD, (hd + 1) * C_QK_PAD)
        oc_ref[:, _col(hd)] = head(qc_ref[:, qk], kc_ref[:, qk], vtc_ref[_col(hd), :], scale_c)


def _attn_ctx_call(qb, kb, vtb, qc, kc, vtc, ob, oc, *, scale_b, scale_c, batch, seq, ctx_len):
    blk0 = batch * seq // ctx_len
    rows = lambda w: pl.BlockSpec((ctx_len, w), lambda b: (blk0 + b, 0))
    cols = lambda w: pl.BlockSpec((w, ctx_len), lambda b: (0, blk0 + b))
    return pl.pallas_call(
        functools.partial(_attn_ctx_kernel, scale_b=scale_b, scale_c=scale_c),
        grid=(batch,),
        in_specs=[rows(qb.shape[1]), rows(kb.shape[1]), cols(vtb.shape[0]),
                  rows(qc.shape[1]), rows(kc.shape[1]), cols(vtc.shape[0]),
                  pl.BlockSpec(memory_space=pl.ANY), pl.BlockSpec(memory_space=pl.ANY)],
        out_specs=[rows(ob.shape[1]), rows(oc.shape[1])],
        out_shape=[jax.ShapeDtypeStruct(ob.shape, ob.dtype), jax.ShapeDtypeStruct(oc.shape, oc.dtype)],
        input_output_aliases={6: 0, 7: 1},
        compiler_params=pltpu.CompilerParams(
            dimension_semantics=("arbitrary",),
            vmem_limit_bytes=32 * MIB),
        name="attn_ctx",
    )(qb, kb, vtb, qc, kc, vtc, ob, oc)


def _outproj_kernel(oa_ref, ob_ref, oc_ref, x_ref, mod_ref, gpost_ref, wo_ref, xo_ref):
    wa, wb = oa_ref.shape[1], ob_ref.shape[1]
    o = _dot(oa_ref[...], wo_ref[0:wa, :])
    o = o + _dot(ob_ref[...], wo_ref[wa:wa + wb, :])
    o = o + _dot(oc_ref[...], wo_ref[wa + wb:, :])
    m = mod_ref[...]
    xo_ref[...] = x_ref[...] + m[2:3] * _rmsnorm(o, gpost_ref[...])


def _outproj_call(l, oa, ob, oc, xa, mods, gpost, wo, *, tiles_per_seq, n_seq):
    t = oa.shape[0]
    d = xa.shape[1]
    seg = lambda i: jnp.minimum(i // tiles_per_seq, n_seq)
    layer = lambda i: (l, 0, 0)
    return pl.pallas_call(
        _outproj_kernel,
        grid=(t // TM,),
        in_specs=[
            pl.BlockSpec((TM, oa.shape[1]), lambda i: (i, 0)),
            pl.BlockSpec((TM, ob.shape[1]), lambda i: (i, 0)),
            pl.BlockSpec((TM, oc.shape[1]), lambda i: (i, 0)),
            pl.BlockSpec((TM, d), lambda i: (i, 0)),
            pl.BlockSpec((None, None, MOD_ROWS, d), lambda i: (l, seg(i), 0, 0)),
            pl.BlockSpec((None, 1, d), layer),
            pl.BlockSpec((None,) + wo.shape[1:], layer, pipeline_mode=pl.Buffered(1)),
        ],
        out_specs=pl.BlockSpec((TM, d), lambda i: (i, 0)),
        out_shape=jax.ShapeDtypeStruct((t, d), F32),
        compiler_params=pltpu.CompilerParams(
            dimension_semantics=("arbitrary",),
            vmem_limit_bytes=48 * MIB),
        name="outproj",
    )(oa, ob, oc, xa, mods, gpost, wo)


def _ffn_kernel(x_ref, mod_ref, gpre_ref, gpost_ref, w1_ref, w3_ref, w2_ref, o_ref, h_scr, acc_scr):
    j = pl.program_id(1)

    @pl.when(j == 0)
    def _():
        m = mod_ref[...]
        h = _rmsnorm(x_ref[...], gpre_ref[...]) * (1.0 + m[4:5]) + m[3:4]
        h_scr[...] = h.astype(BF16)
        acc_scr[...] = jnp.zeros_like(acc_scr)

    h = h_scr[...]
    a = _dot(h, w1_ref[...])
    b = _dot(h, w3_ref[...])
    t = a * jax.nn.sigmoid(a) * b
    acc_scr[...] += _dot(t.astype(BF16), w2_ref[...])

    @pl.when(j == pl.num_programs(1) - 1)
    def _():
        m = mod_ref[...]
        o_ref[...] = x_ref[...] + m[5:6] * _rmsnorm(acc_scr[...], gpost_ref[...])


def _ffn_call(l, xa, mods, gpre, gpost, w1, w3, w2, *, tiles_per_seq, n_seq):
    t, d = xa.shape
    f = w1.shape[2]
    seg = lambda i: jnp.minimum(i // tiles_per_seq, n_seq)
    layer = lambda i, j: (l, 0, 0)
    return pl.pallas_call(
        _ffn_kernel,
        grid=(t // TM, f // TF),
        in_specs=[
            pl.BlockSpec((TM, d), lambda i, j: (i, 0)),
            pl.BlockSpec((None, None, MOD_ROWS, d), lambda i, j: (l, seg(i), 0, 0)),
            pl.BlockSpec((None, 1, d), layer),
            pl.BlockSpec((None, 1, d), layer),
            pl.BlockSpec((None, d, TF), lambda i, j: (l, 0, j)),
            pl.BlockSpec((None, d, TF), lambda i, j: (l, 0, j)),
            pl.BlockSpec((None, TF, d), lambda i, j: (l, j, 0)),
        ],
        out_specs=pl.BlockSpec((TM, d), lambda i, j: (i, 0)),
        out_shape=jax.ShapeDtypeStruct((t, d), F32),
        scratch_shapes=[pltpu.VMEM((TM, d), BF16), pltpu.VMEM((TM, d), F32)],
        compiler_params=pltpu.CompilerParams(
            dimension_semantics=("arbitrary", "arbitrary"),
            vmem_limit_bytes=48 * MIB),
        name="ffn_swiglu",
    )(xa, mods, gpre, gpost, w1, w3, w2)


def _rope_tables(seq):
    pos = jnp.arange(seq, dtype=jnp.int32)
    row = (pos // GRID_W).astype(F32)[:, None]
    colp = (pos % GRID_W).astype(F32)[:, None]

    def cs(n):
        freqs = jnp.power(jnp.float32(ROPE_THETA), -jnp.arange(n, dtype=F32) / n)[None, :]
        return (jnp.cos(row * freqs), jnp.sin(row * freqs), jnp.cos(colp * freqs), jnp.sin(colp * freqs))

    cr, sr, cc, sc = cs(HEAD_DIM // 4)
    cos128 = jnp.concatenate([cr, cr, cc, cc], axis=1)
    sin128 = jnp.concatenate([-sr, sr, -sc, sc], axis=1)
    cr, sr, cc, sc = cs(C_ROPE // 4)
    pad1 = jnp.ones((seq, LANES - C_ROPE), F32)
    pad0 = jnp.zeros((seq, LANES - C_ROPE), F32)
    cos64 = jnp.concatenate([cr, cr, cc, cc, pad1], axis=1)
    sin64 = jnp.concatenate([-sr, sr, -sc, sc, pad0], axis=1)
    ident_c = jnp.ones((TM, LANES), F32)
    ident_s = jnp.zeros((TM, LANES), F32)
    return tuple(jnp.concatenate([tab, ident], axis=0)
                 for tab, ident in ((cos128, ident_c), (sin128, ident_s), (cos64, ident_c), (sin64, ident_s)))


def kernel(x, c, ctx, c_ctx, w_mod, b_mod, g_mix_pre, g_mix_post, g_ffn_pre, g_ffn_post, w_in, sink_a,
           g_qn_b, g_kn_b, g_cq, w_uq, g_ckv, w_ukv, w_out, w_ffn1, w_ffn3, w_ffn2):
    batch, seq, d = x.shape
    ctx_len = ctx.shape[1]
    depth = w_mod.shape[0]
    n_lat, n_ctx = batch * seq, batch * ctx_len
    assert seq % TM == 0 and n_ctx % TM == 0 and seq % ctx_len == 0 and ctx_len % TQ == 0
    assert seq % TQ_GQA == 0 and seq % TQ_MLA == 0 and seq % TK == 0 and ctx_len % LANES == 0
    assert batch + 1 <= MOD_ROWS and seq % GRID_W == 0
    tiles_per_seq = seq // TM
    n_lat_tiles = n_lat // TM

    cv = jnp.zeros((MOD_ROWS, d), F32).at[:batch].set(c).at[batch].set(c_ctx)
    mod_all = _mod_call(cv, w_mod, b_mod)
    mods = mod_all[:, :batch + 1].reshape(depth, batch + 1, 6, d)
    mods = jnp.pad(mods, ((0, 0), (0, 0), (0, MOD_ROWS - 6), (0, 0)))

    tabs = _rope_tables(seq)

    d_in = w_in.shape[2]
    d_in_pad = -(-d_in // LANES) * LANES
    win = jnp.pad(w_in, ((0, 0), (0, 0), (0, d_in_pad - d_in))).astype(BF16)
    wuq = w_uq.reshape(depth, C_Q_LORA, C_HEADS, C_NOPE + C_ROPE)
    wuq = jnp.pad(wuq, ((0, 0), (0, 0), (0, 0), (0, C_QK_PAD - C_NOPE - C_ROPE)))
    wuq = wuq.reshape(depth, C_Q_LORA, C_HEADS * C_QK_PAD).astype(BF16)
    wukv = w_ukv.reshape(depth, C_KV_LORA, C_HEADS, 2, C_NOPE).transpose(0, 1, 3, 2, 4)
    wukv = wukv.reshape(depth, C_KV_LORA, 2 * C_HEADS * C_NOPE).astype(BF16)
    wo = w_out.astype(BF16)
    w1, w3, w2 = w_ffn1.astype(BF16), w_ffn3.astype(BF16), w_ffn2.astype(BF16)
    vecs = lambda v: v.reshape(depth, 1, -1)
    g_mix_pre, g_mix_post, g_ffn_pre, g_ffn_post = map(vecs, (g_mix_pre, g_mix_post, g_ffn_pre, g_ffn_post))
    g_qn_b, g_kn_b, g_cq, g_ckv = map(vecs, (g_qn_b, g_kn_b, g_cq, g_ckv))

    xa = jnp.concatenate([x.reshape(n_lat, d), ctx.reshape(n_ctx, d)], axis=0)
    sc_ab = 1.0 / math.sqrt(HEAD_DIM)
    sc_c = 1.0 / math.sqrt(C_NOPE + C_ROPE)
    dims = dict(batch=batch, seq=seq, ctx_len=ctx_len)

    for l in range(depth):
        last = l == depth - 1
        qa, ka, va, qb, kb, vtb, qc, kc, vtc = _qkv_call(
            l, xa, mods, g_mix_pre, win, tabs, g_qn_b, g_kn_b, g_cq, g_ckv, wuq, wukv,
            n_lat_tiles=n_lat_tiles, tiles_per_seq=tiles_per_seq, n_seq=batch)
        t_out = n_lat if last else n_lat + n_ctx
        oa = _attn_window_call(l, qa, ka, va, sink_a, n_groups=A_KV, heads_per_group=A_HEADS // A_KV,
                               dk=HEAD_DIM, dv=HEAD_DIM, scale=sc_ab, ctx_queries=not last, **dims)
        ob = _attn_dense_call(qb, kb, vtb, n_groups=B_KV, heads_per_group=B_HEADS // B_KV, dk=HEAD_DIM,
                              dv=HEAD_DIM, scale=sc_ab, tq=TQ_GQA, t_out=t_out, **dims)
        oc = _attn_dense_call(qc, kc, vtc, n_groups=C_HEADS, heads_per_group=1, dk=C_QK_PAD,
                              dv=C_V, scale=sc_c, tq=TQ_MLA, t_out=t_out, **dims)
        if not last:
            ob, oc = _attn_ctx_call(qb, kb, vtb, qc, kc, vtc, ob, oc, scale_b=sc_ab, scale_c=sc_c, **dims)
        xa = _outproj_call(l, oa, ob, oc, xa, mods, g_mix_post, wo, tiles_per_seq=tiles_per_seq, n_seq=batch)
        xa = _ffn_call(l, xa, mods, g_ffn_pre, g_ffn_post, w1, w3, w2, tiles_per_seq=tiles_per_seq, n_seq=batch)
    return xa.reshape(batch, seq, d)
```

```python
import functools
import math

import jax
import jax.numpy as jnp
from jax import lax
from jax.experimental import pallas as pl
from jax.experimental.pallas import tpu as pltpu

GRID_W = 64
HEAD_DIM = 128
A_HEADS, A_KV = 6, 2
B_HEADS, B_KV = 6, 2
WINDOW = 128
C_HEADS = 4
C_Q_LORA, C_KV_LORA = 512, 256
C_NOPE, C_ROPE, C_V = 128, 64, 128
C_QK_PAD = 256
ROPE_THETA = 10000.0
EPS = 1e-6
NEG = -1e30
LOG2E = math.log2(math.e)
SAFE_SHIFT_LOG2 = 50.0
BOUND_SLACK = 1.001

LANES = 128
MOD_ROWS = 8
TM = 512
TQ = 256
TQ_GQA = 256
TQ_MLA = 1024
TK = 512
TF = 512
TN_MOD = 1024
MIB = 1024 * 1024

F32 = jnp.float32
BF16 = jnp.bfloat16


def _rmsnorm(x, g):
    return x * lax.rsqrt(jnp.mean(x * x, axis=-1, keepdims=True) + EPS) * g


def _dot(a, b):
    return jnp.dot(a, b, preferred_element_type=F32)


def _dot_nt(a, b):
    return lax.dot_general(a, b, (((1,), (1,)), ((), ())), preferred_element_type=F32)


def _col(i):
    return slice(i * LANES, (i + 1) * LANES)


def _mod_kernel(c_ref, w_ref, b_ref, o_ref):
    cv = c_ref[...]
    s = cv * jax.nn.sigmoid(cv)
    o_ref[...] = _dot(s.astype(BF16), w_ref[...].astype(BF16)) + b_ref[...]


def _mod_call(cv, w_mod, b_mod):
    depth, d, n = w_mod.shape
    return pl.pallas_call(
        _mod_kernel,
        grid=(depth, n // TN_MOD),
        in_specs=[
            pl.BlockSpec((MOD_ROWS, d), lambda l, j: (0, 0)),
            pl.BlockSpec((None, d, TN_MOD), lambda l, j: (l, 0, j)),
            pl.BlockSpec((None, 1, TN_MOD), lambda l, j: (l, 0, j)),
        ],
        out_specs=pl.BlockSpec((None, MOD_ROWS, TN_MOD), lambda l, j: (l, 0, j)),
        out_shape=jax.ShapeDtypeStruct((depth, MOD_ROWS, n), F32),
        compiler_params=pltpu.CompilerParams(
            dimension_semantics=("arbitrary", "arbitrary"),
            vmem_limit_bytes=40 * MIB),
        name="adaln_mod",
    )(cv, w_mod, b_mod.reshape(depth, 1, n))


def _qkv_kernel(x_ref, mod_ref, gpre_ref, win_ref, cos_ref, sin_ref, cos64_ref, sin64_ref,
                gqn_ref, gkn_ref, gcq_ref, gckv_ref, wuq_ref, wukv_ref,
                qa_ref, ka_ref, va_ref, qb_ref, kb_ref, vtb_ref, qc_ref, kc_ref, vtc_ref):
    m = mod_ref[...]
    h = _rmsnorm(x_ref[...], gpre_ref[...]) * (1.0 + m[1:2]) + m[0:1]
    p = _dot(h.astype(BF16), win_ref[...])

    cos, sin = cos_ref[...], sin_ref[...]
    cos64, sin64 = cos64_ref[...], sin64_ref[...]
    lane = lax.broadcasted_iota(jnp.int32, (TM, LANES), 1)
    low32 = (lane & 32) == 0
    low16 = (lane & 16) == 0

    def rope128(t):
        partner = jnp.where(low32, pltpu.roll(t, LANES - 32, 1), pltpu.roll(t, 32, 1))
        return t * cos + partner * sin

    def rope64(t):
        partner = jnp.where(low16, pltpu.roll(t, LANES - 16, 1), pltpu.roll(t, 16, 1))
        return t * cos64 + partner * sin64

    off = 0
    for i in range(A_HEADS):
        qa_ref[:, _col(i)] = rope128(p[:, _col(off + i)]).astype(BF16)
    off += A_HEADS
    for i in range(A_KV):
        ka_ref[:, _col(i)] = rope128(p[:, _col(off + i)]).astype(BF16)
    off += A_KV
    for i in range(A_KV):
        va_ref[:, _col(i)] = p[:, _col(off + i)].astype(BF16)
    off += A_KV

    gqn, gkn = gqn_ref[...], gkn_ref[...]
    for i in range(B_HEADS):
        qb_ref[:, _col(i)] = rope128(_rmsnorm(p[:, _col(off + i)], gqn)).astype(BF16)
    off += B_HEADS
    for i in range(B_KV):
        kb_ref[:, _col(i)] = rope128(_rmsnorm(p[:, _col(off + i)], gkn)).astype(BF16)
    off += B_KV
    for i in range(B_KV):
        vtb_ref[_col(i), :] = p[:, _col(off + i)].T.astype(BF16)
    off += B_KV

    c0 = off * LANES
    cq = _rmsnorm(p[:, c0:c0 + C_Q_LORA], gcq_ref[...])
    q = _dot(cq.astype(BF16), wuq_ref[...])
    c0 += C_Q_LORA
    ckv = _rmsnorm(p[:, c0:c0 + C_KV_LORA], gckv_ref[...])
    kv = _dot(ckv.astype(BF16), wukv_ref[...])
    c0 += C_KV_LORA
    krope = rope64(p[:, c0:c0 + LANES]).astype(BF16)
    for i in range(C_HEADS):
        qc_ref[:, _col(2 * i)] = q[:, _col(2 * i)].astype(BF16)
        qc_ref[:, _col(2 * i + 1)] = rope64(q[:, _col(2 * i + 1)]).astype(BF16)
        kc_ref[:, _col(2 * i)] = kv[:, _col(i)].astype(BF16)
        kc_ref[:, _col(2 * i + 1)] = krope
        vtc_ref[_col(i), :] = kv[:, _col(C_HEADS + i)].T.astype(BF16)


def _qkv_call(l, xa, mods, gpre, win, tabs, gqn, gkn, gcq, gckv, wuq, wukv, *, n_lat_tiles, tiles_per_seq, n_seq):
    t, d = xa.shape
    n_tiles = t // TM
    d_in = win.shape[2]

    def seg(i):
        return jnp.minimum(i // tiles_per_seq, n_seq)

    def rope_blk(i):
        return jnp.where(i < n_lat_tiles, i % tiles_per_seq, tiles_per_seq)

    layer = lambda i: (l, 0, 0)
    vec = lambda n: pl.BlockSpec((None, 1, n), layer)
    tab_spec = pl.BlockSpec((TM, LANES), lambda i: (rope_blk(i), 0))
    row_major = lambda w: (pl.BlockSpec((TM, w), lambda i: (i, 0)), jax.ShapeDtypeStruct((t, w), BF16))
    col_major = lambda w: (pl.BlockSpec((w, TM), lambda i: (0, i)), jax.ShapeDtypeStruct((w, t), BF16))
    outs = [row_major(A_HEADS * HEAD_DIM), row_major(A_KV * HEAD_DIM), row_major(A_KV * HEAD_DIM),
            row_major(B_HEADS * HEAD_DIM), row_major(B_KV * HEAD_DIM), col_major(B_KV * HEAD_DIM),
            row_major(C_HEADS * C_QK_PAD), row_major(C_HEADS * C_QK_PAD), col_major(C_HEADS * C_V)]
    return pl.pallas_call(
        _qkv_kernel,
        grid=(n_tiles,),
        in_specs=[
            pl.BlockSpec((TM, d), lambda i: (i, 0)),
            pl.BlockSpec((None, None, MOD_ROWS, d), lambda i: (l, seg(i), 0, 0)),
            vec(d),
            pl.BlockSpec((None, d, d_in), layer, pipeline_mode=pl.Buffered(1)),
            tab_spec, tab_spec, tab_spec, tab_spec,
            vec(HEAD_DIM), vec(HEAD_DIM), vec(C_Q_LORA), vec(C_KV_LORA),
            pl.BlockSpec((None,) + wuq.shape[1:], layer, pipeline_mode=pl.Buffered(1)),
            pl.BlockSpec((None,) + wukv.shape[1:], layer, pipeline_mode=pl.Buffered(1)),
        ],
        out_specs=[o[0] for o in outs],
        out_shape=[o[1] for o in outs],
        compiler_params=pltpu.CompilerParams(
            dimension_semantics=("arbitrary",),
            vmem_limit_bytes=56 * MIB),
        name="qkv_prep",
    )(xa, mods, gpre, win, *tabs, gqn, gkn, gcq, gckv, wuq, wukv)


def _attn_window_kernel(sink_ref, q_ref, kl_ref, vl_ref, kc_ref, vc_ref, o_ref, *,
                        layer, heads_per_group, dk, dv, scale, n_q_lat, seq):
    g = pl.program_id(1)
    qi = pl.program_id(2)
    c2 = scale * LOG2E
    band = TQ + 2 * WINDOW

    def one_head(r, latent):
        q = q_ref[:, r * dk:(r + 1) * dk]
        s_c = _dot_nt(q, kc_ref[...])
        m2 = jnp.max(s_c, axis=-1, keepdims=True) * c2
        if latent:
            start = pl.multiple_of(jnp.clip(qi * TQ - WINDOW, 0, seq - band), LANES)
            k_l = kl_ref[pl.ds(start, band), :]
            v_l = vl_ref[pl.ds(start, band), :]
            s_l = _dot_nt(q, k_l)
            kpos = start + lax.broadcasted_iota(jnp.int32, (TQ, band), 1)
            qpos = qi * TQ + lax.broadcasted_iota(jnp.int32, (TQ, band), 0)
            s_l = jnp.where(jnp.abs(kpos - qpos) <= WINDOW, s_l, NEG)
            m2 = jnp.maximum(m2, jnp.max(s_l, axis=-1, keepdims=True) * c2)
        sink2 = sink_ref[layer, g * heads_per_group + r] * LOG2E
        m2 = jnp.maximum(m2, sink2)
        p_c = jnp.exp2(s_c * c2 - m2)
        denom = jnp.sum(p_c, axis=-1, keepdims=True) + jnp.exp2(sink2 - m2)
        o = _dot(p_c.astype(BF16), vc_ref[...])
        if latent:
            p_l = jnp.exp2(s_l * c2 - m2)
            denom = denom + jnp.sum(p_l, axis=-1, keepdims=True)
            o = o + _dot(p_l.astype(BF16), v_l)
        o_ref[:, r * dv:(r + 1) * dv] = (o / denom).astype(BF16)

    @pl.when(qi < n_q_lat)
    def _():
        for r in range(heads_per_group):
            one_head(r, True)

    @pl.when(qi >= n_q_lat)
    def _():
        for r in range(heads_per_group):
            one_head(r, False)


def _attn_window_call(l, q, k, v, sink, *, n_groups, heads_per_group, dk, dv, scale, batch, seq, ctx_len,
                      ctx_queries):
    t = q.shape[0]
    n_q_lat = seq // TQ
    n_q_ctx = ctx_len // TQ if ctx_queries else 0
    ctx_blk0 = batch * seq // ctx_len

    def q_map(b, g, qi):
        lat = b * n_q_lat + qi
        ctx = batch * n_q_lat + b * (ctx_len // TQ) + (qi - n_q_lat)
        return (jnp.where(qi < n_q_lat, lat, ctx), g)

    t_out = t if ctx_queries else batch * seq
    return pl.pallas_call(
        functools.partial(_attn_window_kernel, layer=l, heads_per_group=heads_per_group, dk=dk, dv=dv,
                          scale=scale, n_q_lat=n_q_lat, seq=seq),
        grid=(batch, n_groups, n_q_lat + n_q_ctx),
        in_specs=[
            pl.BlockSpec(memory_space=pltpu.SMEM),
            pl.BlockSpec((TQ, heads_per_group * dk), q_map),
            pl.BlockSpec((seq, dk), lambda b, g, qi: (b, g)),
            pl.BlockSpec((seq, dv), lambda b, g, qi: (b, g)),
            pl.BlockSpec((ctx_len, dk), lambda b, g, qi: (ctx_blk0 + b, g)),
            pl.BlockSpec((ctx_len, dv), lambda b, g, qi: (ctx_blk0 + b, g)),
        ],
        out_specs=pl.BlockSpec((TQ, heads_per_group * dv), q_map),
        out_shape=jax.ShapeDtypeStruct((t_out, n_groups * heads_per_group * dv), BF16),
        compiler_params=pltpu.CompilerParams(
            dimension_semantics=("arbitrary", "arbitrary", "arbitrary"),
            vmem_limit_bytes=48 * MIB),
        name="attn_window",
    )(sink, q, k, v, k, v)


def _attn_dense_kernel(q_ref, kl_ref, vtl_ref, kc_ref, vtc_ref, o_ref, k2_scr, *,
                       heads_per_group, dk, dv, scale, tq):
    c2 = scale * LOG2E
    seq = kl_ref.shape[0]

    @pl.when(pl.program_id(2) == 0)
    def _():
        kf = kl_ref[...].astype(F32)
        kcf = kc_ref[...].astype(F32)
        k2_scr[0] = jnp.maximum(jnp.max(jnp.sum(kf * kf, axis=-1, keepdims=True)),
                                jnp.max(jnp.sum(kcf * kcf, axis=-1, keepdims=True)))

    if heads_per_group > 1:
        q = jnp.concatenate([q_ref[:, r * dk:(r + 1) * dk] for r in range(heads_per_group)], axis=0)
    else:
        q = q_ref[...]
    qf = q.astype(F32)
    q2 = jnp.sum(qf * qf, axis=-1, keepdims=True)
    bound = jnp.sqrt(q2 * k2_scr[0]) * (c2 * BOUND_SLACK)
    bound_is_safe = jnp.max(bound) <= SAFE_SHIFT_LOG2

    def finish(acc, denom):
        o = acc / denom
        for r in range(heads_per_group):
            o_ref[:, r * dv:(r + 1) * dv] = o[:, r * tq:(r + 1) * tq].T.astype(BF16)

    @pl.when(bound_is_safe)
    def _():
        shift = bound.T
        p_c = jnp.exp2(_dot_nt(kc_ref[...], q) * c2 - shift)
        p_l = jnp.exp2(_dot_nt(kl_ref[...], q) * c2 - shift)
        acc = _dot(vtc_ref[...], p_c.astype(BF16)) + _dot(vtl_ref[...], p_l.astype(BF16))
        finish(acc, jnp.sum(p_c, axis=0, keepdims=True) + jnp.sum(p_l, axis=0, keepdims=True))

    @pl.when(jnp.logical_not(bound_is_safe))
    def _():
        chunks = [(lambda: kc_ref[...], lambda: vtc_ref[...])]
        for j in range(seq // TK):
            chunks.append((functools.partial(lambda j: kl_ref[j * TK:(j + 1) * TK, :], j),
                           functools.partial(lambda j: vtl_ref[:, j * TK:(j + 1) * TK], j)))
        n = len(chunks)
        s_next = _dot_nt(chunks[0][0](), q)
        m = denom = acc = None
        for j in range(n):
            s = s_next
            if j + 1 < n:
                s_next = _dot_nt(chunks[j + 1][0](), q)
            m_j = jnp.max(s, axis=0, keepdims=True) * c2
            m_new = m_j if j == 0 else jnp.maximum(m, m_j)
            p = jnp.exp2(s * c2 - m_new)
            pv = _dot(chunks[j][1](), p.astype(BF16))
            if j == 0:
                denom = jnp.sum(p, axis=0, keepdims=True)
                acc = pv
            else:
                alpha = jnp.exp2(m - m_new)
                denom = denom * alpha + jnp.sum(p, axis=0, keepdims=True)
                acc = acc * alpha + pv
            m = m_new
        finish(acc, denom)


def _attn_dense_call(q, k, vt, *, n_groups, heads_per_group, dk, dv, scale, batch, seq, ctx_len, tq, t_out):
    n_q = seq // tq
    ctx_blk0 = batch * seq // ctx_len
    return pl.pallas_call(
        functools.partial(_attn_dense_kernel, heads_per_group=heads_per_group, dk=dk, dv=dv, scale=scale, tq=tq),
        grid=(batch, n_groups, n_q),
        in_specs=[
            pl.BlockSpec((tq, heads_per_group * dk), lambda b, g, qi: (b * n_q + qi, g)),
            pl.BlockSpec((seq, dk), lambda b, g, qi: (b, g)),
            pl.BlockSpec((dv, seq), lambda b, g, qi: (g, b)),
            pl.BlockSpec((ctx_len, dk), lambda b, g, qi: (ctx_blk0 + b, g)),
            pl.BlockSpec((dv, ctx_len), lambda b, g, qi: (g, ctx_blk0 + b)),
        ],
        out_specs=pl.BlockSpec((tq, heads_per_group * dv), lambda b, g, qi: (b * n_q + qi, g)),
        out_shape=jax.ShapeDtypeStruct((t_out, n_groups * heads_per_group * dv), BF16),
        scratch_shapes=[pltpu.SMEM((1,), F32)],
        compiler_params=pltpu.CompilerParams(
            dimension_semantics=("arbitrary", "arbitrary", "arbitrary"),
            vmem_limit_bytes=48 * MIB),
        name=f"attn_dense_dk{dk}",
    )(q, k, vt, k, vt)


def _attn_ctx_kernel(qb_ref, kb_ref, vtb_ref, qc_ref, kc_ref, vtc_ref, ob_in, oc_in, ob_ref, oc_ref, *,
                     scale_b, scale_c):
    del ob_in, oc_in

    def head(q, k, vt, scale):
        c2 = scale * LOG2E
        s = _dot_nt(q, k)
        p = jnp.exp2((s - jnp.max(s, axis=-1, keepdims=True)) * c2)
        o = _dot_nt(p.astype(BF16), vt)
        return (o / jnp.sum(p, axis=-1, keepdims=True)).astype(BF16)

    rep = B_HEADS // B_KV
    for hd in range(B_HEADS):
        g = hd // rep
        ob_ref[:, _col(hd)] = head(qb_ref[:, _col(hd)], kb_ref[:, _col(g)], vtb_ref[_col(g), :], scale_b)
    for hd in range(C_HEADS):
        qk = slice(hd * C_QK_PAD, (hd + 1) * C_QK_PAD)
        oc_ref[:, _col(hd)] = head(qc_ref[:, qk], kc_ref[:, qk], vtc_ref[_col(hd), :], scale_c)


def _attn_ctx_call(qb, kb, vtb, qc, kc, vtc, ob, oc, *, scale_b, scale_c, batch, seq, ctx_len):
    blk0 = batch * seq // ctx_len
    rows = lambda w: pl.BlockSpec((ctx_len, w), lambda b: (blk0 + b, 0))
    cols = lambda w: pl.BlockSpec((w, ctx_len), lambda b: (0, blk0 + b))
    return pl.pallas_call(
        functools.partial(_attn_ctx_kernel, scale_b=scale_b, scale_c=scale_c),
        grid=(batch,),
        in_specs=[rows(qb.shape[1]), rows(kb.shape[1]), cols(vtb.shape[0]),
                  rows(qc.shape[1]), rows(kc.shape[1]), cols(vtc.shape[0]),
                  pl.BlockSpec(memory_space=pl.ANY), pl.BlockSpec(memory_space=pl.ANY)],
        out_specs=[rows(ob.shape[1]), rows(oc.shape[1])],
        out_shape=[jax.ShapeDtypeStruct(ob.shape, ob.dtype), jax.ShapeDtypeStruct(oc.shape, oc.dtype)],
        input_output_aliases={6: 0, 7: 1},
        compiler_params=pltpu.CompilerParams(
            dimension_semantics=("arbitrary",),
            vmem_limit_bytes=32 * MIB),
        name="attn_ctx",
    )(qb, kb, vtb, qc, kc, vtc, ob, oc)


def _outproj_kernel(oa_ref, ob_ref, oc_ref, x_ref, mod_ref, gpost_ref, wo_ref, xo_ref):
    wa, wb = oa_ref.shape[1], ob_ref.shape[1]
    o = _dot(oa_ref[...], wo_ref[0:wa, :])
    o = o + _dot(ob_ref[...], wo_ref[wa:wa + wb, :])
    o = o + _dot(oc_ref[...], wo_ref[wa + wb:, :])
    m = mod_ref[...]
    xo_ref[...] = x_ref[...] + m[2:3] * _rmsnorm(o, gpost_ref[...])


def _outproj_call(l, oa, ob, oc, xa, mods, gpost, wo, *, tiles_per_seq, n_seq):
    t = oa.shape[0]
    d = xa.shape[1]
    seg = lambda i: jnp.minimum(i // tiles_per_seq, n_seq)
    layer = lambda i: (l, 0, 0)
    return pl.pallas_call(
        _outproj_kernel,
        grid=(t // TM,),
        in_specs=[
            pl.BlockSpec((TM, oa.shape[1]), lambda i: (i, 0)),
            pl.BlockSpec((TM, ob.shape[1]), lambda i: (i, 0)),
            pl.BlockSpec((TM, oc.shape[1]), lambda i: (i, 0)),
            pl.BlockSpec((TM, d), lambda i: (i, 0)),
            pl.BlockSpec((None, None, MOD_ROWS, d), lambda i: (l, seg(i), 0, 0)),
            pl.BlockSpec((None, 1, d), layer),
            pl.BlockSpec((None,) + wo.shape[1:], layer, pipeline_mode=pl.Buffered(1)),
        ],
        out_specs=pl.BlockSpec((TM, d), lambda i: (i, 0)),
        out_shape=jax.ShapeDtypeStruct((t, d), F32),
        compiler_params=pltpu.CompilerParams(
            dimension_semantics=("arbitrary",),
            vmem_limit_bytes=48 * MIB),
        name="outproj",
    )(oa, ob, oc, xa, mods, gpost, wo)


def _ffn_kernel(x_ref, mod_ref, gpre_ref, gpost_ref, w1_ref, w3_ref, w2_ref, o_ref, h_scr, acc_scr):
    j = pl.program_id(1)

    @pl.when(j == 0)
    def _():
        m = mod_ref[...]
        h = _rmsnorm(x_ref[...], gpre_ref[...]) * (1.0 + m[4:5]) + m[3:4]
        h_scr[...] = h.astype(BF16)
        acc_scr[...] = jnp.zeros_like(acc_scr)

    h = h_scr[...]
    a = _dot(h, w1_ref[...])
    b = _dot(h, w3_ref[...])
    t = a * jax.nn.sigmoid(a) * b
    acc_scr[...] += _dot(t.astype(BF16), w2_ref[...])

    @pl.when(j == pl.num_programs(1) - 1)
    def _():
        m = mod_ref[...]
        o_ref[...] = x_ref[...] + m[5:6] * _rmsnorm(acc_scr[...], gpost_ref[...])


def _ffn_call(l, xa, mods, gpre, gpost, w1, w3, w2, *, tiles_per_seq, n_seq):
    t, d = xa.shape
    f = w1.shape[2]
    seg = lambda i: jnp.minimum(i // tiles_per_seq, n_seq)
    layer = lambda i, j: (l, 0, 0)
    return pl.pallas_call(
        _ffn_kernel,
        grid=(t // TM, f // TF),
        in_specs=[
            pl.BlockSpec((TM, d), lambda i, j: (i, 0)),
            pl.BlockSpec((None, None, MOD_ROWS, d), lambda i, j: (l, seg(i), 0, 0)),
            pl.BlockSpec((None, 1, d), layer),
            pl.BlockSpec((None, 1, d), layer),
            pl.BlockSpec((None, d, TF), lambda i, j: (l, 0, j)),
            pl.BlockSpec((None, d, TF), lambda i, j: (l, 0, j)),
            pl.BlockSpec((None, TF, d), lambda i, j: (l, j, 0)),
        ],
        out_specs=pl.BlockSpec((TM, d), lambda i, j: (i, 0)),
        out_shape=jax.ShapeDtypeStruct((t, d), F32),
        scratch_shapes=[pltpu.VMEM((TM, d), BF16), pltpu.VMEM((TM, d), F32)],
        compiler_params=pltpu.CompilerParams(
            dimension_semantics=("arbitrary", "arbitrary"),
            vmem_limit_bytes=48 * MIB),
        name="ffn_swiglu",
    )(xa, mods, gpre, gpost, w1, w3, w2)


def _rope_tables(seq):
    pos = jnp.arange(seq, dtype=jnp.int32)
    row = (pos // GRID_W).astype(F32)[:, None]
    colp = (pos % GRID_W).astype(F32)[:, None]

    def cs(n):
        freqs = jnp.power(jnp.float32(ROPE_THETA), -jnp.arange(n, dtype=F32) / n)[None, :]
        return (jnp.cos(row * freqs), jnp.sin(row * freqs), jnp.cos(colp * freqs), jnp.sin(colp * freqs))

    cr, sr, cc, sc = cs(HEAD_DIM // 4)
    cos128 = jnp.concatenate([cr, cr, cc, cc], axis=1)
    sin128 = jnp.concatenate([-sr, sr, -sc, sc], axis=1)
    cr, sr, cc, sc = cs(C_ROPE // 4)
    pad1 = jnp.ones((seq, LANES - C_ROPE), F32)
    pad0 = jnp.zeros((seq, LANES - C_ROPE), F32)
    cos64 = jnp.concatenate([cr, cr, cc, cc, pad1], axis=1)
    sin64 = jnp.concatenate([-sr, sr, -sc, sc, pad0], axis=1)
    ident_c = jnp.ones((TM, LANES), F32)
    ident_s = jnp.zeros((TM, LANES), F32)
    return tuple(jnp.concatenate([tab, ident], axis=0)
                 for tab, ident in ((cos128, ident_c), (sin128, ident_s), (cos64, ident_c), (sin64, ident_s)))


def kernel(x, c, ctx, c_ctx, w_mod, b_mod, g_mix_pre, g_mix_post, g_ffn_pre, g_ffn_post, w_in, sink_a,
           g_qn_b, g_kn_b, g_cq, w_uq, g_ckv, w_ukv, w_out, w_ffn1, w_ffn3, w_ffn2):
    batch, seq, d = x.shape
    ctx_len = ctx.shape[1]
    depth = w_mod.shape[0]
    n_lat, n_ctx = batch * seq, batch * ctx_len
    assert seq % TM == 0 and n_ctx % TM == 0 and seq % ctx_len == 0 and ctx_len % TQ == 0
    assert seq % TQ_GQA == 0 and seq % TQ_MLA == 0 and seq % TK == 0 and ctx_len % LANES == 0
    assert batch + 1 <= MOD_ROWS and seq % GRID_W == 0
    tiles_per_seq = seq // TM
    n_lat_tiles = n_lat // TM

    cv = jnp.zeros((MOD_ROWS, d), F32).at[:batch].set(c).at[batch].set(c_ctx)
    mod_all = _mod_call(cv, w_mod, b_mod)
    mods = mod_all[:, :batch + 1].reshape(depth, batch + 1, 6, d)
    mods = jnp.pad(mods, ((0, 0), (0, 0), (0, MOD_ROWS - 6), (0, 0)))

    tabs = _rope_tables(seq)

    d_in = w_in.shape[2]
    d_in_pad = -(-d_in // LANES) * LANES
    win = jnp.pad(w_in, ((0, 0), (0, 0), (0, d_in_pad - d_in))).astype(BF16)
    wuq = w_uq.reshape(depth, C_Q_LORA, C_HEADS, C_NOPE + C_ROPE)
    wuq = jnp.pad(wuq, ((0, 0), (0, 0), (0, 0), (0, C_QK_PAD - C_NOPE - C_ROPE)))
    wuq = wuq.reshape(depth, C_Q_LORA, C_HEADS * C_QK_PAD).astype(BF16)
    wukv = w_ukv.reshape(depth, C_KV_LORA, C_HEADS, 2, C_NOPE).transpose(0, 1, 3, 2, 4)
    wukv = wukv.reshape(depth, C_KV_LORA, 2 * C_HEADS * C_NOPE).astype(BF16)
    wo = w_out.astype(BF16)
    w1, w3, w2 = w_ffn1.astype(BF16), w_ffn3.astype(BF16), w_ffn2.astype(BF16)
    vecs = lambda v: v.reshape(depth, 1, -1)
    g_mix_pre, g_mix_post, g_ffn_pre, g_ffn_post = map(vecs, (g_mix_pre, g_mix_post, g_ffn_pre, g_ffn_post))
    g_qn_b, g_kn_b, g_cq, g_ckv = map(vecs, (g_qn_b, g_kn_b, g_cq, g_ckv))

    xa = jnp.concatenate([x.reshape(n_lat, d), ctx.reshape(n_ctx, d)], axis=0)
    sc_ab = 1.0 / math.sqrt(HEAD_DIM)
    sc_c = 1.0 / math.sqrt(C_NOPE + C_ROPE)
    dims = dict(batch=batch, seq=seq, ctx_len=ctx_len)

    for l in range(depth):
        last = l == depth - 1
        qa, ka, va, qb, kb, vtb, qc, kc, vtc = _qkv_call(
            l, xa, mods, g_mix_pre, win, tabs, g_qn_b, g_kn_b, g_cq, g_ckv, wuq, wukv,
            n_lat_tiles=n_lat_tiles, tiles_per_seq=tiles_per_seq, n_seq=batch)
        t_out = n_lat if last else n_lat + n_ctx
        oa = _attn_window_call(l, qa, ka, va, sink_a, n_groups=A_KV, heads_per_group=A_HEADS // A_KV,
                               dk=HEAD_DIM, dv=HEAD_DIM, scale=sc_ab, ctx_queries=not last, **dims)
        ob = _attn_dense_call(qb, kb, vtb, n_groups=B_KV, heads_per_group=B_HEADS // B_KV, dk=HEAD_DIM,
                              dv=HEAD_DIM, scale=sc_ab, tq=TQ_GQA, t_out=t_out, **dims)
        oc = _attn_dense_call(qc, kc, vtc, n_groups=C_HEADS, heads_per_group=1, dk=C_QK_PAD,
                              dv=C_V, scale=sc_c, tq=TQ_MLA, t_out=t_out, **dims)
        if not last:
            ob, oc = _attn_ctx_call(qb, kb, vtb, qc, kc, vtc, ob, oc, scale_b=sc_ab, scale_c=sc_c, **dims)
        xa = _outproj_call(l, oa, ob, oc, xa, mods, g_mix_post, wo, tiles_per_seq=tiles_per_seq, n_seq=batch)
        xa = _ffn_call(l, xa, mods, g_ffn_pre, g_ffn_post, w1, w3, w2, tiles_per_seq=tiles_per_seq, n_seq=batch)
    return xa.reshape(batch, seq, d)
```

```python
import functools
import math

import jax
import jax.numpy as jnp
from jax import lax
from jax.experimental import pallas as pl
from jax.experimental.pallas import tpu as pltpu

GRID_W = 64
HEAD_DIM = 128
A_HEADS, A_KV = 6, 2
B_HEADS, B_KV = 6, 2
WINDOW = 128
C_HEADS = 4
C_Q_LORA, C_KV_LORA = 512, 256
C_NOPE, C_ROPE, C_V = 128, 64, 128
C_QK_PAD = 256
ROPE_THETA = 10000.0
EPS = 1e-6
NEG = -1e30
LOG2E = math.log2(math.e)
SAFE_SHIFT_LOG2 = 50.0
BOUND_SLACK = 1.001

LANES = 128
MOD_ROWS = 8
TM = 512
TQ = 256
TQ_GQA = 256
TQ_MLA = 1024
TK = 512
TF = 512
N_UP_TILES = 8
TN_MOD = 1024
MIB = 1024 * 1024

F32 = jnp.float32
BF16 = jnp.bfloat16


def _rmsnorm(x, g):
    return x * lax.rsqrt(jnp.mean(x * x, axis=-1, keepdims=True) + EPS) * g


def _dot(a, b):
    return jnp.dot(a, b, preferred_element_type=F32)


def _dot_nt(a, b):
    return lax.dot_general(a, b, (((1,), (1,)), ((), ())), preferred_element_type=F32)


def _dot_tn(a, b):
    return lax.dot_general(a, b, (((0,), (0,)), ((), ())), preferred_element_type=F32)


def _col(i):
    return slice(i * LANES, (i + 1) * LANES)


def _mod_kernel(c_ref, w_ref, b_ref, o_ref):
    cv = c_ref[...]
    s = cv * jax.nn.sigmoid(cv)
    o_ref[...] = _dot(s.astype(BF16), w_ref[...].astype(BF16)) + b_ref[...]


def _mod_call(cv, w_mod, b_mod):
    depth, d, n = w_mod.shape
    return pl.pallas_call(
        _mod_kernel,
        grid=(depth, n // TN_MOD),
        in_specs=[
            pl.BlockSpec((MOD_ROWS, d), lambda l, j: (0, 0)),
            pl.BlockSpec((None, d, TN_MOD), lambda l, j: (l, 0, j)),
            pl.BlockSpec((None, 1, TN_MOD), lambda l, j: (l, 0, j)),
        ],
        out_specs=pl.BlockSpec((None, MOD_ROWS, TN_MOD), lambda l, j: (l, 0, j)),
        out_shape=jax.ShapeDtypeStruct((depth, MOD_ROWS, n), F32),
        compiler_params=pltpu.CompilerParams(
            dimension_semantics=("arbitrary", "arbitrary"),
            vmem_limit_bytes=40 * MIB),
        name="adaln_mod",
    )(cv, w_mod, b_mod.reshape(depth, 1, n))


def _qkv_kernel(x_ref, mod_ref, gpre_ref, win_ref, cos_ref, sin_ref, cos64_ref, sin64_ref,
                gqn_ref, gkn_ref, gcq_ref, gckv_ref, wuq_ref, wukv_ref,
                qa_ref, ka_ref, va_ref, qb_ref, kb_ref, vtb_ref, qc_ref, kc_ref, vtc_ref):
    m = mod_ref[...]
    h = _rmsnorm(x_ref[...], gpre_ref[...]) * (1.0 + m[1:2]) + m[0:1]
    p = _dot(h.astype(BF16), win_ref[...])

    cos, sin = cos_ref[...], sin_ref[...]
    cos64, sin64 = cos64_ref[...], sin64_ref[...]
    lane = lax.broadcasted_iota(jnp.int32, (TM, LANES), 1)
    low32 = (lane & 32) == 0
    low16 = (lane & 16) == 0

    def rope128(t):
        partner = jnp.where(low32, pltpu.roll(t, LANES - 32, 1), pltpu.roll(t, 32, 1))
        return t * cos + partner * sin

    def rope64(t):
        partner = jnp.where(low16, pltpu.roll(t, LANES - 16, 1), pltpu.roll(t, 16, 1))
        return t * cos64 + partner * sin64

    off = 0
    for i in range(A_HEADS):
        qa_ref[:, _col(i)] = rope128(p[:, _col(off + i)]).astype(BF16)
    off += A_HEADS
    for i in range(A_KV):
        ka_ref[:, _col(i)] = rope128(p[:, _col(off + i)]).astype(BF16)
    off += A_KV
    for i in range(A_KV):
        va_ref[:, _col(i)] = p[:, _col(off + i)].astype(BF16)
    off += A_KV

    gqn, gkn = gqn_ref[...], gkn_ref[...]
    for i in range(B_HEADS):
        qb_ref[:, _col(i)] = rope128(_rmsnorm(p[:, _col(off + i)], gqn)).astype(BF16)
    off += B_HEADS
    for i in range(B_KV):
        kb_ref[:, _col(i)] = rope128(_rmsnorm(p[:, _col(off + i)], gkn)).astype(BF16)
    off += B_KV
    for i in range(B_KV):
        vtb_ref[_col(i), :] = p[:, _col(off + i)].T.astype(BF16)
    off += B_KV

    c0 = off * LANES
    cq = _rmsnorm(p[:, c0:c0 + C_Q_LORA], gcq_ref[...])
    q = _dot(cq.astype(BF16), wuq_ref[...])
    c0 += C_Q_LORA
    ckv = _rmsnorm(p[:, c0:c0 + C_KV_LORA], gckv_ref[...])
    kv = _dot(ckv.astype(BF16), wukv_ref[...])
    c0 += C_KV_LORA
    ckr = jnp.concatenate([p[:, c0:c0 + C_ROPE], jnp.zeros((TM, LANES - C_ROPE), F32)], axis=1)
    krope = rope64(ckr).astype(BF16)
    for i in range(C_HEADS):
        qc_ref[:, _col(2 * i)] = q[:, _col(2 * i)].astype(BF16)
        qc_ref[:, _col(2 * i + 1)] = rope64(q[:, _col(2 * i + 1)]).astype(BF16)
        kc_ref[:, _col(2 * i)] = kv[:, _col(i)].astype(BF16)
        kc_ref[:, _col(2 * i + 1)] = krope
        vtc_ref[_col(i), :] = kv[:, _col(C_HEADS + i)].T.astype(BF16)


def _qkv_call(l, xa, mods, gpre, win, tabs, gqn, gkn, gcq, gckv, wuq, wukv, *, n_lat_tiles, tiles_per_seq, n_seq):
    t, d = xa.shape
    n_tiles = t // TM
    d_in = win.shape[2]

    def seg(i):
        return jnp.minimum(i // tiles_per_seq, n_seq)

    def rope_blk(i):
        return jnp.where(i < n_lat_tiles, i % tiles_per_seq, tiles_per_seq)

    layer = lambda i: (l, 0, 0)
    vec = lambda n: pl.BlockSpec((None, 1, n), layer)
    tab_spec = pl.BlockSpec((TM, LANES), lambda i: (rope_blk(i), 0))
    row_major = lambda w: (pl.BlockSpec((TM, w), lambda i: (i, 0)), jax.ShapeDtypeStruct((t, w), BF16))
    col_major = lambda w: (pl.BlockSpec((w, TM), lambda i: (0, i)), jax.ShapeDtypeStruct((w, t), BF16))
    outs = [row_major(A_HEADS * HEAD_DIM), row_major(A_KV * HEAD_DIM), row_major(A_KV * HEAD_DIM),
            row_major(B_HEADS * HEAD_DIM), row_major(B_KV * HEAD_DIM), col_major(B_KV * HEAD_DIM),
            row_major(C_HEADS * C_QK_PAD), row_major(C_HEADS * C_QK_PAD), col_major(C_HEADS * C_V)]
    return pl.pallas_call(
        _qkv_kernel,
        grid=(n_tiles,),
        in_specs=[
            pl.BlockSpec((TM, d), lambda i: (i, 0)),
            pl.BlockSpec((None, None, MOD_ROWS, d), lambda i: (l, seg(i), 0, 0)),
            vec(d),
            pl.BlockSpec((None, d, d_in), layer, pipeline_mode=pl.Buffered(1)),
            tab_spec, tab_spec, tab_spec, tab_spec,
            vec(HEAD_DIM), vec(HEAD_DIM), vec(C_Q_LORA), vec(C_KV_LORA),
            pl.BlockSpec((None,) + wuq.shape[1:], layer, pipeline_mode=pl.Buffered(1)),
            pl.BlockSpec((None,) + wukv.shape[1:], layer, pipeline_mode=pl.Buffered(1)),
        ],
        out_specs=[o[0] for o in outs],
        out_shape=[o[1] for o in outs],
        compiler_params=pltpu.CompilerParams(
            dimension_semantics=("arbitrary",),
            vmem_limit_bytes=56 * MIB),
        name="qkv_prep",
    )(xa, mods, gpre, win, *tabs, gqn, gkn, gcq, gckv, wuq, wukv)


def _attn_window_kernel(sink_ref, q_ref, kl_ref, vl_ref, kc_ref, vc_ref, o_ref, *,
                        layer, heads_per_group, dk, dv, scale, n_q_lat, seq):
    g = pl.program_id(1)
    qi = pl.program_id(2)
    c2 = scale * LOG2E
    band = TQ + 2 * WINDOW
    heads = range(heads_per_group)
    q = jnp.concatenate([q_ref[:, r * dk:(r + 1) * dk] for r in heads], axis=0)
    sink2 = jnp.concatenate(
        [jnp.full((1, TQ), sink_ref[layer, g * heads_per_group + r] * LOG2E, F32) for r in heads], axis=1)

    def finish(acc, denom):
        o = acc / denom
        for r in heads:
            o_ref[:, r * dv:(r + 1) * dv] = o[:, r * TQ:(r + 1) * TQ].T.astype(BF16)

    s_c = _dot_nt(kc_ref[...], q)
    m_c = jnp.maximum(jnp.max(s_c, axis=0, keepdims=True) * c2, sink2)

    @pl.when(qi < n_q_lat)
    def _():
        start = pl.multiple_of(jnp.clip(qi * TQ - WINDOW, 0, seq - band), LANES)
        s_l = _dot_nt(kl_ref[pl.ds(start, band), :], q)
        kpos = start + lax.broadcasted_iota(jnp.int32, (band, TQ), 0)
        qpos = qi * TQ + lax.broadcasted_iota(jnp.int32, (band, TQ), 1)
        in_window = jnp.abs(kpos - qpos) <= WINDOW
        s_l = jnp.where(jnp.concatenate([in_window] * heads_per_group, axis=1), s_l, NEG)
        m = jnp.maximum(m_c, jnp.max(s_l, axis=0, keepdims=True) * c2)
        p_c = jnp.exp2(s_c * c2 - m)
        p_l = jnp.exp2(s_l * c2 - m)
        denom = (jnp.sum(p_c, axis=0, keepdims=True) + jnp.sum(p_l, axis=0, keepdims=True)
                 + jnp.exp2(sink2 - m))
        acc = _dot_tn(vc_ref[...], p_c.astype(BF16)) + _dot_tn(vl_ref[pl.ds(start, band), :], p_l.astype(BF16))
        finish(acc, denom)

    @pl.when(qi >= n_q_lat)
    def _():
        p_c = jnp.exp2(s_c * c2 - m_c)
        denom = jnp.sum(p_c, axis=0, keepdims=True) + jnp.exp2(sink2 - m_c)
        finish(_dot_tn(vc_ref[...], p_c.astype(BF16)), denom)


def _attn_window_call(l, q, k, v, sink, *, n_groups, heads_per_group, dk, dv, scale, batch, seq, ctx_len,
                      ctx_queries):
    t = q.shape[0]
    n_q_lat = seq // TQ
    n_q_ctx = ctx_len // TQ if ctx_queries else 0
    ctx_blk0 = batch * seq // ctx_len

    def q_map(b, g, qi):
        lat = b * n_q_lat + qi
        ctx = batch * n_q_lat + b * (ctx_len // TQ) + (qi - n_q_lat)
        return (jnp.where(qi < n_q_lat, lat, ctx), g)

    t_out = t if ctx_queries else batch * seq
    return pl.pallas_call(
        functools.partial(_attn_window_kernel, layer=l, heads_per_group=heads_per_group, dk=dk, dv=dv,
                          scale=scale, n_q_lat=n_q_lat, seq=seq),
        grid=(batch, n_groups, n_q_lat + n_q_ctx),
        in_specs=[
            pl.BlockSpec(memory_space=pltpu.SMEM),
            pl.BlockSpec((TQ, heads_per_group * dk), q_map),
            pl.BlockSpec((seq, dk), lambda b, g, qi: (b, g)),
            pl.BlockSpec((seq, dv), lambda b, g, qi: (b, g)),
            pl.BlockSpec((ctx_len, dk), lambda b, g, qi: (ctx_blk0 + b, g)),
            pl.BlockSpec((ctx_len, dv), lambda b, g, qi: (ctx_blk0 + b, g)),
        ],
        out_specs=pl.BlockSpec((TQ, heads_per_group * dv), q_map),
        out_shape=jax.ShapeDtypeStruct((t_out, n_groups * heads_per_group * dv), BF16),
        compiler_params=pltpu.CompilerParams(
            dimension_semantics=("arbitrary", "arbitrary", "arbitrary"),
            vmem_limit_bytes=48 * MIB),
        name="attn_window",
    )(sink, q, k, v, k, v)


def _attn_dense_kernel(q_ref, kl_ref, vtl_ref, kc_ref, vtc_ref, o_ref, k2_scr, *,
                       heads_per_group, dk, dv, scale, tq):
    c2 = scale * LOG2E
    seq = kl_ref.shape[0]

    @pl.when(pl.program_id(2) == 0)
    def _():
        kf = kl_ref[...].astype(F32)
        kcf = kc_ref[...].astype(F32)
        k2_scr[0] = jnp.maximum(jnp.max(jnp.sum(kf * kf, axis=-1, keepdims=True)),
                                jnp.max(jnp.sum(kcf * kcf, axis=-1, keepdims=True)))

    if heads_per_group > 1:
        q = jnp.concatenate([q_ref[:, r * dk:(r + 1) * dk] for r in range(heads_per_group)], axis=0)
    else:
        q = q_ref[...]
    qf = q.astype(F32)
    q2 = jnp.sum(qf * qf, axis=-1, keepdims=True)
    bound = jnp.sqrt(q2 * k2_scr[0]) * (c2 * BOUND_SLACK)
    bound_is_safe = jnp.max(bound) <= SAFE_SHIFT_LOG2

    def finish(acc, denom):
        o = acc / denom
        for r in range(heads_per_group):
            o_ref[:, r * dv:(r + 1) * dv] = o[:, r * tq:(r + 1) * tq].T.astype(BF16)

    @pl.when(bound_is_safe)
    def _():
        shift = bound.T
        p_c = jnp.exp2(_dot_nt(kc_ref[...], q) * c2 - shift)
        p_l = jnp.exp2(_dot_nt(kl_ref[...], q) * c2 - shift)
        acc = _dot(vtc_ref[...], p_c.astype(BF16)) + _dot(vtl_ref[...], p_l.astype(BF16))
        finish(acc, jnp.sum(p_c, axis=0, keepdims=True) + jnp.sum(p_l, axis=0, keepdims=True))

    @pl.when(jnp.logical_not(bound_is_safe))
    def _():
        chunks = [(lambda: kc_ref[...], lambda: vtc_ref[...])]
        for j in range(seq // TK):
            chunks.append((functools.partial(lambda j: kl_ref[j * TK:(j + 1) * TK, :], j),
                           functools.partial(lambda j: vtl_ref[:, j * TK:(j + 1) * TK], j)))
        n = len(chunks)
        s_next = _dot_nt(chunks[0][0](), q)
        m = denom = acc = None
        for j in range(n):
            s = s_next
            if j + 1 < n:
                s_next = _dot_nt(chunks[j + 1][0](), q)
            m_j = jnp.max(s, axis=0, keepdims=True) * c2
            m_new = m_j if j == 0 else jnp.maximum(m, m_j)
            p = jnp.exp2(s * c2 - m_new)
            pv = _dot(chunks[j][1](), p.astype(BF16))
            if j == 0:
                denom = jnp.sum(p, axis=0, keepdims=True)
                acc = pv
            else:
                alpha = jnp.exp2(m - m_new)
                denom = denom * alpha + jnp.sum(p, axis=0, keepdims=True)
                acc = acc * alpha + pv
            m = m_new
        finish(acc, denom)


def _attn_dense_call(q, k, vt, *, n_groups, heads_per_group, dk, dv, scale, batch, seq, ctx_len, tq):
    n_q = seq // tq
    ctx_blk0 = batch * seq // ctx_len
    return pl.pallas_call(
        functools.partial(_attn_dense_kernel, heads_per_group=heads_per_group, dk=dk, dv=dv, scale=scale, tq=tq),
        grid=(batch, n_groups, n_q),
        in_specs=[
            pl.BlockSpec((tq, heads_per_group * dk), lambda b, g, qi: (b * n_q + qi, g)),
            pl.BlockSpec((seq, dk), lambda b, g, qi: (b, g)),
            pl.BlockSpec((dv, seq), lambda b, g, qi: (g, b)),
            pl.BlockSpec((ctx_len, dk), lambda b, g, qi: (ctx_blk0 + b, g)),
            pl.BlockSpec((dv, ctx_len), lambda b, g, qi: (g, ctx_blk0 + b)),
        ],
        out_specs=pl.BlockSpec((tq, heads_per_group * dv), lambda b, g, qi: (b * n_q + qi, g)),
        out_shape=jax.ShapeDtypeStruct((batch * seq, n_groups * heads_per_group * dv), BF16),
        scratch_shapes=[pltpu.SMEM((1,), F32)],
        compiler_params=pltpu.CompilerParams(
            dimension_semantics=("arbitrary", "arbitrary", "arbitrary"),
            vmem_limit_bytes=48 * MIB),
        name=f"attn_dense_dk{dk}",
    )(q, k, vt, k, vt)


def _attn_ctx_kernel(qb_ref, kb_ref, vtb_ref, qc_ref, kc_ref, vtc_ref, ob_ref, oc_ref, *, scale_b, scale_c):
    def head(q, k, vt, scale):
        c2 = scale * LOG2E
        s = _dot_nt(q, k)
        p = jnp.exp2((s - jnp.max(s, axis=-1, keepdims=True)) * c2)
        o = _dot_nt(p.astype(BF16), vt)
        return (o / jnp.sum(p, axis=-1, keepdims=True)).astype(BF16)

    rep = B_HEADS // B_KV
    for hd in range(B_HEADS):
        g = hd // rep
        ob_ref[:, _col(hd)] = head(qb_ref[:, _col(hd)], kb_ref[:, _col(g)], vtb_ref[_col(g), :], scale_b)
    for hd in range(C_HEADS):
        qk = slice(hd * C_QK_PAD, (hd + 1) * C_QK_PAD)
        oc_ref[:, _col(hd)] = head(qc_ref[:, qk], kc_ref[:, qk], vtc_ref[_col(hd), :], scale_c)


def _attn_ctx_call(qb, kb, vtb, qc, kc, vtc, *, scale_b, scale_c, batch, seq, ctx_len):
    blk0 = batch * seq // ctx_len
    rows = lambda w: pl.BlockSpec((ctx_len, w), lambda b: (blk0 + b, 0))
    cols = lambda w: pl.BlockSpec((w, ctx_len), lambda b: (0, blk0 + b))
    out = lambda w: (pl.BlockSpec((ctx_len, w), lambda b: (b, 0)),
                     jax.ShapeDtypeStruct((batch * ctx_len, w), BF16))
    outs = [out(B_HEADS * HEAD_DIM), out(C_HEADS * C_V)]
    return pl.pallas_call(
        functools.partial(_attn_ctx_kernel, scale_b=scale_b, scale_c=scale_c),
        grid=(batch,),
        in_specs=[rows(qb.shape[1]), rows(kb.shape[1]), cols(vtb.shape[0]),
                  rows(qc.shape[1]), rows(kc.shape[1]), cols(vtc.shape[0])],
        out_specs=[o[0] for o in outs],
        out_shape=[o[1] for o in outs],
        compiler_params=pltpu.CompilerParams(
            dimension_semantics=("arbitrary",),
            vmem_limit_bytes=32 * MIB),
        name="attn_ctx",
    )(qb, kb, vtb, qc, kc, vtc)


def _outproj_kernel(*refs, n_lat_tiles, has_ctx):
    if has_ctx:
        oa_ref, ob_ref, oc_ref, obc_ref, occ_ref, x_ref, mod_ref, gpost_ref, gffn_ref, wo_ref, xo_ref, h_ref = refs
    else:
        oa_ref, ob_ref, oc_ref, x_ref, mod_ref, gpost_ref, gffn_ref, wo_ref, xo_ref, h_ref = refs
    ob, oc = ob_ref[...], oc_ref[...]
    if has_ctx:
        is_ctx_tile = pl.program_id(0) >= n_lat_tiles
        ob = jnp.where(is_ctx_tile, obc_ref[...], ob)
        oc = jnp.where(is_ctx_tile, occ_ref[...], oc)
    wa, wb = oa_ref.shape[1], ob_ref.shape[1]
    o = _dot(oa_ref[...], wo_ref[0:wa, :])
    o = o + _dot(ob, wo_ref[wa:wa + wb, :])
    o = o + _dot(oc, wo_ref[wa + wb:, :])
    m = mod_ref[...]
    x_new = x_ref[...] + m[2:3] * _rmsnorm(o, gpost_ref[...])
    xo_ref[...] = x_new
    h_ref[...] = (_rmsnorm(x_new, gffn_ref[...]) * (1.0 + m[4:5]) + m[3:4]).astype(BF16)


def _outproj_call(l, oa, ob, oc, ctx_outs, xa, mods, gpost, gffn, wo, *, n_lat_tiles, tiles_per_seq, n_seq):
    t = oa.shape[0]
    d = xa.shape[1]
    seg = lambda i: jnp.minimum(i // tiles_per_seq, n_seq)
    layer = lambda i: (l, 0, 0)
    has_ctx = ctx_outs is not None
    lat_rows = lambda w: pl.BlockSpec((TM, w), lambda i: (jnp.minimum(i, n_lat_tiles - 1), 0))
    ctx_rows = lambda w: pl.BlockSpec((TM, w), lambda i: (jnp.maximum(i - n_lat_tiles, 0), 0))
    in_specs = [pl.BlockSpec((TM, oa.shape[1]), lambda i: (i, 0)), lat_rows(ob.shape[1]), lat_rows(oc.shape[1])]
    args = [oa, ob, oc]
    if has_ctx:
        in_specs += [ctx_rows(a.shape[1]) for a in ctx_outs]
        args += list(ctx_outs)
    in_specs += [
        pl.BlockSpec((TM, d), lambda i: (i, 0)),
        pl.BlockSpec((None, None, MOD_ROWS, d), lambda i: (l, seg(i), 0, 0)),
        pl.BlockSpec((None, 1, d), layer),
        pl.BlockSpec((None, 1, d), layer),
        pl.BlockSpec((None,) + wo.shape[1:], layer, pipeline_mode=pl.Buffered(1)),
    ]
    args += [xa, mods, gpost, gffn, wo]
    return pl.pallas_call(
        functools.partial(_outproj_kernel, n_lat_tiles=n_lat_tiles, has_ctx=has_ctx),
        grid=(t // TM,),
        in_specs=in_specs,
        out_specs=[pl.BlockSpec((TM, d), lambda i: (i, 0)), pl.BlockSpec((TM, d), lambda i: (i, 0))],
        out_shape=[jax.ShapeDtypeStruct((t, d), F32), jax.ShapeDtypeStruct((t, d), BF16)],
        compiler_params=pltpu.CompilerParams(
            dimension_semantics=("arbitrary",),
            vmem_limit_bytes=48 * MIB),
        name="outproj",
    )(*args)


def _ffn_up_kernel(h_ref, w1_ref, w3_ref, t_ref, w1_scr, w3_scr):
    @pl.when(pl.program_id(1) == 0)
    def _():
        w1_scr[...] = w1_ref[...].astype(BF16)
        w3_scr[...] = w3_ref[...].astype(BF16)

    h = h_ref[...]
    a = _dot(h, w1_scr[...])
    b = _dot(h, w3_scr[...])
    t_ref[...] = (a * jax.nn.sigmoid(a) * b).astype(BF16)


def _ffn_up_call(l, h, w1, w3):
    t, d = h.shape
    f = w1.shape[2]
    tm = t // N_UP_TILES
    return pl.pallas_call(
        _ffn_up_kernel,
        grid=(f // TF, N_UP_TILES),
        in_specs=[
            pl.BlockSpec((tm, d), lambda j, i: (i, 0)),
            pl.BlockSpec((None, d, TF), lambda j, i: (l, 0, j)),
            pl.BlockSpec((None, d, TF), lambda j, i: (l, 0, j)),
        ],
        out_specs=pl.BlockSpec((tm, TF), lambda j, i: (i, j)),
        out_shape=jax.ShapeDtypeStruct((t, f), BF16),
        scratch_shapes=[pltpu.VMEM((d, TF), BF16), pltpu.VMEM((d, TF), BF16)],
        compiler_params=pltpu.CompilerParams(
            dimension_semantics=("arbitrary", "arbitrary"),
            vmem_limit_bytes=48 * MIB),
        name="ffn_up",
    )(h, w1, w3)


def _ffn_down_kernel(t_ref, x_ref, mod_ref, gpost_ref, w2_ref, o_ref):
    y = _dot(t_ref[...], w2_ref[...])
    m = mod_ref[...]
    o_ref[...] = x_ref[...] + m[5:6] * _rmsnorm(y, gpost_ref[...])


def _ffn_down_call(l, tt, xa, mods, gpost, w2, *, tiles_per_seq, n_seq):
    t, d = xa.shape
    f = w2.shape[1]
    seg = lambda i: jnp.minimum(i // tiles_per_seq, n_seq)
    layer = lambda i: (l, 0, 0)
    return pl.pallas_call(
        _ffn_down_kernel,
        grid=(t // TM,),
        in_specs=[
            pl.BlockSpec((TM, f), lambda i: (i, 0)),
            pl.BlockSpec((TM, d), lambda i: (i, 0)),
            pl.BlockSpec((None, None, MOD_ROWS, d), lambda i: (l, seg(i), 0, 0)),
            pl.BlockSpec((None, 1, d), layer),
            pl.BlockSpec((None, f, d), layer, pipeline_mode=pl.Buffered(1)),
        ],
        out_specs=pl.BlockSpec((TM, d), lambda i: (i, 0)),
        out_shape=jax.ShapeDtypeStruct((t, d), F32),
        compiler_params=pltpu.CompilerParams(
            dimension_semantics=("arbitrary",),
            vmem_limit_bytes=60 * MIB),
        name="ffn_down",
    )(tt, xa, mods, gpost, w2)


def _rope_tables(seq):
    pos = jnp.arange(seq, dtype=jnp.int32)
    row = (pos // GRID_W).astype(F32)[:, None]
    colp = (pos % GRID_W).astype(F32)[:, None]

    def cs(n):
        freqs = jnp.power(jnp.float32(ROPE_THETA), -jnp.arange(n, dtype=F32) / n)[None, :]
        return (jnp.cos(row * freqs), jnp.sin(row * freqs), jnp.cos(colp * freqs), jnp.sin(colp * freqs))

    cr, sr, cc, sc = cs(HEAD_DIM // 4)
    cos128 = jnp.concatenate([cr, cr, cc, cc], axis=1)
    sin128 = jnp.concatenate([-sr, sr, -sc, sc], axis=1)
    cr, sr, cc, sc = cs(C_ROPE // 4)
    pad1 = jnp.ones((seq, LANES - C_ROPE), F32)
    pad0 = jnp.zeros((seq, LANES - C_ROPE), F32)
    cos64 = jnp.concatenate([cr, cr, cc, cc, pad1], axis=1)
    sin64 = jnp.concatenate([-sr, sr, -sc, sc, pad0], axis=1)
    ident_c = jnp.ones((TM, LANES), F32)
    ident_s = jnp.zeros((TM, LANES), F32)
    return tuple(jnp.concatenate([tab, ident], axis=0)
                 for tab, ident in ((cos128, ident_c), (sin128, ident_s), (cos64, ident_c), (sin64, ident_s)))


def kernel(x, c, ctx, c_ctx, w_mod, b_mod, g_mix_pre, g_mix_post, g_ffn_pre, g_ffn_post, w_in, sink_a,
           g_qn_b, g_kn_b, g_cq, w_uq, g_ckv, w_ukv, w_out, w_ffn1, w_ffn3, w_ffn2):
    batch, seq, d = x.shape
    ctx_len = ctx.shape[1]
    depth = w_mod.shape[0]
    n_lat, n_ctx = batch * seq, batch * ctx_len
    assert seq % TM == 0 and n_ctx % TM == 0 and seq % ctx_len == 0 and ctx_len % TQ == 0
    assert seq % TQ_GQA == 0 and seq % TQ_MLA == 0 and seq % TK == 0 and ctx_len % LANES == 0
    assert batch + 1 <= MOD_ROWS and seq % GRID_W == 0
    assert n_lat % (16 * N_UP_TILES) == 0 and (n_lat + n_ctx) % (16 * N_UP_TILES) == 0
    tiles_per_seq = seq // TM
    n_lat_tiles = n_lat // TM

    cv = jnp.zeros((MOD_ROWS, d), F32).at[:batch].set(c).at[batch].set(c_ctx)
    mod_all = _mod_call(cv, w_mod, b_mod)
    mods = mod_all[:, :batch + 1].reshape(depth, batch + 1, 6, d)
    mods = jnp.pad(mods, ((0, 0), (0, 0), (0, MOD_ROWS - 6), (0, 0)))

    tabs = _rope_tables(seq)

    win = w_in.astype(BF16)
    wuq = w_uq.reshape(depth, C_Q_LORA, C_HEADS, C_NOPE + C_ROPE)
    wuq = jnp.pad(wuq, ((0, 0), (0, 0), (0, 0), (0, C_QK_PAD - C_NOPE - C_ROPE)))
    wuq = wuq.reshape(depth, C_Q_LORA, C_HEADS * C_QK_PAD).astype(BF16)
    wukv = w_ukv.reshape(depth, C_KV_LORA, C_HEADS, 2, C_NOPE).transpose(0, 1, 3, 2, 4)
    wukv = wukv.reshape(depth, C_KV_LORA, 2 * C_HEADS * C_NOPE).astype(BF16)
    wo = w_out.astype(BF16)
    w2 = w_ffn2.astype(BF16)
    vecs = lambda v: v.reshape(depth, 1, -1)
    g_mix_pre, g_mix_post, g_ffn_pre, g_ffn_post = map(vecs, (g_mix_pre, g_mix_post, g_ffn_pre, g_ffn_post))
    g_qn_b, g_kn_b, g_cq, g_ckv = map(vecs, (g_qn_b, g_kn_b, g_cq, g_ckv))

    xa = jnp.concatenate([x.reshape(n_lat, d), ctx.reshape(n_ctx, d)], axis=0)
    sc_ab = 1.0 / math.sqrt(HEAD_DIM)
    sc_c = 1.0 / math.sqrt(C_NOPE + C_ROPE)
    dims = dict(batch=batch, seq=seq, ctx_len=ctx_len)

    for l in range(depth):
        last = l == depth - 1
        qa, ka, va, qb, kb, vtb, qc, kc, vtc = _qkv_call(
            l, xa, mods, g_mix_pre, win, tabs, g_qn_b, g_kn_b, g_cq, g_ckv, wuq, wukv,
            n_lat_tiles=n_lat_tiles, tiles_per_seq=tiles_per_seq, n_seq=batch)
        oa = _attn_window_call(l, qa, ka, va, sink_a, n_groups=A_KV, heads_per_group=A_HEADS // A_KV,
                               dk=HEAD_DIM, dv=HEAD_DIM, scale=sc_ab, ctx_queries=not last, **dims)
        ob = _attn_dense_call(qb, kb, vtb, n_groups=B_KV, heads_per_group=B_HEADS // B_KV, dk=HEAD_DIM,
                              dv=HEAD_DIM, scale=sc_ab, tq=TQ_GQA, **dims)
        oc = _attn_dense_call(qc, kc, vtc, n_groups=C_HEADS, heads_per_group=1, dk=C_QK_PAD,
                              dv=C_V, scale=sc_c, tq=TQ_MLA, **dims)
        ctx_outs = None if last else _attn_ctx_call(qb, kb, vtb, qc, kc, vtc, scale_b=sc_ab, scale_c=sc_c, **dims)
        xa, h = _outproj_call(l, oa, ob, oc, ctx_outs, xa, mods, g_mix_post, g_ffn_pre, wo,
                              n_lat_tiles=n_lat_tiles, tiles_per_seq=tiles_per_seq, n_seq=batch)
        tt = _ffn_up_call(l, h, w_ffn1, w_ffn3)
        xa = _ffn_down_call(l, tt, xa, mods, g_ffn_post, w2, tiles_per_seq=tiles_per_seq, n_seq=batch)
    return xa.reshape(batch, seq, d)
```

```python
import functools
import math

import jax
import jax.numpy as jnp
from jax import lax
from jax.experimental import pallas as pl
from jax.experimental.pallas import tpu as pltpu

GRID_W = 64
HEAD_DIM = 128
A_HEADS, A_KV = 6, 2
B_HEADS, B_KV = 6, 2
WINDOW = 128
C_HEADS = 4
C_Q_LORA, C_KV_LORA = 512, 256
C_NOPE, C_ROPE, C_V = 128, 64, 128
C_QK_PAD = 256
ROPE_THETA = 10000.0
EPS = 1e-6
NEG = -1e30
LOG2E = math.log2(math.e)
SAFE_SHIFT_LOG2 = 50.0
BOUND_SLACK = 1.001

LANES = 128
MOD_ROWS = 8
TM = 512
TQ = 256
SUBTILES_WIN = 2
TQ_GQA = 256
TQ_MLA = 1024
TK = 512
TF = 512
N_UP_TILES = 8
TN_MOD = 1024
MIB = 1024 * 1024

F32 = jnp.float32
BF16 = jnp.bfloat16


def _rmsnorm(x, g):
    return x * lax.rsqrt(jnp.mean(x * x, axis=-1, keepdims=True) + EPS) * g


def _dot(a, b):
    return jnp.dot(a, b, preferred_element_type=F32)


def _dot_nt(a, b):
    return lax.dot_general(a, b, (((1,), (1,)), ((), ())), preferred_element_type=F32)


def _dot_tn(a, b):
    return lax.dot_general(a, b, (((0,), (0,)), ((), ())), preferred_element_type=F32)


def _col(i):
    return slice(i * LANES, (i + 1) * LANES)


def _mod_kernel(c_ref, w_ref, b_ref, o_ref):
    cv = c_ref[...]
    s = cv * jax.nn.sigmoid(cv)
    o_ref[...] = _dot(s.astype(BF16), w_ref[...].astype(BF16)) + b_ref[...]


def _mod_call(cv, w_mod, b_mod):
    depth, d, n = w_mod.shape
    return pl.pallas_call(
        _mod_kernel,
        grid=(depth, n // TN_MOD),
        in_specs=[
            pl.BlockSpec((MOD_ROWS, d), lambda l, j: (0, 0)),
            pl.BlockSpec((None, d, TN_MOD), lambda l, j: (l, 0, j)),
            pl.BlockSpec((None, 1, TN_MOD), lambda l, j: (l, 0, j)),
        ],
        out_specs=pl.BlockSpec((None, MOD_ROWS, TN_MOD), lambda l, j: (l, 0, j)),
        out_shape=jax.ShapeDtypeStruct((depth, MOD_ROWS, n), F32),
        compiler_params=pltpu.CompilerParams(
            dimension_semantics=("arbitrary", "arbitrary"),
            vmem_limit_bytes=40 * MIB),
        name="adaln_mod",
    )(cv, w_mod, b_mod.reshape(depth, 1, n))


def _qkv_kernel(x_ref, mod_ref, gpre_ref, win_ref, cos_ref, sin_ref, cos64_ref, sin64_ref,
                gqn_ref, gkn_ref, gcq_ref, gckv_ref, wuq_ref, wukv_ref, wo_ref,
                qa_ref, ka_ref, va_ref, qb_ref, kb_ref, vtb_ref, qc_ref, kc_ref, vtc_ref, wob_ref):
    wob_ref[...] = wo_ref[...].astype(BF16)
    m = mod_ref[...]
    tr = TM // 2
    lane = lax.broadcasted_iota(jnp.int32, (tr, LANES), 1)
    low32 = (lane & 32) == 0
    low16 = (lane & 16) == 0

    for half in range(2):
        rows = slice(half * tr, (half + 1) * tr)
        h = _rmsnorm(x_ref[rows, :], gpre_ref[...]) * (1.0 + m[1:2]) + m[0:1]
        p = _dot(h.astype(BF16), win_ref[...])
        cos, sin = cos_ref[rows, :], sin_ref[rows, :]
        cos64, sin64 = cos64_ref[rows, :], sin64_ref[rows, :]

        def rope128(t):
            partner = jnp.where(low32, pltpu.roll(t, LANES - 32, 1), pltpu.roll(t, 32, 1))
            return t * cos + partner * sin

        def rope64(t):
            partner = jnp.where(low16, pltpu.roll(t, LANES - 16, 1), pltpu.roll(t, 16, 1))
            return t * cos64 + partner * sin64

        off = 0
        for i in range(A_HEADS):
            qa_ref[rows, _col(i)] = rope128(p[:, _col(off + i)]).astype(BF16)
        off += A_HEADS
        for i in range(A_KV):
            ka_ref[rows, _col(i)] = rope128(p[:, _col(off + i)]).astype(BF16)
        off += A_KV
        for i in range(A_KV):
            va_ref[rows, _col(i)] = p[:, _col(off + i)].astype(BF16)
        off += A_KV

        gqn, gkn = gqn_ref[...], gkn_ref[...]
        for i in range(B_HEADS):
            qb_ref[rows, _col(i)] = rope128(_rmsnorm(p[:, _col(off + i)], gqn)).astype(BF16)
        off += B_HEADS
        for i in range(B_KV):
            kb_ref[rows, _col(i)] = rope128(_rmsnorm(p[:, _col(off + i)], gkn)).astype(BF16)
        off += B_KV
        for i in range(B_KV):
            vtb_ref[_col(i), rows] = p[:, _col(off + i)].T.astype(BF16)
        off += B_KV

        c0 = off * LANES
        cq = _rmsnorm(p[:, c0:c0 + C_Q_LORA], gcq_ref[...])
        q = _dot(cq.astype(BF16), wuq_ref[...])
        c0 += C_Q_LORA
        ckv = _rmsnorm(p[:, c0:c0 + C_KV_LORA], gckv_ref[...])
        kv = _dot(ckv.astype(BF16), wukv_ref[...])
        c0 += C_KV_LORA
        ckr = jnp.concatenate([p[:, c0:c0 + C_ROPE], jnp.zeros((tr, LANES - C_ROPE), F32)], axis=1)
        krope = rope64(ckr).astype(BF16)
        for i in range(C_HEADS):
            qc_ref[rows, _col(2 * i)] = q[:, _col(2 * i)].astype(BF16)
            qc_ref[rows, _col(2 * i + 1)] = rope64(q[:, _col(2 * i + 1)]).astype(BF16)
            kc_ref[rows, _col(2 * i)] = kv[:, _col(i)].astype(BF16)
            kc_ref[rows, _col(2 * i + 1)] = krope
            vtc_ref[_col(i), rows] = kv[:, _col(C_HEADS + i)].T.astype(BF16)


def _qkv_call(l, xa, mods, gpre, win, tabs, gqn, gkn, gcq, gckv, wuq, wukv, w_out, *,
              n_lat_tiles, tiles_per_seq, n_seq):
    t, d = xa.shape
    n_tiles = t // TM
    d_in = win.shape[2]
    d_mix = w_out.shape[1]
    n_wo_blocks = d_mix // LANES
    assert n_tiles >= n_wo_blocks
    wo_blk = lambda i: jnp.minimum(i, n_wo_blocks - 1)

    def seg(i):
        return jnp.minimum(i // tiles_per_seq, n_seq)

    def rope_blk(i):
        return jnp.where(i < n_lat_tiles, i % tiles_per_seq, tiles_per_seq)

    layer = lambda i: (l, 0, 0)
    vec = lambda n: pl.BlockSpec((None, 1, n), layer)
    tab_spec = pl.BlockSpec((TM, LANES), lambda i: (rope_blk(i), 0))
    row_major = lambda w: (pl.BlockSpec((TM, w), lambda i: (i, 0)), jax.ShapeDtypeStruct((t, w), BF16))
    col_major = lambda w: (pl.BlockSpec((w, TM), lambda i: (0, i)), jax.ShapeDtypeStruct((w, t), BF16))
    outs = [row_major(A_HEADS * HEAD_DIM), row_major(A_KV * HEAD_DIM), row_major(A_KV * HEAD_DIM),
            row_major(B_HEADS * HEAD_DIM), row_major(B_KV * HEAD_DIM), col_major(B_KV * HEAD_DIM),
            row_major(C_HEADS * C_QK_PAD), row_major(C_HEADS * C_QK_PAD), col_major(C_HEADS * C_V),
            (pl.BlockSpec((LANES, d), lambda i: (wo_blk(i), 0)), jax.ShapeDtypeStruct((d_mix, d), BF16))]
    return pl.pallas_call(
        _qkv_kernel,
        grid=(n_tiles,),
        in_specs=[
            pl.BlockSpec((TM, d), lambda i: (i, 0)),
            pl.BlockSpec((None, None, MOD_ROWS, d), lambda i: (l, seg(i), 0, 0)),
            vec(d),
            pl.BlockSpec((None, d, d_in), layer, pipeline_mode=pl.Buffered(1)),
            tab_spec, tab_spec, tab_spec, tab_spec,
            vec(HEAD_DIM), vec(HEAD_DIM), vec(C_Q_LORA), vec(C_KV_LORA),
            pl.BlockSpec((None,) + wuq.shape[1:], layer, pipeline_mode=pl.Buffered(1)),
            pl.BlockSpec((None,) + wukv.shape[1:], layer, pipeline_mode=pl.Buffered(1)),
            pl.BlockSpec((None, LANES, d), lambda i: (l, wo_blk(i), 0)),
        ],
        out_specs=[o[0] for o in outs],
        out_shape=[o[1] for o in outs],
        compiler_params=pltpu.CompilerParams(
            dimension_semantics=("arbitrary",),
            vmem_limit_bytes=56 * MIB),
        name="qkv_prep",
    )(xa, mods, gpre, win, *tabs, gqn, gkn, gcq, gckv, wuq, wukv, w_out)


def _attn_window_kernel(sink_ref, q_ref, kl_ref, vl_ref, kc_ref, vc_ref, o_ref, *,
                        layer, heads_per_group, dk, dv, scale, seq):
    g = pl.program_id(1)
    qi = pl.program_id(2)
    c2 = scale * LOG2E
    band = TQ + 2 * WINDOW
    heads = range(heads_per_group)
    sink2 = jnp.concatenate(
        [jnp.full((1, TQ), sink_ref[layer, g * heads_per_group + r] * LOG2E, F32) for r in heads], axis=1)
    for sub in range(SUBTILES_WIN):
        rows = slice(sub * TQ, (sub + 1) * TQ)
        q0 = (qi * SUBTILES_WIN + sub) * TQ
        q = jnp.concatenate([q_ref[rows, r * dk:(r + 1) * dk] for r in heads], axis=0)
        start = pl.multiple_of(jnp.clip(q0 - WINDOW, 0, seq - band), LANES)
        s_c = _dot_nt(kc_ref[...], q)
        s_l = _dot_nt(kl_ref[pl.ds(start, band), :], q)
        kpos = start + lax.broadcasted_iota(jnp.int32, (band, TQ), 0)
        qpos = q0 + lax.broadcasted_iota(jnp.int32, (band, TQ), 1)
        in_window = jnp.abs(kpos - qpos) <= WINDOW
        s_l = jnp.where(jnp.concatenate([in_window] * heads_per_group, axis=1), s_l, NEG)
        m = jnp.maximum(jnp.maximum(jnp.max(s_c, axis=0, keepdims=True), jnp.max(s_l, axis=0, keepdims=True)) * c2,
                        sink2)
        p_c = jnp.exp2(s_c * c2 - m)
        p_l = jnp.exp2(s_l * c2 - m)
        denom = (jnp.sum(p_c, axis=0, keepdims=True) + jnp.sum(p_l, axis=0, keepdims=True)
                 + jnp.exp2(sink2 - m))
        acc = _dot_tn(vc_ref[...], p_c.astype(BF16)) + _dot_tn(vl_ref[pl.ds(start, band), :], p_l.astype(BF16))
        o = acc / denom
        for r in heads:
            o_ref[rows, r * dv:(r + 1) * dv] = o[:, r * TQ:(r + 1) * TQ].T.astype(BF16)


def _attn_window_call(l, q, k, v, sink, *, n_groups, heads_per_group, dk, dv, scale, batch, seq, ctx_len):
    tq = SUBTILES_WIN * TQ
    n_q = seq // tq
    ctx_blk0 = batch * seq // ctx_len
    return pl.pallas_call(
        functools.partial(_attn_window_kernel, layer=l, heads_per_group=heads_per_group, dk=dk, dv=dv,
                          scale=scale, seq=seq),
        grid=(batch, n_groups, n_q),
        in_specs=[
            pl.BlockSpec(memory_space=pltpu.SMEM),
            pl.BlockSpec((tq, heads_per_group * dk), lambda b, g, qi: (b * n_q + qi, g)),
            pl.BlockSpec((seq, dk), lambda b, g, qi: (b, g)),
            pl.BlockSpec((seq, dv), lambda b, g, qi: (b, g)),
            pl.BlockSpec((ctx_len, dk), lambda b, g, qi: (ctx_blk0 + b, g)),
            pl.BlockSpec((ctx_len, dv), lambda b, g, qi: (ctx_blk0 + b, g)),
        ],
        out_specs=pl.BlockSpec((tq, heads_per_group * dv), lambda b, g, qi: (b * n_q + qi, g)),
        out_shape=jax.ShapeDtypeStruct((batch * seq, n_groups * heads_per_group * dv), BF16),
        compiler_params=pltpu.CompilerParams(
            dimension_semantics=("arbitrary", "arbitrary", "arbitrary"),
            vmem_limit_bytes=48 * MIB),
        name="attn_window",
    )(sink, q, k, v, k, v)


def _attn_dense_kernel(q_ref, kl_ref, vtl_ref, kc_ref, vtc_ref, o_ref, k2_scr, *,
                       heads_per_group, dk, dv, scale, tq):
    c2 = scale * LOG2E
    seq = kl_ref.shape[0]

    @pl.when(pl.program_id(2) == 0)
    def _():
        kf = kl_ref[...].astype(F32)
        kcf = kc_ref[...].astype(F32)
        k2_scr[0] = jnp.maximum(jnp.max(jnp.sum(kf * kf, axis=-1, keepdims=True)),
                                jnp.max(jnp.sum(kcf * kcf, axis=-1, keepdims=True)))

    if heads_per_group > 1:
        q = jnp.concatenate([q_ref[:, r * dk:(r + 1) * dk] for r in range(heads_per_group)], axis=0)
    else:
        q = q_ref[...]
    qf = q.astype(F32)
    q2 = jnp.sum(qf * qf, axis=-1, keepdims=True)
    bound = jnp.sqrt(q2 * k2_scr[0]) * (c2 * BOUND_SLACK)
    bound_is_safe = jnp.max(bound) <= SAFE_SHIFT_LOG2

    def finish(acc, denom):
        o = acc / denom
        for r in range(heads_per_group):
            o_ref[:, r * dv:(r + 1) * dv] = o[:, r * tq:(r + 1) * tq].T.astype(BF16)

    @pl.when(bound_is_safe)
    def _():
        shift = bound.T
        p_c = jnp.exp2(_dot_nt(kc_ref[...], q) * c2 - shift)
        p_l = jnp.exp2(_dot_nt(kl_ref[...], q) * c2 - shift)
        acc = _dot(vtc_ref[...], p_c.astype(BF16)) + _dot(vtl_ref[...], p_l.astype(BF16))
        finish(acc, jnp.sum(p_c, axis=0, keepdims=True) + jnp.sum(p_l, axis=0, keepdims=True))

    @pl.when(jnp.logical_not(bound_is_safe))
    def _():
        chunks = [(lambda: kc_ref[...], lambda: vtc_ref[...])]
        for j in range(seq // TK):
            chunks.append((functools.partial(lambda j: kl_ref[j * TK:(j + 1) * TK, :], j),
                           functools.partial(lambda j: vtl_ref[:, j * TK:(j + 1) * TK], j)))
        n = len(chunks)
        s_next = _dot_nt(chunks[0][0](), q)
        m = denom = acc = None
        for j in range(n):
            s = s_next
            if j + 1 < n:
                s_next = _dot_nt(chunks[j + 1][0](), q)
            m_j = jnp.max(s, axis=0, keepdims=True) * c2
            m_new = m_j if j == 0 else jnp.maximum(m, m_j)
            p = jnp.exp2(s * c2 - m_new)
            pv = _dot(chunks[j][1](), p.astype(BF16))
            if j == 0:
                denom = jnp.sum(p, axis=0, keepdims=True)
                acc = pv
            else:
                alpha = jnp.exp2(m - m_new)
                denom = denom * alpha + jnp.sum(p, axis=0, keepdims=True)
                acc = acc * alpha + pv
            m = m_new
        finish(acc, denom)


def _attn_dense_call(q, k, vt, *, n_groups, heads_per_group, dk, dv, scale, batch, seq, ctx_len, tq):
    n_q = seq // tq
    ctx_blk0 = batch * seq // ctx_len
    return pl.pallas_call(
        functools.partial(_attn_dense_kernel, heads_per_group=heads_per_group, dk=dk, dv=dv, scale=scale, tq=tq),
        grid=(batch, n_groups, n_q),
        in_specs=[
            pl.BlockSpec((tq, heads_per_group * dk), lambda b, g, qi: (b * n_q + qi, g)),
            pl.BlockSpec((seq, dk), lambda b, g, qi: (b, g)),
            pl.BlockSpec((dv, seq), lambda b, g, qi: (g, b)),
            pl.BlockSpec((ctx_len, dk), lambda b, g, qi: (ctx_blk0 + b, g)),
            pl.BlockSpec((dv, ctx_len), lambda b, g, qi: (g, ctx_blk0 + b)),
        ],
        out_specs=pl.BlockSpec((tq, heads_per_group * dv), lambda b, g, qi: (b * n_q + qi, g)),
        out_shape=jax.ShapeDtypeStruct((batch * seq, n_groups * heads_per_group * dv), BF16),
        scratch_shapes=[pltpu.SMEM((1,), F32)],
        compiler_params=pltpu.CompilerParams(
            dimension_semantics=("arbitrary", "arbitrary", "arbitrary"),
            vmem_limit_bytes=48 * MIB),
        name=f"attn_dense_dk{dk}",
    )(q, k, vt, k, vt)


def _attn_ctx_kernel(sink_ref, qa_ref, ka_ref, va_ref, qb_ref, kb_ref, vtb_ref, qc_ref, kc_ref, vtc_ref,
                     oa_ref, ob_ref, oc_ref, *, layer, scale_ab, scale_c):
    def head(q, k, scale, pv, sink2=None):
        c2 = scale * LOG2E
        s = _dot_nt(q, k)
        m = jnp.max(s, axis=-1, keepdims=True) * c2
        if sink2 is not None:
            m = jnp.maximum(m, sink2)
        p = jnp.exp2(s * c2 - m)
        denom = jnp.sum(p, axis=-1, keepdims=True)
        if sink2 is not None:
            denom = denom + jnp.exp2(sink2 - m)
        return (pv(p.astype(BF16)) / denom).astype(BF16)

    for hd in range(A_HEADS):
        g = hd // (A_HEADS // A_KV)
        oa_ref[:, _col(hd)] = head(qa_ref[:, _col(hd)], ka_ref[:, _col(g)], scale_ab,
                                   lambda p: _dot(p, va_ref[:, _col(g)]), sink_ref[layer, hd] * LOG2E)
    for hd in range(B_HEADS):
        g = hd // (B_HEADS // B_KV)
        ob_ref[:, _col(hd)] = head(qb_ref[:, _col(hd)], kb_ref[:, _col(g)], scale_ab,
                                   lambda p: _dot_nt(p, vtb_ref[_col(g), :]))
    for hd in range(C_HEADS):
        qk = slice(hd * C_QK_PAD, (hd + 1) * C_QK_PAD)
        oc_ref[:, _col(hd)] = head(qc_ref[:, qk], kc_ref[:, qk], scale_c,
                                   lambda p: _dot_nt(p, vtc_ref[_col(hd), :]))


def _attn_ctx_call(l, sink, qa, ka, va, qb, kb, vtb, qc, kc, vtc, *, scale_ab, scale_c, batch, seq, ctx_len):
    blk0 = batch * seq // ctx_len
    rows = lambda a: pl.BlockSpec((ctx_len, a.shape[1]), lambda b: (blk0 + b, 0))
    cols = lambda a: pl.BlockSpec((a.shape[0], ctx_len), lambda b: (0, blk0 + b))
    out = lambda w: (pl.BlockSpec((ctx_len, w), lambda b: (b, 0)),
                     jax.ShapeDtypeStruct((batch * ctx_len, w), BF16))
    outs = [out(A_HEADS * HEAD_DIM), out(B_HEADS * HEAD_DIM), out(C_HEADS * C_V)]
    return pl.pallas_call(
        functools.partial(_attn_ctx_kernel, layer=l, scale_ab=scale_ab, scale_c=scale_c),
        grid=(batch,),
        in_specs=[pl.BlockSpec(memory_space=pltpu.SMEM), rows(qa), rows(ka), rows(va),
                  rows(qb), rows(kb), cols(vtb), rows(qc), rows(kc), cols(vtc)],
        out_specs=[o[0] for o in outs],
        out_shape=[o[1] for o in outs],
        compiler_params=pltpu.CompilerParams(
            dimension_semantics=("arbitrary",),
            vmem_limit_bytes=32 * MIB),
        name="attn_ctx",
    )(sink, qa, ka, va, qb, kb, vtb, qc, kc, vtc)


def _outproj_kernel(*refs, n_lat_tiles, has_ctx):
    lat_refs, refs = refs[:3], refs[3:]
    if has_ctx:
        ctx_refs, refs = refs[:3], refs[3:]
    x_ref, mod_ref, gpost_ref, gffn_ref, wo_ref, xo_ref, h_ref = refs
    m = mod_ref[...]
    for half in range(2):
        rows = slice(half * (TM // 2), (half + 1) * (TM // 2))
        o, w0 = None, 0
        for g, lat_ref in enumerate(lat_refs):
            og = lat_ref[rows, :]
            if has_ctx:
                og = jnp.where(pl.program_id(0) >= n_lat_tiles, ctx_refs[g][rows, :], og)
            width = lat_ref.shape[1]
            part = _dot(og, wo_ref[w0:w0 + width, :])
            o = part if o is None else o + part
            w0 += width
        x_new = x_ref[rows, :] + m[2:3] * _rmsnorm(o, gpost_ref[...])
        xo_ref[rows, :] = x_new
        h_ref[rows, :] = (_rmsnorm(x_new, gffn_ref[...]) * (1.0 + m[4:5]) + m[3:4]).astype(BF16)


def _outproj_call(l, lat_outs, ctx_outs, xa, mods, gpost, gffn, wo, *, n_tiles, n_lat_tiles, tiles_per_seq, n_seq):
    t = n_tiles * TM
    d = xa.shape[1]
    seg = lambda i: jnp.minimum(i // tiles_per_seq, n_seq)
    layer = lambda i: (l, 0, 0)
    has_ctx = ctx_outs is not None
    lat_rows = lambda a: pl.BlockSpec((TM, a.shape[1]), lambda i: (jnp.minimum(i, n_lat_tiles - 1), 0))
    ctx_rows = lambda a: pl.BlockSpec((TM, a.shape[1]), lambda i: (jnp.maximum(i - n_lat_tiles, 0), 0))
    in_specs = [lat_rows(a) for a in lat_outs]
    args = list(lat_outs)
    if has_ctx:
        in_specs += [ctx_rows(a) for a in ctx_outs]
        args += list(ctx_outs)
    in_specs += [
        pl.BlockSpec((TM, d), lambda i: (i, 0)),
        pl.BlockSpec((None, None, MOD_ROWS, d), lambda i: (l, seg(i), 0, 0)),
        pl.BlockSpec((None, 1, d), layer),
        pl.BlockSpec((None, 1, d), layer),
        pl.BlockSpec(wo.shape, lambda i: (0, 0), pipeline_mode=pl.Buffered(1)),
    ]
    args += [xa, mods, gpost, gffn, wo]
    return pl.pallas_call(
        functools.partial(_outproj_kernel, n_lat_tiles=n_lat_tiles, has_ctx=has_ctx),
        grid=(n_tiles,),
        in_specs=in_specs,
        out_specs=[pl.BlockSpec((TM, d), lambda i: (i, 0)), pl.BlockSpec((TM, d), lambda i: (i, 0))],
        out_shape=[jax.ShapeDtypeStruct((t, d), F32), jax.ShapeDtypeStruct((t, d), BF16)],
        compiler_params=pltpu.CompilerParams(
            dimension_semantics=("arbitrary",),
            vmem_limit_bytes=48 * MIB),
        name="outproj",
    )(*args)


def _ffn_up_kernel(h_ref, w1_ref, w3_ref, w2_ref, t_ref, w2b_ref, w1_scr, w3_scr):
    @pl.when(pl.program_id(1) == 0)
    def _():
        w1_scr[...] = w1_ref[...].astype(BF16)
        w3_scr[...] = w3_ref[...].astype(BF16)
        w2b_ref[...] = w2_ref[...].astype(BF16)

    h = h_ref[...]
    a = _dot(h, w1_scr[...])
    b = _dot(h, w3_scr[...])
    t_ref[...] = (a * jax.nn.sigmoid(a) * b).astype(BF16)


def _ffn_up_call(l, h, w1, w3, w2):
    t, d = h.shape
    f = w1.shape[2]
    tm = t // N_UP_TILES
    return pl.pallas_call(
        _ffn_up_kernel,
        grid=(f // TF, N_UP_TILES),
        in_specs=[
            pl.BlockSpec((tm, d), lambda j, i: (i, 0)),
            pl.BlockSpec((None, d, TF), lambda j, i: (l, 0, j)),
            pl.BlockSpec((None, d, TF), lambda j, i: (l, 0, j)),
            pl.BlockSpec((None, TF, d), lambda j, i: (l, j, 0)),
        ],
        out_specs=[pl.BlockSpec((tm, TF), lambda j, i: (i, j)), pl.BlockSpec((TF, d), lambda j, i: (j, 0))],
        out_shape=[jax.ShapeDtypeStruct((t, f), BF16), jax.ShapeDtypeStruct((f, d), BF16)],
        scratch_shapes=[pltpu.VMEM((d, TF), BF16), pltpu.VMEM((d, TF), BF16)],
        compiler_params=pltpu.CompilerParams(
            dimension_semantics=("arbitrary", "arbitrary"),
            vmem_limit_bytes=56 * MIB),
        name="ffn_up",
    )(h, w1, w3, w2)


def _ffn_down_kernel(t_ref, x_ref, mod_ref, gpost_ref, w2_ref, o_ref):
    y = _dot(t_ref[...], w2_ref[...])
    m = mod_ref[...]
    o_ref[...] = x_ref[...] + m[5:6] * _rmsnorm(y, gpost_ref[...])


def _ffn_down_call(l, tt, xa, mods, gpost, w2, *, tiles_per_seq, n_seq):
    t, d = xa.shape
    f = w2.shape[0]
    seg = lambda i: jnp.minimum(i // tiles_per_seq, n_seq)
    layer = lambda i: (l, 0, 0)
    return pl.pallas_call(
        _ffn_down_kernel,
        grid=(t // TM,),
        in_specs=[
            pl.BlockSpec((TM, f), lambda i: (i, 0)),
            pl.BlockSpec((TM, d), lambda i: (i, 0)),
            pl.BlockSpec((None, None, MOD_ROWS, d), lambda i: (l, seg(i), 0, 0)),
            pl.BlockSpec((None, 1, d), layer),
            pl.BlockSpec((f, d), lambda i: (0, 0), pipeline_mode=pl.Buffered(1)),
        ],
        out_specs=pl.BlockSpec((TM, d), lambda i: (i, 0)),
        out_shape=jax.ShapeDtypeStruct((t, d), F32),
        compiler_params=pltpu.CompilerParams(
            dimension_semantics=("arbitrary",),
            vmem_limit_bytes=60 * MIB),
        name="ffn_down",
    )(tt, xa, mods, gpost, w2)


def _rope_tables(seq):
    pos = jnp.arange(seq, dtype=jnp.int32)
    row = (pos // GRID_W).astype(F32)[:, None]
    colp = (pos % GRID_W).astype(F32)[:, None]

    def cs(n):
        freqs = jnp.power(jnp.float32(ROPE_THETA), -jnp.arange(n, dtype=F32) / n)[None, :]
        return (jnp.cos(row * freqs), jnp.sin(row * freqs), jnp.cos(colp * freqs), jnp.sin(colp * freqs))

    cr, sr, cc, sc = cs(HEAD_DIM // 4)
    cos128 = jnp.concatenate([cr, cr, cc, cc], axis=1)
    sin128 = jnp.concatenate([-sr, sr, -sc, sc], axis=1)
    cr, sr, cc, sc = cs(C_ROPE // 4)
    pad1 = jnp.ones((seq, LANES - C_ROPE), F32)
    pad0 = jnp.zeros((seq, LANES - C_ROPE), F32)
    cos64 = jnp.concatenate([cr, cr, cc, cc, pad1], axis=1)
    sin64 = jnp.concatenate([-sr, sr, -sc, sc, pad0], axis=1)
    ident_c = jnp.ones((TM, LANES), F32)
    ident_s = jnp.zeros((TM, LANES), F32)
    return tuple(jnp.concatenate([tab, ident], axis=0)
                 for tab, ident in ((cos128, ident_c), (sin128, ident_s), (cos64, ident_c), (sin64, ident_s)))


def kernel(x, c, ctx, c_ctx, w_mod, b_mod, g_mix_pre, g_mix_post, g_ffn_pre, g_ffn_post, w_in, sink_a,
           g_qn_b, g_kn_b, g_cq, w_uq, g_ckv, w_ukv, w_out, w_ffn1, w_ffn3, w_ffn2):
    batch, seq, d = x.shape
    ctx_len = ctx.shape[1]
    depth = w_mod.shape[0]
    n_lat, n_ctx = batch * seq, batch * ctx_len
    assert seq % TM == 0 and n_ctx % TM == 0 and seq % ctx_len == 0 and ctx_len % TQ == 0
    assert seq % TQ_GQA == 0 and seq % TQ_MLA == 0 and seq % TK == 0 and ctx_len % LANES == 0
    assert batch + 1 <= MOD_ROWS and seq % GRID_W == 0 and seq % (SUBTILES_WIN * TQ) == 0
    assert n_lat % (16 * N_UP_TILES) == 0 and (n_lat + n_ctx) % (16 * N_UP_TILES) == 0
    tiles_per_seq = seq // TM
    n_lat_tiles = n_lat // TM

    cv = jnp.zeros((MOD_ROWS, d), F32).at[:batch].set(c).at[batch].set(c_ctx)
    mod_all = _mod_call(cv, w_mod, b_mod)
    mods = mod_all[:, :batch + 1].reshape(depth, batch + 1, 6, d)
    mods = jnp.pad(mods, ((0, 0), (0, 0), (0, MOD_ROWS - 6), (0, 0)))

    tabs = _rope_tables(seq)

    win = w_in.astype(BF16)
    wuq = w_uq.reshape(depth, C_Q_LORA, C_HEADS, C_NOPE + C_ROPE)
    wuq = jnp.pad(wuq, ((0, 0), (0, 0), (0, 0), (0, C_QK_PAD - C_NOPE - C_ROPE)))
    wuq = wuq.reshape(depth, C_Q_LORA, C_HEADS * C_QK_PAD).astype(BF16)
    wukv = w_ukv.reshape(depth, C_KV_LORA, C_HEADS, 2, C_NOPE).transpose(0, 1, 3, 2, 4)
    wukv = wukv.reshape(depth, C_KV_LORA, 2 * C_HEADS * C_NOPE).astype(BF16)
    vecs = lambda v: v.reshape(depth, 1, -1)
    g_mix_pre, g_mix_post, g_ffn_pre, g_ffn_post = map(vecs, (g_mix_pre, g_mix_post, g_ffn_pre, g_ffn_post))
    g_qn_b, g_kn_b, g_cq, g_ckv = map(vecs, (g_qn_b, g_kn_b, g_cq, g_ckv))

    xa = jnp.concatenate([x.reshape(n_lat, d), ctx.reshape(n_ctx, d)], axis=0)
    sc_ab = 1.0 / math.sqrt(HEAD_DIM)
    sc_c = 1.0 / math.sqrt(C_NOPE + C_ROPE)
    dims = dict(batch=batch, seq=seq, ctx_len=ctx_len)

    for l in range(depth):
        last = l == depth - 1
        qa, ka, va, qb, kb, vtb, qc, kc, vtc, wo = _qkv_call(
            l, xa, mods, g_mix_pre, win, tabs, g_qn_b, g_kn_b, g_cq, g_ckv, wuq, wukv, w_out,
            n_lat_tiles=n_lat_tiles, tiles_per_seq=tiles_per_seq, n_seq=batch)
        oa = _attn_window_call(l, qa, ka, va, sink_a, n_groups=A_KV, heads_per_group=A_HEADS // A_KV,
                               dk=HEAD_DIM, dv=HEAD_DIM, scale=sc_ab, **dims)
        ob = _attn_dense_call(qb, kb, vtb, n_groups=B_KV, heads_per_group=B_HEADS // B_KV, dk=HEAD_DIM,
                              dv=HEAD_DIM, scale=sc_ab, tq=TQ_GQA, **dims)
        oc = _attn_dense_call(qc, kc, vtc, n_groups=C_HEADS, heads_per_group=1, dk=C_QK_PAD,
                              dv=C_V, scale=sc_c, tq=TQ_MLA, **dims)
        ctx_outs = None if last else _attn_ctx_call(l, sink_a, qa, ka, va, qb, kb, vtb, qc, kc, vtc,
                                                    scale_ab=sc_ab, scale_c=sc_c, **dims)
        n_tiles = n_lat_tiles if last else n_lat_tiles + n_ctx // TM
        xa, h = _outproj_call(l, (oa, ob, oc), ctx_outs, xa, mods, g_mix_post, g_ffn_pre, wo, n_tiles=n_tiles,
                              n_lat_tiles=n_lat_tiles, tiles_per_seq=tiles_per_seq, n_seq=batch)
        tt, w2 = _ffn_up_call(l, h, w_ffn1, w_ffn3, w_ffn2)
        xa = _ffn_down_call(l, tt, xa, mods, g_ffn_post, w2, tiles_per_seq=tiles_per_seq, n_seq=batch)
    return xa.reshape(batch, seq, d)
```

```python
import functools
import math

import jax
import jax.numpy as jnp
import numpy as np
from jax import lax
from jax.experimental import pallas as pl
from jax.experimental.pallas import tpu as pltpu

GRID_W = 64
HEAD_DIM = 128
A_HEADS, A_KV = 6, 2
B_HEADS, B_KV = 6, 2
WINDOW = 128
C_HEADS = 4
C_Q_LORA, C_KV_LORA = 512, 256
C_NOPE, C_ROPE, C_V = 128, 64, 128
C_QK_PAD = 256
MLA_SHIFT_LANE = C_NOPE + C_ROPE
ROPE_THETA = 10000.0
EPS = 1e-6
NEG = -1e30
LOG2E = math.log2(math.e)
SAFE_SHIFT_LOG2 = 50.0
BOUND_SLACK = 1.001

LANES = 128
MOD_ROWS = 8
TM = 512
TQ = 256
SUBTILES_WIN = 2
TQ_GQA = 256
TQ_MLA = 1024
TK = 512
TF = 512
N_UP_TILES = 8
TN_MOD = 1024
MIB = 1024 * 1024

F32 = jnp.float32
BF16 = jnp.bfloat16


def _rmsnorm(x, g):
    return x * lax.rsqrt(jnp.mean(x * x, axis=-1, keepdims=True) + EPS) * g


def _dot(a, b):
    return jnp.dot(a, b, preferred_element_type=F32)


def _dot_nt(a, b):
    return lax.dot_general(a, b, (((1,), (1,)), ((), ())), preferred_element_type=F32)


def _dot_tn(a, b):
    return lax.dot_general(a, b, (((0,), (0,)), ((), ())), preferred_element_type=F32)


def _col(i):
    return slice(i * LANES, (i + 1) * LANES)


def _mod_kernel(c_ref, w_ref, b_ref, o_ref):
    cv = c_ref[...]
    s = cv * jax.nn.sigmoid(cv)
    o_ref[...] = _dot(s.astype(BF16), w_ref[...].astype(BF16)) + b_ref[...]


def _mod_call(cv, w_mod, b_mod):
    depth, d, n = w_mod.shape
    return pl.pallas_call(
        _mod_kernel,
        grid=(depth, n // TN_MOD),
        in_specs=[
            pl.BlockSpec((MOD_ROWS, d), lambda l, j: (0, 0)),
            pl.BlockSpec((None, d, TN_MOD), lambda l, j: (l, 0, j)),
            pl.BlockSpec((None, 1, TN_MOD), lambda l, j: (l, 0, j)),
        ],
        out_specs=pl.BlockSpec((None, MOD_ROWS, TN_MOD), lambda l, j: (l, 0, j)),
        out_shape=jax.ShapeDtypeStruct((depth, MOD_ROWS, n), F32),
        compiler_params=pltpu.CompilerParams(
            dimension_semantics=("arbitrary", "arbitrary"),
            vmem_limit_bytes=40 * MIB),
        name="adaln_mod",
    )(cv, w_mod, b_mod.reshape(depth, 1, n))


def _qkv_kernel(x_ref, mod_ref, gpre_ref, win_ref, cos_ref, sin_ref, cos64_ref, sin64_ref,
                gqn_ref, gkn_ref, gcq_ref, gckv_ref, wuq_ref, wukv_ref, wo_ref, win_next_ref,
                qa_ref, ka_ref, va_ref, qb_ref, kb_ref, vtb_ref, qc_ref, kc_ref, vtc_ref, wob_ref, win_next_b_ref):
    wob_ref[...] = wo_ref[...].astype(BF16)
    win_next_b_ref[...] = win_next_ref[...].astype(BF16)
    m = mod_ref[...]
    tr = TM // 2
    lane = lax.broadcasted_iota(jnp.int32, (tr, LANES), 1)
    low32 = (lane & 32) == 0
    low16 = (lane & 16) == 0

    for half in range(2):
        rows = slice(half * tr, (half + 1) * tr)
        h = _rmsnorm(x_ref[rows, :], gpre_ref[...]) * (1.0 + m[1:2]) + m[0:1]
        p = _dot(h.astype(BF16), win_ref[...])
        cos, sin = cos_ref[rows, :], sin_ref[rows, :]
        cos64, sin64 = cos64_ref[rows, :], sin64_ref[rows, :]

        def rope128(t):
            partner = jnp.where(low32, pltpu.roll(t, LANES - 32, 1), pltpu.roll(t, 32, 1))
            return t * cos + partner * sin

        def rope64(t):
            partner = jnp.where(low16, pltpu.roll(t, LANES - 16, 1), pltpu.roll(t, 16, 1))
            return t * cos64 + partner * sin64

        off = 0
        for i in range(A_HEADS):
            qa_ref[rows, _col(i)] = rope128(p[:, _col(off + i)]).astype(BF16)
        off += A_HEADS
        for i in range(A_KV):
            ka_ref[rows, _col(i)] = rope128(p[:, _col(off + i)]).astype(BF16)
        off += A_KV
        for i in range(A_KV):
            va_ref[rows, _col(i)] = p[:, _col(off + i)].astype(BF16)
        off += A_KV

        gqn, gkn = gqn_ref[...], gkn_ref[...]
        for i in range(B_HEADS):
            qb_ref[rows, _col(i)] = rope128(_rmsnorm(p[:, _col(off + i)], gqn)).astype(BF16)
        off += B_HEADS
        for i in range(B_KV):
            kb_ref[rows, _col(i)] = rope128(_rmsnorm(p[:, _col(off + i)], gkn)).astype(BF16)
        off += B_KV
        for i in range(B_KV):
            vtb_ref[_col(i), rows] = p[:, _col(off + i)].T.astype(BF16)
        off += B_KV

        c0 = off * LANES
        cq = _rmsnorm(p[:, c0:c0 + C_Q_LORA], gcq_ref[...])
        q = _dot(cq.astype(BF16), wuq_ref[...])
        c0 += C_Q_LORA
        ckv = _rmsnorm(p[:, c0:c0 + C_KV_LORA], gckv_ref[...])
        kv = _dot(ckv.astype(BF16), wukv_ref[...])
        c0 += C_KV_LORA
        ckr = jnp.concatenate([p[:, c0:c0 + C_ROPE], jnp.zeros((tr, LANES - C_ROPE), F32)], axis=1)
        krope = jnp.where(lane == C_ROPE, 1.0, rope64(ckr)).astype(BF16)
        for i in range(C_HEADS):
            qc_ref[rows, _col(2 * i)] = q[:, _col(2 * i)].astype(BF16)
            qc_ref[rows, _col(2 * i + 1)] = rope64(q[:, _col(2 * i + 1)]).astype(BF16)
            kc_ref[rows, _col(2 * i)] = kv[:, _col(i)].astype(BF16)
            kc_ref[rows, _col(2 * i + 1)] = krope
            vtc_ref[_col(i), rows] = kv[:, _col(C_HEADS + i)].T.astype(BF16)


def _qkv_call(l, xa, mods, gpre, win, tabs, gqn, gkn, gcq, gckv, wuq, wukv, w_out, w_in, *,
              n_lat_tiles, tiles_per_seq, n_seq):
    t, d = xa.shape
    n_tiles = t // TM
    d_in = win.shape[1]
    d_mix = w_out.shape[1]
    l_next = min(l + 1, w_in.shape[0] - 1)
    assert n_tiles >= d_mix // LANES and n_tiles >= d // LANES
    wo_blk = lambda i: jnp.minimum(i, d_mix // LANES - 1)
    win_blk = lambda i: jnp.minimum(i, d // LANES - 1)

    def seg(i):
        return jnp.minimum(i // tiles_per_seq, n_seq)

    def rope_blk(i):
        return jnp.where(i < n_lat_tiles, i % tiles_per_seq, tiles_per_seq)

    layer = lambda i: (l, 0, 0)
    vec = lambda n: pl.BlockSpec((None, 1, n), layer)
    tab_spec = pl.BlockSpec((TM, LANES), lambda i: (rope_blk(i), 0))
    row_major = lambda w: (pl.BlockSpec((TM, w), lambda i: (i, 0)), jax.ShapeDtypeStruct((t, w), BF16))
    col_major = lambda w: (pl.BlockSpec((w, TM), lambda i: (0, i)), jax.ShapeDtypeStruct((w, t), BF16))
    outs = [row_major(A_HEADS * HEAD_DIM), row_major(A_KV * HEAD_DIM), row_major(A_KV * HEAD_DIM),
            row_major(B_HEADS * HEAD_DIM), row_major(B_KV * HEAD_DIM), col_major(B_KV * HEAD_DIM),
            row_major(C_HEADS * C_QK_PAD), row_major(C_HEADS * C_QK_PAD), col_major(C_HEADS * C_V),
            (pl.BlockSpec((LANES, d), lambda i: (wo_blk(i), 0)), jax.ShapeDtypeStruct((d_mix, d), BF16)),
            (pl.BlockSpec((LANES, d_in), lambda i: (win_blk(i), 0)), jax.ShapeDtypeStruct((d, d_in), BF16))]
    return pl.pallas_call(
        _qkv_kernel,
        grid=(n_tiles,),
        in_specs=[
            pl.BlockSpec((TM, d), lambda i: (i, 0)),
            pl.BlockSpec((None, None, MOD_ROWS, d), lambda i: (l, seg(i), 0, 0)),
            vec(d),
            pl.BlockSpec((d, d_in), lambda i: (0, 0), pipeline_mode=pl.Buffered(1)),
            tab_spec, tab_spec, tab_spec, tab_spec,
            vec(HEAD_DIM), vec(HEAD_DIM), vec(C_Q_LORA), vec(C_KV_LORA),
            pl.BlockSpec((None,) + wuq.shape[1:], layer, pipeline_mode=pl.Buffered(1)),
            pl.BlockSpec((None,) + wukv.shape[1:], layer, pipeline_mode=pl.Buffered(1)),
            pl.BlockSpec((None, LANES, d), lambda i: (l, wo_blk(i), 0)),
            pl.BlockSpec((None, LANES, d_in), lambda i: (l_next, win_blk(i), 0)),
        ],
        out_specs=[o[0] for o in outs],
        out_shape=[o[1] for o in outs],
        compiler_params=pltpu.CompilerParams(
            dimension_semantics=("arbitrary",),
            vmem_limit_bytes=56 * MIB),
        name="qkv_prep",
    )(xa, mods, gpre, win, *tabs, gqn, gkn, gcq, gckv, wuq, wukv, w_out, w_in)


def _attn_window_kernel(sink_ref, q_ref, kl_ref, vl_ref, kc_ref, vc_ref, o_ref, *,
                        layer, heads_per_group, dk, dv, scale, seq):
    g = pl.program_id(1)
    qi = pl.program_id(2)
    c2 = scale * LOG2E
    band = TQ + 2 * WINDOW
    heads = range(heads_per_group)
    sink2 = jnp.concatenate(
        [jnp.full((1, TQ), sink_ref[layer, g * heads_per_group + r] * LOG2E, F32) for r in heads], axis=1)
    for sub in range(SUBTILES_WIN):
        rows = slice(sub * TQ, (sub + 1) * TQ)
        q0 = (qi * SUBTILES_WIN + sub) * TQ
        q = jnp.concatenate([q_ref[rows, r * dk:(r + 1) * dk] for r in heads], axis=0)
        start = pl.multiple_of(jnp.clip(q0 - WINDOW, 0, seq - band), LANES)
        s_c = _dot_nt(kc_ref[...], q)
        s_l = _dot_nt(kl_ref[pl.ds(start, band), :], q)
        kpos = start + lax.broadcasted_iota(jnp.int32, (band, TQ), 0)
        qpos = q0 + lax.broadcasted_iota(jnp.int32, (band, TQ), 1)
        in_window = jnp.abs(kpos - qpos) <= WINDOW
        s_l = jnp.where(jnp.concatenate([in_window] * heads_per_group, axis=1), s_l, NEG)
        m = jnp.maximum(jnp.maximum(jnp.max(s_c, axis=0, keepdims=True), jnp.max(s_l, axis=0, keepdims=True)) * c2,
                        sink2)
        p_c = jnp.exp2(s_c * c2 - m)
        p_l = jnp.exp2(s_l * c2 - m)
        denom = (jnp.sum(p_c, axis=0, keepdims=True) + jnp.sum(p_l, axis=0, keepdims=True)
                 + jnp.exp2(sink2 - m))
        acc = _dot_tn(vc_ref[...], p_c.astype(BF16)) + _dot_tn(vl_ref[pl.ds(start, band), :], p_l.astype(BF16))
        o = acc / denom
        for r in heads:
            o_ref[rows, r * dv:(r + 1) * dv] = o[:, r * TQ:(r + 1) * TQ].T.astype(BF16)


def _attn_window_call(l, q, k, v, sink, *, n_groups, heads_per_group, dk, dv, scale, batch, seq, ctx_len):
    tq = SUBTILES_WIN * TQ
    n_q = seq // tq
    ctx_blk0 = batch * seq // ctx_len
    return pl.pallas_call(
        functools.partial(_attn_window_kernel, layer=l, heads_per_group=heads_per_group, dk=dk, dv=dv,
                          scale=scale, seq=seq),
        grid=(batch, n_groups, n_q),
        in_specs=[
            pl.BlockSpec(memory_space=pltpu.SMEM),
            pl.BlockSpec((tq, heads_per_group * dk), lambda b, g, qi: (b * n_q + qi, g)),
            pl.BlockSpec((seq, dk), lambda b, g, qi: (b, g)),
            pl.BlockSpec((seq, dv), lambda b, g, qi: (b, g)),
            pl.BlockSpec((ctx_len, dk), lambda b, g, qi: (ctx_blk0 + b, g)),
            pl.BlockSpec((ctx_len, dv), lambda b, g, qi: (ctx_blk0 + b, g)),
        ],
        out_specs=pl.BlockSpec((tq, heads_per_group * dv), lambda b, g, qi: (b * n_q + qi, g)),
        out_shape=jax.ShapeDtypeStruct((batch * seq, n_groups * heads_per_group * dv), BF16),
        compiler_params=pltpu.CompilerParams(
            dimension_semantics=("arbitrary", "arbitrary", "arbitrary"),
            vmem_limit_bytes=48 * MIB),
        name="attn_window",
    )(sink, q, k, v, k, v)


def _attn_dense_kernel(*refs, heads_per_group, dk, dv, scale, tq, shift_lane, cast_side_job):
    if cast_side_job:
        q_ref, kl_ref, vtl_ref, kc_ref, vtc_ref, w_ref, o_ref, wb_ref, k2_scr = refs
        wb_ref[...] = w_ref[...].astype(BF16)
    else:
        q_ref, kl_ref, vtl_ref, kc_ref, vtc_ref, o_ref, k2_scr = refs
    c2 = scale * LOG2E
    seq = kl_ref.shape[0]

    @pl.when(pl.program_id(2) == 0)
    def _():
        kf = kl_ref[...].astype(F32)
        kcf = kc_ref[...].astype(F32)
        k2_scr[0] = jnp.maximum(jnp.max(jnp.sum(kf * kf, axis=-1, keepdims=True)),
                                jnp.max(jnp.sum(kcf * kcf, axis=-1, keepdims=True)))

    if heads_per_group > 1:
        q = jnp.concatenate([q_ref[:, r * dk:(r + 1) * dk] for r in range(heads_per_group)], axis=0)
    else:
        q = q_ref[...]
    qf = q.astype(F32)
    q2 = jnp.sum(qf * qf, axis=-1, keepdims=True)
    bound = jnp.sqrt(q2 * k2_scr[0]) * BOUND_SLACK
    bound_is_safe = jnp.max(bound) * c2 <= SAFE_SHIFT_LOG2

    def finish(acc, denom):
        o = acc / denom
        for r in range(heads_per_group):
            o_ref[:, r * dv:(r + 1) * dv] = o[:, r * tq:(r + 1) * tq].T.astype(BF16)

    @pl.when(bound_is_safe)
    def _():
        if shift_lane is None:
            shift = bound.T * c2
            p_c = jnp.exp2(_dot_nt(kc_ref[...], q) * c2 - shift)
            p_l = jnp.exp2(_dot_nt(kl_ref[...], q) * c2 - shift)
        else:
            lane = lax.broadcasted_iota(jnp.int32, q.shape, 1)
            q_shifted = jnp.where(lane == shift_lane, -bound, qf).astype(BF16)
            p_c = jnp.exp2(_dot_nt(kc_ref[...], q_shifted) * c2)
            p_l = jnp.exp2(_dot_nt(kl_ref[...], q_shifted) * c2)
        acc = _dot(vtc_ref[...], p_c.astype(BF16)) + _dot(vtl_ref[...], p_l.astype(BF16))
        finish(acc, jnp.sum(p_c, axis=0, keepdims=True) + jnp.sum(p_l, axis=0, keepdims=True))

    @pl.when(jnp.logical_not(bound_is_safe))
    def _():
        chunks = [(lambda: kc_ref[...], lambda: vtc_ref[...])]
        for j in range(seq // TK):
            chunks.append((functools.partial(lambda j: kl_ref[j * TK:(j + 1) * TK, :], j),
                           functools.partial(lambda j: vtl_ref[:, j * TK:(j + 1) * TK], j)))
        n = len(chunks)
        s_next = _dot_nt(chunks[0][0](), q)
        m = denom = acc = None
        for j in range(n):
            s = s_next
            if j + 1 < n:
                s_next = _dot_nt(chunks[j + 1][0](), q)
            m_j = jnp.max(s, axis=0, keepdims=True) * c2
            m_new = m_j if j == 0 else jnp.maximum(m, m_j)
            p = jnp.exp2(s * c2 - m_new)
            pv = _dot(chunks[j][1](), p.astype(BF16))
            if j == 0:
                denom = jnp.sum(p, axis=0, keepdims=True)
                acc = pv
            else:
                alpha = jnp.exp2(m - m_new)
                denom = denom * alpha + jnp.sum(p, axis=0, keepdims=True)
                acc = acc * alpha + pv
            m = m_new
        finish(acc, denom)


def _attn_dense_call(q, k, vt, *, n_groups, heads_per_group, dk, dv, scale, batch, seq, ctx_len, tq,
                     shift_lane=None, cast_weight=None):
    n_q = seq // tq
    ctx_blk0 = batch * seq // ctx_len
    in_specs = [
        pl.BlockSpec((tq, heads_per_group * dk), lambda b, g, qi: (b * n_q + qi, g)),
        pl.BlockSpec((seq, dk), lambda b, g, qi: (b, g)),
        pl.BlockSpec((dv, seq), lambda b, g, qi: (g, b)),
        pl.BlockSpec((ctx_len, dk), lambda b, g, qi: (ctx_blk0 + b, g)),
        pl.BlockSpec((dv, ctx_len), lambda b, g, qi: (g, ctx_blk0 + b)),
    ]
    args = [q, k, vt, k, vt]
    out_specs = [pl.BlockSpec((tq, heads_per_group * dv), lambda b, g, qi: (b * n_q + qi, g))]
    out_shape = [jax.ShapeDtypeStruct((batch * seq, n_groups * heads_per_group * dv), BF16)]
    if cast_weight is not None:
        l, w = cast_weight
        n_steps = batch * n_groups * n_q
        rows, cols = w.shape[1] // n_steps, w.shape[2]
        assert rows * n_steps == w.shape[1] and rows % 16 == 0
        step = lambda b, g, qi: (b * n_groups + g) * n_q + qi
        in_specs.append(pl.BlockSpec((None, rows, cols), lambda b, g, qi: (l, step(b, g, qi), 0)))
        args.append(w)
        out_specs.append(pl.BlockSpec((rows, cols), lambda b, g, qi: (step(b, g, qi), 0)))
        out_shape.append(jax.ShapeDtypeStruct(w.shape[1:], BF16))
    outs = pl.pallas_call(
        functools.partial(_attn_dense_kernel, heads_per_group=heads_per_group, dk=dk, dv=dv, scale=scale, tq=tq,
                          shift_lane=shift_lane, cast_side_job=cast_weight is not None),
        grid=(batch, n_groups, n_q),
        in_specs=in_specs,
        out_specs=out_specs,
        out_shape=out_shape,
        scratch_shapes=[pltpu.SMEM((1,), F32)],
        compiler_params=pltpu.CompilerParams(
            dimension_semantics=("arbitrary", "arbitrary", "arbitrary"),
            vmem_limit_bytes=48 * MIB),
        name=f"attn_dense_dk{dk}",
    )(*args)
    return outs if cast_weight is not None else outs[0]


def _attn_ctx_kernel(sink_ref, qa_ref, ka_ref, va_ref, qb_ref, kb_ref, vtb_ref, qc_ref, kc_ref, vtc_ref,
                     oa_ref, ob_ref, oc_ref, *, layer, scale_ab, scale_c):
    def head(q, k, scale, pv, sink2=None):
        c2 = scale * LOG2E
        s = _dot_nt(q, k)
        m = jnp.max(s, axis=-1, keepdims=True) * c2
        if sink2 is not None:
            m = jnp.maximum(m, sink2)
        p = jnp.exp2(s * c2 - m)
        denom = jnp.sum(p, axis=-1, keepdims=True)
        if sink2 is not None:
            denom = denom + jnp.exp2(sink2 - m)
        return (pv(p.astype(BF16)) / denom).astype(BF16)

    for hd in range(A_HEADS):
        g = hd // (A_HEADS // A_KV)
        oa_ref[:, _col(hd)] = head(qa_ref[:, _col(hd)], ka_ref[:, _col(g)], scale_ab,
                                   lambda p: _dot(p, va_ref[:, _col(g)]), sink_ref[layer, hd] * LOG2E)
    for hd in range(B_HEADS):
        g = hd // (B_HEADS // B_KV)
        ob_ref[:, _col(hd)] = head(qb_ref[:, _col(hd)], kb_ref[:, _col(g)], scale_ab,
                                   lambda p: _dot_nt(p, vtb_ref[_col(g), :]))
    for hd in range(C_HEADS):
        qk = slice(hd * C_QK_PAD, (hd + 1) * C_QK_PAD)
        oc_ref[:, _col(hd)] = head(qc_ref[:, qk], kc_ref[:, qk], scale_c,
                                   lambda p: _dot_nt(p, vtc_ref[_col(hd), :]))


def _attn_ctx_call(l, sink, qa, ka, va, qb, kb, vtb, qc, kc, vtc, *, scale_ab, scale_c, batch, seq, ctx_len):
    blk0 = batch * seq // ctx_len
    rows = lambda a: pl.BlockSpec((ctx_len, a.shape[1]), lambda b: (blk0 + b, 0))
    cols = lambda a: pl.BlockSpec((a.shape[0], ctx_len), lambda b: (0, blk0 + b))
    out = lambda w: (pl.BlockSpec((ctx_len, w), lambda b: (b, 0)),
                     jax.ShapeDtypeStruct((batch * ctx_len, w), BF16))
    outs = [out(A_HEADS * HEAD_DIM), out(B_HEADS * HEAD_DIM), out(C_HEADS * C_V)]
    return pl.pallas_call(
        functools.partial(_attn_ctx_kernel, layer=l, scale_ab=scale_ab, scale_c=scale_c),
        grid=(batch,),
        in_specs=[pl.BlockSpec(memory_space=pltpu.SMEM), rows(qa), rows(ka), rows(va),
                  rows(qb), rows(kb), cols(vtb), rows(qc), rows(kc), cols(vtc)],
        out_specs=[o[0] for o in outs],
        out_shape=[o[1] for o in outs],
        compiler_params=pltpu.CompilerParams(
            dimension_semantics=("arbitrary",),
            vmem_limit_bytes=32 * MIB),
        name="attn_ctx",
    )(sink, qa, ka, va, qb, kb, vtb, qc, kc, vtc)


def _outproj_kernel(*refs, n_lat_tiles, has_ctx):
    lat_refs, refs = refs[:3], refs[3:]
    if has_ctx:
        ctx_refs, refs = refs[:3], refs[3:]
    x_ref, mod_ref, gpost_ref, gffn_ref, wo_ref, xo_ref, h_ref = refs
    m = mod_ref[...]
    for half in range(2):
        rows = slice(half * (TM // 2), (half + 1) * (TM // 2))
        o, w0 = None, 0
        for g, lat_ref in enumerate(lat_refs):
            og = lat_ref[rows, :]
            if has_ctx:
                og = jnp.where(pl.program_id(0) >= n_lat_tiles, ctx_refs[g][rows, :], og)
            width = lat_ref.shape[1]
            part = _dot(og, wo_ref[w0:w0 + width, :])
            o = part if o is None else o + part
            w0 += width
        x_new = x_ref[rows, :] + m[2:3] * _rmsnorm(o, gpost_ref[...])
        xo_ref[rows, :] = x_new
        h_ref[rows, :] = (_rmsnorm(x_new, gffn_ref[...]) * (1.0 + m[4:5]) + m[3:4]).astype(BF16)


def _outproj_call(l, lat_outs, ctx_outs, xa, mods, gpost, gffn, wo, *, n_tiles, n_lat_tiles, tiles_per_seq, n_seq):
    t = n_tiles * TM
    d = xa.shape[1]
    seg = lambda i: jnp.minimum(i // tiles_per_seq, n_seq)
    layer = lambda i: (l, 0, 0)
    has_ctx = ctx_outs is not None
    lat_rows = lambda a: pl.BlockSpec((TM, a.shape[1]), lambda i: (jnp.minimum(i, n_lat_tiles - 1), 0))
    ctx_rows = lambda a: pl.BlockSpec((TM, a.shape[1]), lambda i: (jnp.maximum(i - n_lat_tiles, 0), 0))
    in_specs = [lat_rows(a) for a in lat_outs]
    args = list(lat_outs)
    if has_ctx:
        in_specs += [ctx_rows(a) for a in ctx_outs]
        args += list(ctx_outs)
    in_specs += [
        pl.BlockSpec((TM, d), lambda i: (i, 0)),
        pl.BlockSpec((None, None, MOD_ROWS, d), lambda i: (l, seg(i), 0, 0)),
        pl.BlockSpec((None, 1, d), layer),
        pl.BlockSpec((None, 1, d), layer),
        pl.BlockSpec(wo.shape, lambda i: (0, 0), pipeline_mode=pl.Buffered(1)),
    ]
    args += [xa, mods, gpost, gffn, wo]
    return pl.pallas_call(
        functools.partial(_outproj_kernel, n_lat_tiles=n_lat_tiles, has_ctx=has_ctx),
        grid=(n_tiles,),
        in_specs=in_specs,
        out_specs=[pl.BlockSpec((TM, d), lambda i: (i, 0)), pl.BlockSpec((TM, d), lambda i: (i, 0))],
        out_shape=[jax.ShapeDtypeStruct((t, d), F32), jax.ShapeDtypeStruct((t, d), BF16)],
        compiler_params=pltpu.CompilerParams(
            dimension_semantics=("arbitrary",),
            vmem_limit_bytes=48 * MIB),
        name="outproj",
    )(*args)


def _ffn_up_kernel(h_ref, w1_ref, w3_ref, t_ref, w1_scr, w3_scr):
    @pl.when(pl.program_id(1) == 0)
    def _():
        w1_scr[...] = w1_ref[...].astype(BF16)
        w3_scr[...] = w3_ref[...].astype(BF16)

    h = h_ref[...]
    a = _dot(h, w1_scr[...])
    b = _dot(h, w3_scr[...])
    t_ref[...] = (a * jax.nn.sigmoid(a) * b).astype(BF16)


def _ffn_up_call(l, h, w1, w3):
    t, d = h.shape
    f = w1.shape[2]
    tm = t // N_UP_TILES
    return pl.pallas_call(
        _ffn_up_kernel,
        grid=(f // TF, N_UP_TILES),
        in_specs=[
            pl.BlockSpec((tm, d), lambda j, i: (i, 0)),
            pl.BlockSpec((None, d, TF), lambda j, i: (l, 0, j)),
            pl.BlockSpec((None, d, TF), lambda j, i: (l, 0, j)),
        ],
        out_specs=pl.BlockSpec((tm, TF), lambda j, i: (i, j)),
        out_shape=jax.ShapeDtypeStruct((t, f), BF16),
        scratch_shapes=[pltpu.VMEM((d, TF), BF16), pltpu.VMEM((d, TF), BF16)],
        compiler_params=pltpu.CompilerParams(
            dimension_semantics=("arbitrary", "arbitrary"),
            vmem_limit_bytes=48 * MIB),
        name="ffn_up",
    )(h, w1, w3)


def _ffn_down_kernel(t_ref, x_ref, mod_ref, gpost_ref, w2_ref, o_ref):
    y = _dot(t_ref[...], w2_ref[...])
    m = mod_ref[...]
    o_ref[...] = x_ref[...] + m[5:6] * _rmsnorm(y, gpost_ref[...])


def _ffn_down_call(l, tt, xa, mods, gpost, w2, *, tiles_per_seq, n_seq):
    t, d = xa.shape
    f = w2.shape[0]
    seg = lambda i: jnp.minimum(i // tiles_per_seq, n_seq)
    layer = lambda i: (l, 0, 0)
    return pl.pallas_call(
        _ffn_down_kernel,
        grid=(t // TM,),
        in_specs=[
            pl.BlockSpec((TM, f), lambda i: (i, 0)),
            pl.BlockSpec((TM, d), lambda i: (i, 0)),
            pl.BlockSpec((None, None, MOD_ROWS, d), lambda i: (l, seg(i), 0, 0)),
            pl.BlockSpec((None, 1, d), layer),
            pl.BlockSpec((f, d), lambda i: (0, 0), pipeline_mode=pl.Buffered(1)),
        ],
        out_specs=pl.BlockSpec((TM, d), lambda i: (i, 0)),
        out_shape=jax.ShapeDtypeStruct((t, d), F32),
        compiler_params=pltpu.CompilerParams(
            dimension_semantics=("arbitrary",),
            vmem_limit_bytes=60 * MIB),
        name="ffn_down",
    )(tt, xa, mods, gpost, w2)


def _rope_tables(seq):
    pos = np.arange(seq)
    row = (pos // GRID_W).astype(np.float64)[:, None]
    colp = (pos % GRID_W).astype(np.float64)[:, None]

    def cs(n):
        freqs = np.power(np.float64(ROPE_THETA), -np.arange(n, dtype=np.float64) / n)[None, :]
        return (np.cos(row * freqs), np.sin(row * freqs), np.cos(colp * freqs), np.sin(colp * freqs))

    cr, sr, cc, sc = cs(HEAD_DIM // 4)
    cos128 = np.concatenate([cr, cr, cc, cc], axis=1)
    sin128 = np.concatenate([-sr, sr, -sc, sc], axis=1)
    cr, sr, cc, sc = cs(C_ROPE // 4)
    pad1 = np.ones((seq, LANES - C_ROPE), np.float32)
    pad0 = np.zeros((seq, LANES - C_ROPE), np.float32)
    cos64 = np.concatenate([cr, cr, cc, cc, pad1], axis=1)
    sin64 = np.concatenate([-sr, sr, -sc, sc, pad0], axis=1)
    ident_c = np.ones((TM, LANES), np.float32)
    ident_s = np.zeros((TM, LANES), np.float32)
    return tuple(jnp.asarray(np.concatenate([tab, ident], axis=0).astype(np.float32))
                 for tab, ident in ((cos128, ident_c), (sin128, ident_s), (cos64, ident_c), (sin64, ident_s)))


def kernel(x, c, ctx, c_ctx, w_mod, b_mod, g_mix_pre, g_mix_post, g_ffn_pre, g_ffn_post, w_in, sink_a,
           g_qn_b, g_kn_b, g_cq, w_uq, g_ckv, w_ukv, w_out, w_ffn1, w_ffn3, w_ffn2):
    batch, seq, d = x.shape
    ctx_len = ctx.shape[1]
    depth = w_mod.shape[0]
    n_lat, n_ctx = batch * seq, batch * ctx_len
    assert seq % TM == 0 and n_ctx % TM == 0 and seq % ctx_len == 0 and ctx_len % TQ == 0
    assert seq % TQ_GQA == 0 and seq % TQ_MLA == 0 and seq % TK == 0 and ctx_len % LANES == 0
    assert batch + 1 <= MOD_ROWS and seq % GRID_W == 0 and seq % (SUBTILES_WIN * TQ) == 0
    assert n_lat % (16 * N_UP_TILES) == 0 and (n_lat + n_ctx) % (16 * N_UP_TILES) == 0
    tiles_per_seq = seq // TM
    n_lat_tiles = n_lat // TM

    cv = jnp.zeros((MOD_ROWS, d), F32).at[:batch].set(c).at[batch].set(c_ctx)
    mod_all = _mod_call(cv, w_mod, b_mod)
    mods = mod_all[:, :batch + 1].reshape(depth, batch + 1, 6, d)
    mods = jnp.pad(mods, ((0, 0), (0, 0), (0, MOD_ROWS - 6), (0, 0)))

    tabs = _rope_tables(seq)

    win = w_in[0].astype(BF16)
    wuq = w_uq.reshape(depth, C_Q_LORA, C_HEADS, C_NOPE + C_ROPE)
    wuq = jnp.pad(wuq, ((0, 0), (0, 0), (0, 0), (0, C_QK_PAD - C_NOPE - C_ROPE)))
    wuq = wuq.reshape(depth, C_Q_LORA, C_HEADS * C_QK_PAD).astype(BF16)
    wukv = w_ukv.reshape(depth, C_KV_LORA, C_HEADS, 2, C_NOPE).transpose(0, 1, 3, 2, 4)
    wukv = wukv.reshape(depth, C_KV_LORA, 2 * C_HEADS * C_NOPE).astype(BF16)
    vecs = lambda v: v.reshape(depth, 1, -1)
    g_mix_pre, g_mix_post, g_ffn_pre, g_ffn_post = map(vecs, (g_mix_pre, g_mix_post, g_ffn_pre, g_ffn_post))
    g_qn_b, g_kn_b, g_cq, g_ckv = map(vecs, (g_qn_b, g_kn_b, g_cq, g_ckv))

    xa = jnp.concatenate([x.reshape(n_lat, d), ctx.reshape(n_ctx, d)], axis=0)
    sc_ab = 1.0 / math.sqrt(HEAD_DIM)
    sc_c = 1.0 / math.sqrt(C_NOPE + C_ROPE)
    dims = dict(batch=batch, seq=seq, ctx_len=ctx_len)

    for l in range(depth):
        last = l == depth - 1
        qa, ka, va, qb, kb, vtb, qc, kc, vtc, wo, win = _qkv_call(
            l, xa, mods, g_mix_pre, win, tabs, g_qn_b, g_kn_b, g_cq, g_ckv, wuq, wukv, w_out, w_in,
            n_lat_tiles=n_lat_tiles, tiles_per_seq=tiles_per_seq, n_seq=batch)
        oa = _attn_window_call(l, qa, ka, va, sink_a, n_groups=A_KV, heads_per_group=A_HEADS // A_KV,
                               dk=HEAD_DIM, dv=HEAD_DIM, scale=sc_ab, **dims)
        ob = _attn_dense_call(qb, kb, vtb, n_groups=B_KV, heads_per_group=B_HEADS // B_KV, dk=HEAD_DIM,
                              dv=HEAD_DIM, scale=sc_ab, tq=TQ_GQA, **dims)
        oc, w2 = _attn_dense_call(qc, kc, vtc, n_groups=C_HEADS, heads_per_group=1, dk=C_QK_PAD,
                                  dv=C_V, scale=sc_c, tq=TQ_MLA, shift_lane=MLA_SHIFT_LANE,
                                  cast_weight=(l, w_ffn2), **dims)
        ctx_outs = None if last else _attn_ctx_call(l, sink_a, qa, ka, va, qb, kb, vtb, qc, kc, vtc,
                                                    scale_ab=sc_ab, scale_c=sc_c, **dims)
        n_tiles = n_lat_tiles if last else n_lat_tiles + n_ctx // TM
        xa, h = _outproj_call(l, (oa, ob, oc), ctx_outs, xa, mods, g_mix_post, g_ffn_pre, wo, n_tiles=n_tiles,
                              n_lat_tiles=n_lat_tiles, tiles_per_seq=tiles_per_seq, n_seq=batch)
        tt = _ffn_up_call(l, h, w_ffn1, w_ffn3)
        xa = _ffn_down_call(l, tt, xa, mods, g_ffn_post, w2, tiles_per_seq=tiles_per_seq, n_seq=batch)
    return xa.reshape(batch, seq, d)
```

```python
import functools
import math

import jax
import jax.numpy as jnp
import numpy as np
from jax import lax
from jax.experimental import pallas as pl
from jax.experimental.pallas import tpu as pltpu

GRID_W = 64
HEAD_DIM = 128
A_HEADS, A_KV = 6, 2
B_HEADS, B_KV = 6, 2
WINDOW = 128
C_HEADS = 4
C_Q_LORA, C_KV_LORA = 512, 256
C_NOPE, C_ROPE, C_V = 128, 64, 128
C_QK_PAD = 256
MLA_SHIFT_LANE = C_NOPE + C_ROPE
VT_EXT = HEAD_DIM + 16
ROPE_THETA = 10000.0
EPS = 1e-6
NEG = -1e30
LOG2E = math.log2(math.e)
SAFE_SHIFT_LOG2 = 50.0
BOUND_SLACK = 1.001

LANES = 128
MOD_ROWS = 8
TM = 512
TQ = 256
SUBTILES_WIN = 2
TQ_GQA = 256
TQ_MLA = 1024
TK = 512
TF = 512
N_UP_TILES = 8
TN_MOD = 1024
MIB = 1024 * 1024

F32 = jnp.float32
BF16 = jnp.bfloat16


def _rmsnorm(x, g):
    return x * lax.rsqrt(jnp.mean(x * x, axis=-1, keepdims=True) + EPS) * g


def _dot(a, b):
    return jnp.dot(a, b, preferred_element_type=F32)


def _dot_nt(a, b):
    return lax.dot_general(a, b, (((1,), (1,)), ((), ())), preferred_element_type=F32)


def _dot_tn(a, b):
    return lax.dot_general(a, b, (((0,), (0,)), ((), ())), preferred_element_type=F32)


def _col(i):
    return slice(i * LANES, (i + 1) * LANES)


def _mod_kernel(c_ref, w_ref, b_ref, o_ref):
    cv = c_ref[...]
    s = cv * jax.nn.sigmoid(cv)
    o_ref[...] = _dot(s.astype(BF16), w_ref[...].astype(BF16)) + b_ref[...]


def _mod_call(cv, w_mod, b_mod):
    depth, d, n = w_mod.shape
    return pl.pallas_call(
        _mod_kernel,
        grid=(depth, n // TN_MOD),
        in_specs=[
            pl.BlockSpec((MOD_ROWS, d), lambda l, j: (0, 0)),
            pl.BlockSpec((None, d, TN_MOD), lambda l, j: (l, 0, j)),
            pl.BlockSpec((None, 1, TN_MOD), lambda l, j: (l, 0, j)),
        ],
        out_specs=pl.BlockSpec((None, MOD_ROWS, TN_MOD), lambda l, j: (l, 0, j)),
        out_shape=jax.ShapeDtypeStruct((depth, MOD_ROWS, n), F32),
        compiler_params=pltpu.CompilerParams(
            dimension_semantics=("arbitrary", "arbitrary"),
            vmem_limit_bytes=40 * MIB),
        name="adaln_mod",
    )(cv, w_mod, b_mod.reshape(depth, 1, n))


def _qkv_kernel(x_ref, mod_ref, gpre_ref, win_ref, cos_ref, sin_ref, cos64_ref, sin64_ref,
                gqn_ref, gkn_ref, gcq_ref, gckv_ref, wuq_ref, wukv_ref, wo_ref,
                qa_ref, ka_ref, va_ref, qb_ref, kb_ref, vtb_ref, qc_ref, kc_ref, vtc_ref, wob_ref):
    wob_ref[...] = wo_ref[...].astype(BF16)
    m = mod_ref[...]
    tr = TM // 2
    lane = lax.broadcasted_iota(jnp.int32, (tr, LANES), 1)
    low32 = (lane & 32) == 0
    low16 = (lane & 16) == 0

    for half in range(2):
        rows = slice(half * tr, (half + 1) * tr)
        h = _rmsnorm(x_ref[rows, :], gpre_ref[...]) * (1.0 + m[1:2]) + m[0:1]
        p = _dot(h.astype(BF16), win_ref[...])
        cos, sin = cos_ref[rows, :], sin_ref[rows, :]
        cos64, sin64 = cos64_ref[rows, :], sin64_ref[rows, :]

        def rope128(t):
            partner = jnp.where(low32, pltpu.roll(t, LANES - 32, 1), pltpu.roll(t, 32, 1))
            return t * cos + partner * sin

        def rope64(t):
            partner = jnp.where(low16, pltpu.roll(t, LANES - 16, 1), pltpu.roll(t, 16, 1))
            return t * cos64 + partner * sin64

        off = 0
        for i in range(A_HEADS):
            qa_ref[rows, _col(i)] = rope128(p[:, _col(off + i)]).astype(BF16)
        off += A_HEADS
        for i in range(A_KV):
            ka_ref[rows, _col(i)] = rope128(p[:, _col(off + i)]).astype(BF16)
        off += A_KV
        for i in range(A_KV):
            va_ref[rows, _col(i)] = p[:, _col(off + i)].astype(BF16)
        off += A_KV

        gqn, gkn = gqn_ref[...], gkn_ref[...]
        for i in range(B_HEADS):
            qb_ref[rows, _col(i)] = rope128(_rmsnorm(p[:, _col(off + i)], gqn)).astype(BF16)
        off += B_HEADS
        for i in range(B_KV):
            kb_ref[rows, _col(i)] = rope128(_rmsnorm(p[:, _col(off + i)], gkn)).astype(BF16)
        off += B_KV
        for i in range(B_KV):
            vtb_ref[i * VT_EXT:i * VT_EXT + HEAD_DIM, rows] = p[:, _col(off + i)].T.astype(BF16)
            vtb_ref[i * VT_EXT + HEAD_DIM:(i + 1) * VT_EXT, rows] = jnp.ones((VT_EXT - HEAD_DIM, tr), BF16)
        off += B_KV

        c0 = off * LANES
        cq = _rmsnorm(p[:, c0:c0 + C_Q_LORA], gcq_ref[...])
        q = _dot(cq.astype(BF16), wuq_ref[...])
        c0 += C_Q_LORA
        ckv = _rmsnorm(p[:, c0:c0 + C_KV_LORA], gckv_ref[...])
        kv = _dot(ckv.astype(BF16), wukv_ref[...])
        c0 += C_KV_LORA
        ckr = jnp.concatenate([p[:, c0:c0 + C_ROPE], jnp.zeros((tr, LANES - C_ROPE), F32)], axis=1)
        krope = jnp.where(lane == C_ROPE, 1.0, rope64(ckr)).astype(BF16)
        for i in range(C_HEADS):
            qc_ref[rows, _col(2 * i)] = q[:, _col(2 * i)].astype(BF16)
            qc_ref[rows, _col(2 * i + 1)] = rope64(q[:, _col(2 * i + 1)]).astype(BF16)
            kc_ref[rows, _col(2 * i)] = kv[:, _col(i)].astype(BF16)
            kc_ref[rows, _col(2 * i + 1)] = krope
            vtc_ref[_col(i), rows] = kv[:, _col(C_HEADS + i)].T.astype(BF16)


def _qkv_call(l, xa, mods, gpre, win, tabs, gqn, gkn, gcq, gckv, wuq, wukv, w_out, *,
              n_lat_tiles, tiles_per_seq, n_seq):
    t, d = xa.shape
    n_tiles = t // TM
    d_in = win.shape[2]
    d_mix = w_out.shape[1]
    assert n_tiles >= d_mix // LANES
    wo_blk = lambda i: jnp.minimum(i, d_mix // LANES - 1)

    def seg(i):
        return jnp.minimum(i // tiles_per_seq, n_seq)

    def rope_blk(i):
        return jnp.where(i < n_lat_tiles, i % tiles_per_seq, tiles_per_seq)

    layer = lambda i: (l, 0, 0)
    vec = lambda n: pl.BlockSpec((None, 1, n), layer)
    tab_spec = pl.BlockSpec((TM, LANES), lambda i: (rope_blk(i), 0))
    row_major = lambda w: (pl.BlockSpec((TM, w), lambda i: (i, 0)), jax.ShapeDtypeStruct((t, w), BF16))
    col_major = lambda w: (pl.BlockSpec((w, TM), lambda i: (0, i)), jax.ShapeDtypeStruct((w, t), BF16))
    outs = [row_major(A_HEADS * HEAD_DIM), row_major(A_KV * HEAD_DIM), row_major(A_KV * HEAD_DIM),
            row_major(B_HEADS * HEAD_DIM), row_major(B_KV * HEAD_DIM), col_major(B_KV * VT_EXT),
            row_major(C_HEADS * C_QK_PAD), row_major(C_HEADS * C_QK_PAD), col_major(C_HEADS * C_V),
            (pl.BlockSpec((LANES, d), lambda i: (wo_blk(i), 0)), jax.ShapeDtypeStruct((d_mix, d), BF16))]
    return pl.pallas_call(
        _qkv_kernel,
        grid=(n_tiles,),
        in_specs=[
            pl.BlockSpec((TM, d), lambda i: (i, 0)),
            pl.BlockSpec((None, None, MOD_ROWS, d), lambda i: (l, seg(i), 0, 0)),
            vec(d),
            pl.BlockSpec((None, d, d_in), layer, pipeline_mode=pl.Buffered(1)),
            tab_spec, tab_spec, tab_spec, tab_spec,
            vec(HEAD_DIM), vec(HEAD_DIM), vec(C_Q_LORA), vec(C_KV_LORA),
            pl.BlockSpec((None,) + wuq.shape[1:], layer, pipeline_mode=pl.Buffered(1)),
            pl.BlockSpec((None,) + wukv.shape[1:], layer, pipeline_mode=pl.Buffered(1)),
            pl.BlockSpec((None, LANES, d), lambda i: (l, wo_blk(i), 0)),
        ],
        out_specs=[o[0] for o in outs],
        out_shape=[o[1] for o in outs],
        compiler_params=pltpu.CompilerParams(
            dimension_semantics=("arbitrary",),
            vmem_limit_bytes=56 * MIB),
        name="qkv_prep",
    )(xa, mods, gpre, win, *tabs, gqn, gkn, gcq, gckv, wuq, wukv, w_out)


def _attn_window_kernel(sink_ref, q_ref, kl_ref, vl_ref, kc_ref, vc_ref, o_ref, *,
                        layer, heads_per_group, dk, dv, scale, seq):
    g = pl.program_id(1)
    qi = pl.program_id(2)
    c2 = scale * LOG2E
    band = TQ + 2 * WINDOW
    heads = range(heads_per_group)
    sink2 = jnp.concatenate(
        [jnp.full((1, TQ), sink_ref[layer, g * heads_per_group + r] * LOG2E, F32) for r in heads], axis=1)
    for sub in range(SUBTILES_WIN):
        rows = slice(sub * TQ, (sub + 1) * TQ)
        q0 = (qi * SUBTILES_WIN + sub) * TQ
        q = jnp.concatenate([q_ref[rows, r * dk:(r + 1) * dk] for r in heads], axis=0)
        start = pl.multiple_of(jnp.clip(q0 - WINDOW, 0, seq - band), LANES)
        s_c = _dot_nt(kc_ref[...], q)
        s_l = _dot_nt(kl_ref[pl.ds(start, band), :], q)
        kpos = start + lax.broadcasted_iota(jnp.int32, (band, TQ), 0)
        qpos = q0 + lax.broadcasted_iota(jnp.int32, (band, TQ), 1)
        in_window = jnp.abs(kpos - qpos) <= WINDOW
        s_l = jnp.where(jnp.concatenate([in_window] * heads_per_group, axis=1), s_l, NEG)
        m = jnp.maximum(jnp.maximum(jnp.max(s_c, axis=0, keepdims=True), jnp.max(s_l, axis=0, keepdims=True)) * c2,
                        sink2)
        p_c = jnp.exp2(s_c * c2 - m)
        p_l = jnp.exp2(s_l * c2 - m)
        denom = (jnp.sum(p_c, axis=0, keepdims=True) + jnp.sum(p_l, axis=0, keepdims=True)
                 + jnp.exp2(sink2 - m))
        acc = _dot_tn(vc_ref[...], p_c.astype(BF16)) + _dot_tn(vl_ref[pl.ds(start, band), :], p_l.astype(BF16))
        o = acc / denom
        for r in heads:
            o_ref[rows, r * dv:(r + 1) * dv] = o[:, r * TQ:(r + 1) * TQ].T.astype(BF16)


def _attn_window_call(l, q, k, v, sink, *, n_groups, heads_per_group, dk, dv, scale, batch, seq, ctx_len):
    tq = SUBTILES_WIN * TQ
    n_q = seq // tq
    ctx_blk0 = batch * seq // ctx_len
    return pl.pallas_call(
        functools.partial(_attn_window_kernel, layer=l, heads_per_group=heads_per_group, dk=dk, dv=dv,
                          scale=scale, seq=seq),
        grid=(batch, n_groups, n_q),
        in_specs=[
            pl.BlockSpec(memory_space=pltpu.SMEM),
            pl.BlockSpec((tq, heads_per_group * dk), lambda b, g, qi: (b * n_q + qi, g)),
            pl.BlockSpec((seq, dk), lambda b, g, qi: (b, g)),
            pl.BlockSpec((seq, dv), lambda b, g, qi: (b, g)),
            pl.BlockSpec((ctx_len, dk), lambda b, g, qi: (ctx_blk0 + b, g)),
            pl.BlockSpec((ctx_len, dv), lambda b, g, qi: (ctx_blk0 + b, g)),
        ],
        out_specs=pl.BlockSpec((tq, heads_per_group * dv), lambda b, g, qi: (b * n_q + qi, g)),
        out_shape=jax.ShapeDtypeStruct((batch * seq, n_groups * heads_per_group * dv), BF16),
        compiler_params=pltpu.CompilerParams(
            dimension_semantics=("arbitrary", "arbitrary", "arbitrary"),
            vmem_limit_bytes=48 * MIB),
        name="attn_window",
    )(sink, q, k, v, k, v)


def _attn_dense_kernel(*refs, heads_per_group, dk, dv, scale, tq, shift_lane, cast_side_job):
    if cast_side_job:
        q_ref, kl_ref, vtl_ref, kc_ref, vtc_ref, w_ref, o_ref, wb_ref, k2_scr = refs
        wb_ref[...] = w_ref[...].astype(BF16)
    else:
        q_ref, kl_ref, vtl_ref, kc_ref, vtc_ref, o_ref, k2_scr = refs
    c2 = scale * LOG2E
    seq = kl_ref.shape[0]

    @pl.when(pl.program_id(2) == 0)
    def _():
        kf = kl_ref[...].astype(F32)
        kcf = kc_ref[...].astype(F32)
        k2_scr[0] = jnp.maximum(jnp.max(jnp.sum(kf * kf, axis=-1, keepdims=True)),
                                jnp.max(jnp.sum(kcf * kcf, axis=-1, keepdims=True)))

    if heads_per_group > 1:
        q = jnp.concatenate([q_ref[:, r * dk:(r + 1) * dk] for r in range(heads_per_group)], axis=0)
    else:
        q = q_ref[...]
    qf = q.astype(F32)
    q2 = jnp.sum(qf * qf, axis=-1, keepdims=True)
    bound = jnp.sqrt(q2 * k2_scr[0]) * BOUND_SLACK
    bound_is_safe = jnp.max(bound) * c2 <= SAFE_SHIFT_LOG2

    def finish(acc, denom):
        o = acc / denom
        for r in range(heads_per_group):
            o_ref[:, r * dv:(r + 1) * dv] = o[:, r * tq:(r + 1) * tq].T.astype(BF16)

    @pl.when(bound_is_safe)
    def _():
        if shift_lane is None:
            shift = bound.T * c2
            p_c = jnp.exp2(_dot_nt(kc_ref[...], q) * c2 - shift)
            p_l = jnp.exp2(_dot_nt(kl_ref[...], q) * c2 - shift)
        else:
            lane = lax.broadcasted_iota(jnp.int32, q.shape, 1)
            q_shifted = jnp.where(lane == shift_lane, -bound, qf).astype(BF16)
            p_c = jnp.exp2(_dot_nt(kc_ref[...], q_shifted) * c2)
            p_l = jnp.exp2(_dot_nt(kl_ref[...], q_shifted) * c2)
        acc = _dot(vtc_ref[...], p_c.astype(BF16)) + _dot(vtl_ref[...], p_l.astype(BF16))
        if vtl_ref.shape[0] > dv:
            finish(acc[:dv], acc[dv:dv + 1])
        else:
            finish(acc, jnp.sum(p_c, axis=0, keepdims=True) + jnp.sum(p_l, axis=0, keepdims=True))

    @pl.when(jnp.logical_not(bound_is_safe))
    def _():
        chunks = [(lambda: kc_ref[...], lambda: vtc_ref[0:dv, :])]
        for j in range(seq // TK):
            chunks.append((functools.partial(lambda j: kl_ref[j * TK:(j + 1) * TK, :], j),
                           functools.partial(lambda j: vtl_ref[0:dv, j * TK:(j + 1) * TK], j)))
        n = len(chunks)
        s_next = _dot_nt(chunks[0][0](), q)
        m = denom = acc = None
        for j in range(n):
            s = s_next
            if j + 1 < n:
                s_next = _dot_nt(chunks[j + 1][0](), q)
            m_j = jnp.max(s, axis=0, keepdims=True) * c2
            m_new = m_j if j == 0 else jnp.maximum(m, m_j)
            p = jnp.exp2(s * c2 - m_new)
            pv = _dot(chunks[j][1](), p.astype(BF16))
            if j == 0:
                denom = jnp.sum(p, axis=0, keepdims=True)
                acc = pv
            else:
                alpha = jnp.exp2(m - m_new)
                denom = denom * alpha + jnp.sum(p, axis=0, keepdims=True)
                acc = acc * alpha + pv
            m = m_new
        finish(acc, denom)


def _attn_dense_call(q, k, vt, *, n_groups, heads_per_group, dk, dv, scale, batch, seq, ctx_len, tq,
                     shift_lane=None, cast_weight=None):
    n_q = seq // tq
    ctx_blk0 = batch * seq // ctx_len
    vt_rows = vt.shape[0] // n_groups
    in_specs = [
        pl.BlockSpec((tq, heads_per_group * dk), lambda b, g, qi: (b * n_q + qi, g)),
        pl.BlockSpec((seq, dk), lambda b, g, qi: (b, g)),
        pl.BlockSpec((vt_rows, seq), lambda b, g, qi: (g, b)),
        pl.BlockSpec((ctx_len, dk), lambda b, g, qi: (ctx_blk0 + b, g)),
        pl.BlockSpec((vt_rows, ctx_len), lambda b, g, qi: (g, ctx_blk0 + b)),
    ]
    args = [q, k, vt, k, vt]
    out_specs = [pl.BlockSpec((tq, heads_per_group * dv), lambda b, g, qi: (b * n_q + qi, g))]
    out_shape = [jax.ShapeDtypeStruct((batch * seq, n_groups * heads_per_group * dv), BF16)]
    if cast_weight is not None:
        l, w = cast_weight
        n_steps = batch * n_groups * n_q
        rows, cols = w.shape[1] // n_steps, w.shape[2]
        assert rows * n_steps == w.shape[1] and rows % 16 == 0
        step = lambda b, g, qi: (b * n_groups + g) * n_q + qi
        in_specs.append(pl.BlockSpec((None, rows, cols), lambda b, g, qi: (l, step(b, g, qi), 0)))
        args.append(w)
        out_specs.append(pl.BlockSpec((rows, cols), lambda b, g, qi: (step(b, g, qi), 0)))
        out_shape.append(jax.ShapeDtypeStruct(w.shape[1:], BF16))
    outs = pl.pallas_call(
        functools.partial(_attn_dense_kernel, heads_per_group=heads_per_group, dk=dk, dv=dv, scale=scale, tq=tq,
                          shift_lane=shift_lane, cast_side_job=cast_weight is not None),
        grid=(batch, n_groups, n_q),
        in_specs=in_specs,
        out_specs=out_specs,
        out_shape=out_shape,
        scratch_shapes=[pltpu.SMEM((1,), F32)],
        compiler_params=pltpu.CompilerParams(
            dimension_semantics=("arbitrary", "arbitrary", "arbitrary"),
            vmem_limit_bytes=48 * MIB),
        name=f"attn_dense_dk{dk}",
    )(*args)
    return outs if cast_weight is not None else outs[0]


def _attn_ctx_kernel(sink_ref, qa_ref, ka_ref, va_ref, qb_ref, kb_ref, vtb_ref, qc_ref, kc_ref, vtc_ref,
                     oa_ref, ob_ref, oc_ref, *, layer, scale_ab, scale_c):
    def head(q, k, scale, pv, sink2=None):
        c2 = scale * LOG2E
        s = _dot_nt(q, k)
        m = jnp.max(s, axis=-1, keepdims=True) * c2
        if sink2 is not None:
            m = jnp.maximum(m, sink2)
        p = jnp.exp2(s * c2 - m)
        denom = jnp.sum(p, axis=-1, keepdims=True)
        if sink2 is not None:
            denom = denom + jnp.exp2(sink2 - m)
        return (pv(p.astype(BF16)) / denom).astype(BF16)

    for hd in range(A_HEADS):
        g = hd // (A_HEADS // A_KV)
        oa_ref[:, _col(hd)] = head(qa_ref[:, _col(hd)], ka_ref[:, _col(g)], scale_ab,
                                   lambda p: _dot(p, va_ref[:, _col(g)]), sink_ref[layer, hd] * LOG2E)
    for hd in range(B_HEADS):
        g = hd // (B_HEADS // B_KV)
        ob_ref[:, _col(hd)] = head(qb_ref[:, _col(hd)], kb_ref[:, _col(g)], scale_ab,
                                   lambda p: _dot_nt(p, vtb_ref[g * VT_EXT:g * VT_EXT + HEAD_DIM, :]))
    for hd in range(C_HEADS):
        qk = slice(hd * C_QK_PAD, (hd + 1) * C_QK_PAD)
        oc_ref[:, _col(hd)] = head(qc_ref[:, qk], kc_ref[:, qk], scale_c,
                                   lambda p: _dot_nt(p, vtc_ref[_col(hd), :]))


def _attn_ctx_call(l, sink, qa, ka, va, qb, kb, vtb, qc, kc, vtc, *, scale_ab, scale_c, batch, seq, ctx_len):
    blk0 = batch * seq // ctx_len
    rows = lambda a: pl.BlockSpec((ctx_len, a.shape[1]), lambda b: (blk0 + b, 0))
    cols = lambda a: pl.BlockSpec((a.shape[0], ctx_len), lambda b: (0, blk0 + b))
    out = lambda w: (pl.BlockSpec((ctx_len, w), lambda b: (b, 0)),
                     jax.ShapeDtypeStruct((batch * ctx_len, w), BF16))
    outs = [out(A_HEADS * HEAD_DIM), out(B_HEADS * HEAD_DIM), out(C_HEADS * C_V)]
    return pl.pallas_call(
        functools.partial(_attn_ctx_kernel, layer=l, scale_ab=scale_ab, scale_c=scale_c),
        grid=(batch,),
        in_specs=[pl.BlockSpec(memory_space=pltpu.SMEM), rows(qa), rows(ka), rows(va),
                  rows(qb), rows(kb), cols(vtb), rows(qc), rows(kc), cols(vtc)],
        out_specs=[o[0] for o in outs],
        out_shape=[o[1] for o in outs],
        compiler_params=pltpu.CompilerParams(
            dimension_semantics=("arbitrary",),
            vmem_limit_bytes=32 * MIB),
        name="attn_ctx",
    )(sink, qa, ka, va, qb, kb, vtb, qc, kc, vtc)


def _outproj_kernel(*refs, n_lat_tiles, has_ctx):
    lat_refs, refs = refs[:3], refs[3:]
    if has_ctx:
        ctx_refs, refs = refs[:3], refs[3:]
    x_ref, mod_ref, gpost_ref, gffn_ref, wo_ref, xo_ref, h_ref = refs
    m = mod_ref[...]
    for half in range(2):
        rows = slice(half * (TM // 2), (half + 1) * (TM // 2))
        o, w0 = None, 0
        for g, lat_ref in enumerate(lat_refs):
            og = lat_ref[rows, :]
            if has_ctx:
                og = jnp.where(pl.program_id(0) >= n_lat_tiles, ctx_refs[g][rows, :], og)
            width = lat_ref.shape[1]
            part = _dot(og, wo_ref[w0:w0 + width, :])
            o = part if o is None else o + part
            w0 += width
        x_new = x_ref[rows, :] + m[2:3] * _rmsnorm(o, gpost_ref[...])
        xo_ref[rows, :] = x_new
        h_ref[rows, :] = (_rmsnorm(x_new, gffn_ref[...]) * (1.0 + m[4:5]) + m[3:4]).astype(BF16)


def _outproj_call(l, lat_outs, ctx_outs, xa, mods, gpost, gffn, wo, *, n_tiles, n_lat_tiles, tiles_per_seq, n_seq):
    t = n_tiles * TM
    d = xa.shape[1]
    seg = lambda i: jnp.minimum(i // tiles_per_seq, n_seq)
    layer = lambda i: (l, 0, 0)
    has_ctx = ctx_outs is not None
    lat_rows = lambda a: pl.BlockSpec((TM, a.shape[1]), lambda i: (jnp.minimum(i, n_lat_tiles - 1), 0))
    ctx_rows = lambda a: pl.BlockSpec((TM, a.shape[1]), lambda i: (jnp.maximum(i - n_lat_tiles, 0), 0))
    in_specs = [lat_rows(a) for a in lat_outs]
    args = list(lat_outs)
    if has_ctx:
        in_specs += [ctx_rows(a) for a in ctx_outs]
        args += list(ctx_outs)
    in_specs += [
        pl.BlockSpec((TM, d), lambda i: (i, 0)),
        pl.BlockSpec((None, None, MOD_ROWS, d), lambda i: (l, seg(i), 0, 0)),
        pl.BlockSpec((None, 1, d), layer),
        pl.BlockSpec((None, 1, d), layer),
        pl.BlockSpec(wo.shape, lambda i: (0, 0), pipeline_mode=pl.Buffered(1)),
    ]
    args += [xa, mods, gpost, gffn, wo]
    return pl.pallas_call(
        functools.partial(_outproj_kernel, n_lat_tiles=n_lat_tiles, has_ctx=has_ctx),
        grid=(n_tiles,),
        in_specs=in_specs,
        out_specs=[pl.BlockSpec((TM, d), lambda i: (i, 0)), pl.BlockSpec((TM, d), lambda i: (i, 0))],
        out_shape=[jax.ShapeDtypeStruct((t, d), F32), jax.ShapeDtypeStruct((t, d), BF16)],
        compiler_params=pltpu.CompilerParams(
            dimension_semantics=("arbitrary",),
            vmem_limit_bytes=48 * MIB),
        name="outproj",
    )(*args)


def _ffn_up_kernel(h_ref, w1_ref, w3_ref, t_ref, w1_scr, w3_scr):
    @pl.when(pl.program_id(1) == 0)
    def _():
        w1_scr[...] = w1_ref[...].astype(BF16)
        w3_scr[...] = w3_ref[...].astype(BF16)

    h = h_ref[...]
    a = _dot(h, w1_scr[...])
    b = _dot(h, w3_scr[...])
    t_ref[...] = (a * jax.nn.sigmoid(a) * b).astype(BF16)


def _ffn_up_call(l, h, w1, w3):
    t, d = h.shape
    f = w1.shape[2]
    tm = t // N_UP_TILES
    return pl.pallas_call(
        _ffn_up_kernel,
        grid=(f // TF, N_UP_TILES),
        in_specs=[
            pl.BlockSpec((tm, d), lambda j, i: (i, 0)),
            pl.BlockSpec((None, d, TF), lambda j, i: (l, 0, j)),
            pl.BlockSpec((None, d, TF), lambda j, i: (l, 0, j)),
        ],
        out_specs=pl.BlockSpec((tm, TF), lambda j, i: (i, j)),
        out_shape=jax.ShapeDtypeStruct((t, f), BF16),
        scratch_shapes=[pltpu.VMEM((d, TF), BF16), pltpu.VMEM((d, TF), BF16)],
        compiler_params=pltpu.CompilerParams(
            dimension_semantics=("arbitrary", "arbitrary"),
            vmem_limit_bytes=48 * MIB),
        name="ffn_up",
    )(h, w1, w3)


def _ffn_down_kernel(t_ref, x_ref, mod_ref, gpost_ref, w2_ref, o_ref):
    y = _dot(t_ref[...], w2_ref[...])
    m = mod_ref[...]
    o_ref[...] = x_ref[...] + m[5:6] * _rmsnorm(y, gpost_ref[...])


def _ffn_down_call(l, tt, xa, mods, gpost, w2, *, tiles_per_seq, n_seq):
    t, d = xa.shape
    f = w2.shape[0]
    seg = lambda i: jnp.minimum(i // tiles_per_seq, n_seq)
    layer = lambda i: (l, 0, 0)
    return pl.pallas_call(
        _ffn_down_kernel,
        grid=(t // TM,),
        in_specs=[
            pl.BlockSpec((TM, f), lambda i: (i, 0)),
            pl.BlockSpec((TM, d), lambda i: (i, 0)),
            pl.BlockSpec((None, None, MOD_ROWS, d), lambda i: (l, seg(i), 0, 0)),
            pl.BlockSpec((None, 1, d), layer),
            pl.BlockSpec((f, d), lambda i: (0, 0), pipeline_mode=pl.Buffered(1)),
        ],
        out_specs=pl.BlockSpec((TM, d), lambda i: (i, 0)),
        out_shape=jax.ShapeDtypeStruct((t, d), F32),
        compiler_params=pltpu.CompilerParams(
            dimension_semantics=("arbitrary",),
            vmem_limit_bytes=60 * MIB),
        name="ffn_down",
    )(tt, xa, mods, gpost, w2)


def _rope_tables(seq):
    pos = np.arange(seq)
    row = (pos // GRID_W).astype(np.float64)[:, None]
    colp = (pos % GRID_W).astype(np.float64)[:, None]

    def cs(n):
        freqs = np.power(np.float64(ROPE_THETA), -np.arange(n, dtype=np.float64) / n)[None, :]
        return (np.cos(row * freqs), np.sin(row * freqs), np.cos(colp * freqs), np.sin(colp * freqs))

    cr, sr, cc, sc = cs(HEAD_DIM // 4)
    cos128 = np.concatenate([cr, cr, cc, cc], axis=1)
    sin128 = np.concatenate([-sr, sr, -sc, sc], axis=1)
    cr, sr, cc, sc = cs(C_ROPE // 4)
    pad1 = np.ones((seq, LANES - C_ROPE), np.float32)
    pad0 = np.zeros((seq, LANES - C_ROPE), np.float32)
    cos64 = np.concatenate([cr, cr, cc, cc, pad1], axis=1)
    sin64 = np.concatenate([-sr, sr, -sc, sc, pad0], axis=1)
    ident_c = np.ones((TM, LANES), np.float32)
    ident_s = np.zeros((TM, LANES), np.float32)
    return tuple(jnp.asarray(np.concatenate([tab, ident], axis=0).astype(np.float32))
                 for tab, ident in ((cos128, ident_c), (sin128, ident_s), (cos64, ident_c), (sin64, ident_s)))


def kernel(x, c, ctx, c_ctx, w_mod, b_mod, g_mix_pre, g_mix_post, g_ffn_pre, g_ffn_post, w_in, sink_a,
           g_qn_b, g_kn_b, g_cq, w_uq, g_ckv, w_ukv, w_out, w_ffn1, w_ffn3, w_ffn2):
    batch, seq, d = x.shape
    ctx_len = ctx.shape[1]
    depth = w_mod.shape[0]
    n_lat, n_ctx = batch * seq, batch * ctx_len
    assert seq % TM == 0 and n_ctx % TM == 0 and seq % ctx_len == 0 and ctx_len % TQ == 0
    assert seq % TQ_GQA == 0 and seq % TQ_MLA == 0 and seq % TK == 0 and ctx_len % LANES == 0
    assert batch + 1 <= MOD_ROWS and seq % GRID_W == 0 and seq % (SUBTILES_WIN * TQ) == 0
    assert n_lat % (16 * N_UP_TILES) == 0 and (n_lat + n_ctx) % (16 * N_UP_TILES) == 0
    tiles_per_seq = seq // TM
    n_lat_tiles = n_lat // TM

    cv = jnp.zeros((MOD_ROWS, d), F32).at[:batch].set(c).at[batch].set(c_ctx)
    mod_all = _mod_call(cv, w_mod, b_mod)
    mods = mod_all[:, :batch + 1].reshape(depth, batch + 1, 6, d)
    mods = jnp.pad(mods, ((0, 0), (0, 0), (0, MOD_ROWS - 6), (0, 0)))

    tabs = _rope_tables(seq)

    d_in = w_in.shape[2]
    d_in_pad = -(-d_in // LANES) * LANES
    win = jnp.concatenate([w_in.astype(BF16), jnp.zeros((depth, d, d_in_pad - d_in), BF16)], axis=2)
    wuq = w_uq.reshape(depth, C_Q_LORA, C_HEADS, C_NOPE + C_ROPE)
    wuq = jnp.pad(wuq, ((0, 0), (0, 0), (0, 0), (0, C_QK_PAD - C_NOPE - C_ROPE)))
    wuq = wuq.reshape(depth, C_Q_LORA, C_HEADS * C_QK_PAD).astype(BF16)
    wukv = w_ukv.reshape(depth, C_KV_LORA, C_HEADS, 2, C_NOPE).transpose(0, 1, 3, 2, 4)
    wukv = wukv.reshape(depth, C_KV_LORA, 2 * C_HEADS * C_NOPE).astype(BF16)
    vecs = lambda v: v.reshape(depth, 1, -1)
    g_mix_pre, g_mix_post, g_ffn_pre, g_ffn_post = map(vecs, (g_mix_pre, g_mix_post, g_ffn_pre, g_ffn_post))
    g_qn_b, g_kn_b, g_cq, g_ckv = map(vecs, (g_qn_b, g_kn_b, g_cq, g_ckv))

    xa = jnp.concatenate([x.reshape(n_lat, d), ctx.reshape(n_ctx, d)], axis=0)
    sc_ab = 1.0 / math.sqrt(HEAD_DIM)
    sc_c = 1.0 / math.sqrt(C_NOPE + C_ROPE)
    dims = dict(batch=batch, seq=seq, ctx_len=ctx_len)

    for l in range(depth):
        last = l == depth - 1
        qa, ka, va, qb, kb, vtb, qc, kc, vtc, wo = _qkv_call(
            l, xa, mods, g_mix_pre, win, tabs, g_qn_b, g_kn_b, g_cq, g_ckv, wuq, wukv, w_out,
            n_lat_tiles=n_lat_tiles, tiles_per_seq=tiles_per_seq, n_seq=batch)
        oa = _attn_window_call(l, qa, ka, va, sink_a, n_groups=A_KV, heads_per_group=A_HEADS // A_KV,
                               dk=HEAD_DIM, dv=HEAD_DIM, scale=sc_ab, **dims)
        ob = _attn_dense_call(qb, kb, vtb, n_groups=B_KV, heads_per_group=B_HEADS // B_KV, dk=HEAD_DIM,
                              dv=HEAD_DIM, scale=sc_ab, tq=TQ_GQA, **dims)
        oc, w2 = _attn_dense_call(qc, kc, vtc, n_groups=C_HEADS, heads_per_group=1, dk=C_QK_PAD,
                                  dv=C_V, scale=sc_c, tq=TQ_MLA, shift_lane=MLA_SHIFT_LANE,
                                  cast_weight=(l, w_ffn2), **dims)
        ctx_outs = None if last else _attn_ctx_call(l, sink_a, qa, ka, va, qb, kb, vtb, qc, kc, vtc,
                                                    scale_ab=sc_ab, scale_c=sc_c, **dims)
        n_tiles = n_lat_tiles if last else n_lat_tiles + n_ctx // TM
        xa, h = _outproj_call(l, (oa, ob, oc), ctx_outs, xa, mods, g_mix_post, g_ffn_pre, wo, n_tiles=n_tiles,
                              n_lat_tiles=n_lat_tiles, tiles_per_seq=tiles_per_seq, n_seq=batch)
        tt = _ffn_up_call(l, h, w_ffn1, w_ffn3)
        xa = _ffn_down_call(l, tt, xa, mods, g_ffn_post, w2, tiles_per_seq=tiles_per_seq, n_seq=batch)
    return xa.reshape(batch, seq, d)
```

```python
import functools
import math

import jax
import jax.numpy as jnp
import numpy as np
from jax import lax
from jax.experimental import pallas as pl
from jax.experimental.pallas import tpu as pltpu

GRID_W = 64
HEAD_DIM = 128
A_HEADS, A_KV = 6, 2
B_HEADS, B_KV = 6, 2
WINDOW = 128
C_HEADS = 4
C_Q_LORA, C_KV_LORA = 512, 256
C_NOPE, C_ROPE, C_V = 128, 64, 128
C_QK_PAD = 256
MLA_SHIFT_LANE = C_NOPE + C_ROPE
VT_EXT = HEAD_DIM + 16
ROPE_THETA = 10000.0
EPS = 1e-6
NEG = -1e30
LOG2E = math.log2(math.e)
SAFE_SHIFT_LOG2 = 50.0
BOUND_SLACK = 1.001

LANES = 128
MOD_ROWS = 8
TM = 512
TQ = 256
SUBTILES_WIN = 2
TQ_GQA = 256
TQ_MLA = 1024
TK = 512
TF = 512
N_UP_TILES = 8
TN_MOD = 1024
MIB = 1024 * 1024

F32 = jnp.float32
BF16 = jnp.bfloat16


def _rmsnorm(x, g):
    return x * lax.rsqrt(jnp.mean(x * x, axis=-1, keepdims=True) + EPS) * g


def _dot(a, b):
    return jnp.dot(a, b, preferred_element_type=F32)


def _dot_nt(a, b):
    return lax.dot_general(a, b, (((1,), (1,)), ((), ())), preferred_element_type=F32)


def _dot_tn(a, b):
    return lax.dot_general(a, b, (((0,), (0,)), ((), ())), preferred_element_type=F32)


def _col(i):
    return slice(i * LANES, (i + 1) * LANES)


def _mod_kernel(c_ref, w_ref, b_ref, o_ref):
    cv = c_ref[...]
    s = cv * jax.nn.sigmoid(cv)
    o_ref[...] = _dot(s.astype(BF16), w_ref[...].astype(BF16)) + b_ref[...]


def _mod_call(cv, w_mod, b_mod):
    depth, d, n = w_mod.shape
    return pl.pallas_call(
        _mod_kernel,
        grid=(depth, n // TN_MOD),
        in_specs=[
            pl.BlockSpec((MOD_ROWS, d), lambda l, j: (0, 0)),
            pl.BlockSpec((None, d, TN_MOD), lambda l, j: (l, 0, j)),
            pl.BlockSpec((None, 1, TN_MOD), lambda l, j: (l, 0, j)),
        ],
        out_specs=pl.BlockSpec((None, MOD_ROWS, TN_MOD), lambda l, j: (l, 0, j)),
        out_shape=jax.ShapeDtypeStruct((depth, MOD_ROWS, n), F32),
        compiler_params=pltpu.CompilerParams(
            dimension_semantics=("arbitrary", "arbitrary"),
            vmem_limit_bytes=40 * MIB),
        name="adaln_mod",
    )(cv, w_mod, b_mod.reshape(depth, 1, n))


def _qkv_kernel(*refs, n_lat_tiles, split_x):
    x_refs, refs = (refs[:2], refs[2:]) if split_x else (refs[:1], refs[1:])
    (mod_ref, gpre_ref, win_ref, wkr_ref, cos_ref, sin_ref, cos64_ref, sin64_ref,
     gqn_ref, gkn_ref, gcq_ref, gckv_ref, wuq_ref, wukv_ref, wo_ref,
     qa_ref, ka_ref, va_ref, qb_ref, kb_ref, vtb_ref, qc_ref, kc_ref, vtc_ref, wob_ref) = refs
    wob_ref[...] = wo_ref[...].astype(BF16)
    m = mod_ref[...]
    tr = TM // 2
    lane = lax.broadcasted_iota(jnp.int32, (tr, LANES), 1)
    low32 = (lane & 32) == 0
    low16 = (lane & 16) == 0

    for half in range(2):
        rows = slice(half * tr, (half + 1) * tr)
        x = x_refs[0][rows, :]
        if split_x:
            x = jnp.where(pl.program_id(0) >= n_lat_tiles, x_refs[1][rows, :], x)
        h = (_rmsnorm(x, gpre_ref[...]) * (1.0 + m[1:2]) + m[0:1]).astype(BF16)
        p = _dot(h, win_ref[...])
        ckr = _dot(h, wkr_ref[...])
        cos, sin = cos_ref[rows, :], sin_ref[rows, :]
        cos64, sin64 = cos64_ref[rows, :], sin64_ref[rows, :]

        def rope128(t):
            partner = jnp.where(low32, pltpu.roll(t, LANES - 32, 1), pltpu.roll(t, 32, 1))
            return t * cos + partner * sin

        def rope64(t):
            partner = jnp.where(low16, pltpu.roll(t, LANES - 16, 1), pltpu.roll(t, 16, 1))
            return t * cos64 + partner * sin64

        off = 0
        for i in range(A_HEADS):
            qa_ref[rows, _col(i)] = rope128(p[:, _col(off + i)]).astype(BF16)
        off += A_HEADS
        for i in range(A_KV):
            ka_ref[rows, _col(i)] = rope128(p[:, _col(off + i)]).astype(BF16)
        off += A_KV
        for i in range(A_KV):
            va_ref[rows, _col(i)] = p[:, _col(off + i)].astype(BF16)
        off += A_KV

        gqn, gkn = gqn_ref[...], gkn_ref[...]
        for i in range(B_HEADS):
            qb_ref[rows, _col(i)] = rope128(_rmsnorm(p[:, _col(off + i)], gqn)).astype(BF16)
        off += B_HEADS
        for i in range(B_KV):
            kb_ref[rows, _col(i)] = rope128(_rmsnorm(p[:, _col(off + i)], gkn)).astype(BF16)
        off += B_KV
        for i in range(B_KV):
            vtb_ref[i * VT_EXT:i * VT_EXT + HEAD_DIM, rows] = p[:, _col(off + i)].T.astype(BF16)
            vtb_ref[i * VT_EXT + HEAD_DIM:(i + 1) * VT_EXT, rows] = jnp.ones((VT_EXT - HEAD_DIM, tr), BF16)
        off += B_KV

        c0 = off * LANES
        cq = _rmsnorm(p[:, c0:c0 + C_Q_LORA], gcq_ref[...])
        q = _dot(cq.astype(BF16), wuq_ref[...])
        c0 += C_Q_LORA
        ckv = _rmsnorm(p[:, c0:c0 + C_KV_LORA], gckv_ref[...])
        kv = _dot(ckv.astype(BF16), wukv_ref[...])
        ckr = jnp.concatenate([ckr, jnp.zeros((tr, LANES - C_ROPE), F32)], axis=1)
        krope = jnp.where(lane == C_ROPE, 1.0, rope64(ckr)).astype(BF16)
        for i in range(C_HEADS):
            qc_ref[rows, _col(2 * i)] = q[:, _col(2 * i)].astype(BF16)
            qc_ref[rows, _col(2 * i + 1)] = rope64(q[:, _col(2 * i + 1)]).astype(BF16)
            kc_ref[rows, _col(2 * i)] = kv[:, _col(i)].astype(BF16)
            kc_ref[rows, _col(2 * i + 1)] = krope
            vtc_ref[_col(i), rows] = kv[:, _col(C_HEADS + i)].T.astype(BF16)


def _qkv_call(l, xs, mods, gpre, win, wkr, tabs, gqn, gkn, gcq, gckv, wuq, wukv, w_out, *,
              n_lat_tiles, tiles_per_seq, n_seq):
    split_x = len(xs) == 2
    t, d = sum(a.shape[0] for a in xs), xs[0].shape[1]
    n_tiles = t // TM
    d_in = win.shape[2]
    d_mix = w_out.shape[1]
    if split_x:
        x_specs = [pl.BlockSpec((TM, d), lambda i: (jnp.minimum(i, n_lat_tiles - 1), 0)),
                   pl.BlockSpec((TM, d), lambda i: (jnp.maximum(i - n_lat_tiles, 0), 0))]
    else:
        x_specs = [pl.BlockSpec((TM, d), lambda i: (i, 0))]
    assert n_tiles >= d_mix // LANES
    wo_blk = lambda i: jnp.minimum(i, d_mix // LANES - 1)

    def seg(i):
        return jnp.minimum(i // tiles_per_seq, n_seq)

    def rope_blk(i):
        return jnp.where(i < n_lat_tiles, i % tiles_per_seq, tiles_per_seq)

    layer = lambda i: (l, 0, 0)
    vec = lambda n: pl.BlockSpec((None, 1, n), layer)
    tab_spec = pl.BlockSpec((TM, LANES), lambda i: (rope_blk(i), 0))
    row_major = lambda w: (pl.BlockSpec((TM, w), lambda i: (i, 0)), jax.ShapeDtypeStruct((t, w), BF16))
    col_major = lambda w: (pl.BlockSpec((w, TM), lambda i: (0, i)), jax.ShapeDtypeStruct((w, t), BF16))
    outs = [row_major(A_HEADS * HEAD_DIM), row_major(A_KV * HEAD_DIM), row_major(A_KV * HEAD_DIM),
            row_major(B_HEADS * HEAD_DIM), row_major(B_KV * HEAD_DIM), col_major(B_KV * VT_EXT),
            row_major(C_HEADS * C_QK_PAD), row_major(C_HEADS * C_QK_PAD), col_major(C_HEADS * C_V),
            (pl.BlockSpec((LANES, d), lambda i: (wo_blk(i), 0)), jax.ShapeDtypeStruct((d_mix, d), BF16))]
    return pl.pallas_call(
        functools.partial(_qkv_kernel, n_lat_tiles=n_lat_tiles, split_x=split_x),
        grid=(n_tiles,),
        in_specs=x_specs + [
            pl.BlockSpec((None, None, MOD_ROWS, d), lambda i: (l, seg(i), 0, 0)),
            vec(d),
            pl.BlockSpec((None, d, d_in), layer, pipeline_mode=pl.Buffered(1)),
            pl.BlockSpec((None, d, C_ROPE), layer, pipeline_mode=pl.Buffered(1)),
            tab_spec, tab_spec, tab_spec, tab_spec,
            vec(HEAD_DIM), vec(HEAD_DIM), vec(C_Q_LORA), vec(C_KV_LORA),
            pl.BlockSpec((None,) + wuq.shape[1:], layer, pipeline_mode=pl.Buffered(1)),
            pl.BlockSpec((None,) + wukv.shape[1:], layer, pipeline_mode=pl.Buffered(1)),
            pl.BlockSpec((None, LANES, d), lambda i: (l, wo_blk(i), 0)),
        ],
        out_specs=[o[0] for o in outs],
        out_shape=[o[1] for o in outs],
        compiler_params=pltpu.CompilerParams(
            dimension_semantics=("arbitrary",),
            vmem_limit_bytes=56 * MIB),
        name="qkv_prep",
    )(*xs, mods, gpre, win, wkr, *tabs, gqn, gkn, gcq, gckv, wuq, wukv, w_out)


def _attn_window_kernel(sink_ref, q_ref, kl_ref, vl_ref, kc_ref, vc_ref, o_ref, *,
                        layer, heads_per_group, dk, dv, scale, seq):
    g = pl.program_id(1)
    qi = pl.program_id(2)
    c2 = scale * LOG2E
    band = TQ + 2 * WINDOW
    heads = range(heads_per_group)
    sink2 = jnp.concatenate(
        [jnp.full((1, TQ), sink_ref[layer, g * heads_per_group + r] * LOG2E, F32) for r in heads], axis=1)
    for sub in range(SUBTILES_WIN):
        rows = slice(sub * TQ, (sub + 1) * TQ)
        q0 = (qi * SUBTILES_WIN + sub) * TQ
        q = jnp.concatenate([q_ref[rows, r * dk:(r + 1) * dk] for r in heads], axis=0)
        start = pl.multiple_of(jnp.clip(q0 - WINDOW, 0, seq - band), LANES)
        s_c = _dot_nt(kc_ref[...], q)
        s_l = _dot_nt(kl_ref[pl.ds(start, band), :], q)
        kpos = start + lax.broadcasted_iota(jnp.int32, (band, TQ), 0)
        qpos = q0 + lax.broadcasted_iota(jnp.int32, (band, TQ), 1)
        in_window = jnp.abs(kpos - qpos) <= WINDOW
        s_l = jnp.where(jnp.concatenate([in_window] * heads_per_group, axis=1), s_l, NEG)
        m = jnp.maximum(jnp.maximum(jnp.max(s_c, axis=0, keepdims=True), jnp.max(s_l, axis=0, keepdims=True)) * c2,
                        sink2)
        p_c = jnp.exp2(s_c * c2 - m)
        p_l = jnp.exp2(s_l * c2 - m)
        denom = (jnp.sum(p_c, axis=0, keepdims=True) + jnp.sum(p_l, axis=0, keepdims=True)
                 + jnp.exp2(sink2 - m))
        acc = _dot_tn(vc_ref[...], p_c.astype(BF16)) + _dot_tn(vl_ref[pl.ds(start, band), :], p_l.astype(BF16))
        o = acc / denom
        for r in heads:
            o_ref[rows, r * dv:(r + 1) * dv] = o[:, r * TQ:(r + 1) * TQ].T.astype(BF16)


def _attn_window_call(l, q, k, v, sink, *, n_groups, heads_per_group, dk, dv, scale, batch, seq, ctx_len):
    tq = SUBTILES_WIN * TQ
    n_q = seq // tq
    ctx_blk0 = batch * seq // ctx_len
    return pl.pallas_call(
        functools.partial(_attn_window_kernel, layer=l, heads_per_group=heads_per_group, dk=dk, dv=dv,
                          scale=scale, seq=seq),
        grid=(batch, n_groups, n_q),
        in_specs=[
            pl.BlockSpec(memory_space=pltpu.SMEM),
            pl.BlockSpec((tq, heads_per_group * dk), lambda b, g, qi: (b * n_q + qi, g)),
            pl.BlockSpec((seq, dk), lambda b, g, qi: (b, g)),
            pl.BlockSpec((seq, dv), lambda b, g, qi: (b, g)),
            pl.BlockSpec((ctx_len, dk), lambda b, g, qi: (ctx_blk0 + b, g)),
            pl.BlockSpec((ctx_len, dv), lambda b, g, qi: (ctx_blk0 + b, g)),
        ],
        out_specs=pl.BlockSpec((tq, heads_per_group * dv), lambda b, g, qi: (b * n_q + qi, g)),
        out_shape=jax.ShapeDtypeStruct((batch * seq, n_groups * heads_per_group * dv), BF16),
        compiler_params=pltpu.CompilerParams(
            dimension_semantics=("arbitrary", "arbitrary", "arbitrary"),
            vmem_limit_bytes=48 * MIB),
        name="attn_window",
    )(sink, q, k, v, k, v)


def _attn_dense_kernel(*refs, heads_per_group, dk, dv, scale, tq, shift_lane, cast_side_job):
    if cast_side_job:
        q_ref, kl_ref, vtl_ref, kc_ref, vtc_ref, w_ref, o_ref, wb_ref, k2_scr = refs
        wb_ref[...] = w_ref[...].astype(BF16)
    else:
        q_ref, kl_ref, vtl_ref, kc_ref, vtc_ref, o_ref, k2_scr = refs
    c2 = scale * LOG2E
    seq = kl_ref.shape[0]

    @pl.when(pl.program_id(2) == 0)
    def _():
        kf = kl_ref[...].astype(F32)
        kcf = kc_ref[...].astype(F32)
        k2_scr[0] = jnp.maximum(jnp.max(jnp.sum(kf * kf, axis=-1, keepdims=True)),
                                jnp.max(jnp.sum(kcf * kcf, axis=-1, keepdims=True)))

    if heads_per_group > 1:
        q = jnp.concatenate([q_ref[:, r * dk:(r + 1) * dk] for r in range(heads_per_group)], axis=0)
    else:
        q = q_ref[...]
    qf = q.astype(F32)
    q2 = jnp.sum(qf * qf, axis=-1, keepdims=True)
    bound = jnp.sqrt(q2 * k2_scr[0]) * BOUND_SLACK
    bound_is_safe = jnp.max(bound) * c2 <= SAFE_SHIFT_LOG2

    def finish(acc, denom):
        o = acc / denom
        for r in range(heads_per_group):
            o_ref[:, r * dv:(r + 1) * dv] = o[:, r * tq:(r + 1) * tq].T.astype(BF16)

    @pl.when(bound_is_safe)
    def _():
        if shift_lane is None:
            shift = bound.T * c2
            p_c = jnp.exp2(_dot_nt(kc_ref[...], q) * c2 - shift)
            p_l = jnp.exp2(_dot_nt(kl_ref[...], q) * c2 - shift)
        else:
            lane = lax.broadcasted_iota(jnp.int32, q.shape, 1)
            q_shifted = jnp.where(lane == shift_lane, -bound, qf).astype(BF16)
            p_c = jnp.exp2(_dot_nt(kc_ref[...], q_shifted) * c2)
            p_l = jnp.exp2(_dot_nt(kl_ref[...], q_shifted) * c2)
        acc = _dot(vtc_ref[...], p_c.astype(BF16)) + _dot(vtl_ref[...], p_l.astype(BF16))
        if vtl_ref.shape[0] > dv:
            finish(acc[:dv], acc[dv:dv + 1])
        else:
            finish(acc, jnp.sum(p_c, axis=0, keepdims=True) + jnp.sum(p_l, axis=0, keepdims=True))

    @pl.when(jnp.logical_not(bound_is_safe))
    def _():
        chunks = [(lambda: kc_ref[...], lambda: vtc_ref[0:dv, :])]
        for j in range(seq // TK):
            chunks.append((functools.partial(lambda j: kl_ref[j * TK:(j + 1) * TK, :], j),
                           functools.partial(lambda j: vtl_ref[0:dv, j * TK:(j + 1) * TK], j)))
        n = len(chunks)
        s_next = _dot_nt(chunks[0][0](), q)
        m = denom = acc = None
        for j in range(n):
            s = s_next
            if j + 1 < n:
                s_next = _dot_nt(chunks[j + 1][0](), q)
            m_j = jnp.max(s, axis=0, keepdims=True) * c2
            m_new = m_j if j == 0 else jnp.maximum(m, m_j)
            p = jnp.exp2(s * c2 - m_new)
            pv = _dot(chunks[j][1](), p.astype(BF16))
            if j == 0:
                denom = jnp.sum(p, axis=0, keepdims=True)
                acc = pv
            else:
                alpha = jnp.exp2(m - m_new)
                denom = denom * alpha + jnp.sum(p, axis=0, keepdims=True)
                acc = acc * alpha + pv
            m = m_new
        finish(acc, denom)


def _attn_dense_call(q, k, vt, *, n_groups, heads_per_group, dk, dv, scale, batch, seq, ctx_len, tq,
                     shift_lane=None, cast_weight=None):
    n_q = seq // tq
    ctx_blk0 = batch * seq // ctx_len
    vt_rows = vt.shape[0] // n_groups
    in_specs = [
        pl.BlockSpec((tq, heads_per_group * dk), lambda b, g, qi: (b * n_q + qi, g)),
        pl.BlockSpec((seq, dk), lambda b, g, qi: (b, g)),
        pl.BlockSpec((vt_rows, seq), lambda b, g, qi: (g, b)),
        pl.BlockSpec((ctx_len, dk), lambda b, g, qi: (ctx_blk0 + b, g)),
        pl.BlockSpec((vt_rows, ctx_len), lambda b, g, qi: (g, ctx_blk0 + b)),
    ]
    args = [q, k, vt, k, vt]
    out_specs = [pl.BlockSpec((tq, heads_per_group * dv), lambda b, g, qi: (b * n_q + qi, g))]
    out_shape = [jax.ShapeDtypeStruct((batch * seq, n_groups * heads_per_group * dv), BF16)]
    if cast_weight is not None:
        l, w = cast_weight
        n_steps = batch * n_groups * n_q
        rows, cols = w.shape[1] // n_steps, w.shape[2]
        assert rows * n_steps == w.shape[1] and rows % 16 == 0
        step = lambda b, g, qi: (b * n_groups + g) * n_q + qi
        in_specs.append(pl.BlockSpec((None, rows, cols), lambda b, g, qi: (l, step(b, g, qi), 0)))
        args.append(w)
        out_specs.append(pl.BlockSpec((rows, cols), lambda b, g, qi: (step(b, g, qi), 0)))
        out_shape.append(jax.ShapeDtypeStruct(w.shape[1:], BF16))
    outs = pl.pallas_call(
        functools.partial(_attn_dense_kernel, heads_per_group=heads_per_group, dk=dk, dv=dv, scale=scale, tq=tq,
                          shift_lane=shift_lane, cast_side_job=cast_weight is not None),
        grid=(batch, n_groups, n_q),
        in_specs=in_specs,
        out_specs=out_specs,
        out_shape=out_shape,
        scratch_shapes=[pltpu.SMEM((1,), F32)],
        compiler_params=pltpu.CompilerParams(
            dimension_semantics=("arbitrary", "arbitrary", "arbitrary"),
            vmem_limit_bytes=48 * MIB),
        name=f"attn_dense_dk{dk}",
    )(*args)
    return outs if cast_weight is not None else outs[0]


def _attn_ctx_kernel(sink_ref, qa_ref, ka_ref, va_ref, qb_ref, kb_ref, vtb_ref, qc_ref, kc_ref, vtc_ref,
                     oa_ref, ob_ref, oc_ref, *, layer, scale_ab, scale_c):
    def head(q, k, scale, pv, sink2=None):
        c2 = scale * LOG2E
        s = _dot_nt(q, k)
        m = jnp.max(s, axis=-1, keepdims=True) * c2
        if sink2 is not None:
            m = jnp.maximum(m, sink2)
        p = jnp.exp2(s * c2 - m)
        denom = jnp.sum(p, axis=-1, keepdims=True)
        if sink2 is not None:
            denom = denom + jnp.exp2(sink2 - m)
        return (pv(p.astype(BF16)) / denom).astype(BF16)

    for hd in range(A_HEADS):
        g = hd // (A_HEADS // A_KV)
        oa_ref[:, _col(hd)] = head(qa_ref[:, _col(hd)], ka_ref[:, _col(g)], scale_ab,
                                   lambda p: _dot(p, va_ref[:, _col(g)]), sink_ref[layer, hd] * LOG2E)
    for hd in range(B_HEADS):
        g = hd // (B_HEADS // B_KV)
        ob_ref[:, _col(hd)] = head(qb_ref[:, _col(hd)], kb_ref[:, _col(g)], scale_ab,
                                   lambda p: _dot_nt(p, vtb_ref[g * VT_EXT:g * VT_EXT + HEAD_DIM, :]))
    for hd in range(C_HEADS):
        qk = slice(hd * C_QK_PAD, (hd + 1) * C_QK_PAD)
        oc_ref[:, _col(hd)] = head(qc_ref[:, qk], kc_ref[:, qk], scale_c,
                                   lambda p: _dot_nt(p, vtc_ref[_col(hd), :]))


def _attn_ctx_call(l, sink, qa, ka, va, qb, kb, vtb, qc, kc, vtc, *, scale_ab, scale_c, batch, seq, ctx_len):
    blk0 = batch * seq // ctx_len
    rows = lambda a: pl.BlockSpec((ctx_len, a.shape[1]), lambda b: (blk0 + b, 0))
    cols = lambda a: pl.BlockSpec((a.shape[0], ctx_len), lambda b: (0, blk0 + b))
    out = lambda w: (pl.BlockSpec((ctx_len, w), lambda b: (b, 0)),
                     jax.ShapeDtypeStruct((batch * ctx_len, w), BF16))
    outs = [out(A_HEADS * HEAD_DIM), out(B_HEADS * HEAD_DIM), out(C_HEADS * C_V)]
    return pl.pallas_call(
        functools.partial(_attn_ctx_kernel, layer=l, scale_ab=scale_ab, scale_c=scale_c),
        grid=(batch,),
        in_specs=[pl.BlockSpec(memory_space=pltpu.SMEM), rows(qa), rows(ka), rows(va),
                  rows(qb), rows(kb), cols(vtb), rows(qc), rows(kc), cols(vtc)],
        out_specs=[o[0] for o in outs],
        out_shape=[o[1] for o in outs],
        compiler_params=pltpu.CompilerParams(
            dimension_semantics=("arbitrary",),
            vmem_limit_bytes=32 * MIB),
        name="attn_ctx",
    )(sink, qa, ka, va, qb, kb, vtb, qc, kc, vtc)


def _outproj_kernel(*refs, n_lat_tiles, has_ctx, split_x):
    lat_refs, refs = refs[:3], refs[3:]
    if has_ctx:
        ctx_refs, refs = refs[:3], refs[3:]
    x_refs, refs = (refs[:2], refs[2:]) if split_x else (refs[:1], refs[1:])
    mod_ref, gpost_ref, gffn_ref, wo_ref, xo_ref, h_ref = refs
    is_ctx_tile = pl.program_id(0) >= n_lat_tiles
    m = mod_ref[...]
    for half in range(2):
        rows = slice(half * (TM // 2), (half + 1) * (TM // 2))
        o, w0 = None, 0
        for g, lat_ref in enumerate(lat_refs):
            og = lat_ref[rows, :]
            if has_ctx:
                og = jnp.where(is_ctx_tile, ctx_refs[g][rows, :], og)
            width = lat_ref.shape[1]
            part = _dot(og, wo_ref[w0:w0 + width, :])
            o = part if o is None else o + part
            w0 += width
        x = x_refs[0][rows, :]
        if split_x:
            x = jnp.where(is_ctx_tile, x_refs[1][rows, :], x)
        x_new = x + m[2:3] * _rmsnorm(o, gpost_ref[...])
        xo_ref[rows, :] = x_new
        h_ref[rows, :] = (_rmsnorm(x_new, gffn_ref[...]) * (1.0 + m[4:5]) + m[3:4]).astype(BF16)


def _outproj_call(l, lat_outs, ctx_outs, xs, mods, gpost, gffn, wo, *, n_tiles, n_lat_tiles, tiles_per_seq, n_seq):
    t = n_tiles * TM
    d = xs[0].shape[1]
    seg = lambda i: jnp.minimum(i // tiles_per_seq, n_seq)
    layer = lambda i: (l, 0, 0)
    has_ctx = ctx_outs is not None
    split_x = len(xs) == 2
    lat_rows = lambda a: pl.BlockSpec((TM, a.shape[1]), lambda i: (jnp.minimum(i, n_lat_tiles - 1), 0))
    ctx_rows = lambda a: pl.BlockSpec((TM, a.shape[1]), lambda i: (jnp.maximum(i - n_lat_tiles, 0), 0))
    in_specs = [lat_rows(a) for a in lat_outs]
    args = list(lat_outs)
    if has_ctx:
        in_specs += [ctx_rows(a) for a in ctx_outs]
        args += list(ctx_outs)
    in_specs += [lat_rows(xs[0]), ctx_rows(xs[1])] if split_x else [pl.BlockSpec((TM, d), lambda i: (i, 0))]
    args += list(xs)
    in_specs += [
        pl.BlockSpec((None, None, MOD_ROWS, d), lambda i: (l, seg(i), 0, 0)),
        pl.BlockSpec((None, 1, d), layer),
        pl.BlockSpec((None, 1, d), layer),
        pl.BlockSpec(wo.shape, lambda i: (0, 0), pipeline_mode=pl.Buffered(1)),
    ]
    args += [mods, gpost, gffn, wo]
    return pl.pallas_call(
        functools.partial(_outproj_kernel, n_lat_tiles=n_lat_tiles, has_ctx=has_ctx, split_x=split_x),
        grid=(n_tiles,),
        in_specs=in_specs,
        out_specs=[pl.BlockSpec((TM, d), lambda i: (i, 0)), pl.BlockSpec((TM, d), lambda i: (i, 0))],
        out_shape=[jax.ShapeDtypeStruct((t, d), F32), jax.ShapeDtypeStruct((t, d), BF16)],
        compiler_params=pltpu.CompilerParams(
            dimension_semantics=("arbitrary",),
            vmem_limit_bytes=48 * MIB),
        name="outproj",
    )(*args)


def _ffn_up_kernel(h_ref, w1_ref, w3_ref, t_ref, w1_scr, w3_scr):
    @pl.when(pl.program_id(1) == 0)
    def _():
        w1_scr[...] = w1_ref[...].astype(BF16)
        w3_scr[...] = w3_ref[...].astype(BF16)

    h = h_ref[...]
    a = _dot(h, w1_scr[...])
    b = _dot(h, w3_scr[...])
    t_ref[...] = (a * jax.nn.sigmoid(a) * b).astype(BF16)


def _ffn_up_call(l, h, w1, w3):
    t, d = h.shape
    f = w1.shape[2]
    tm = t // N_UP_TILES
    return pl.pallas_call(
        _ffn_up_kernel,
        grid=(f // TF, N_UP_TILES),
        in_specs=[
            pl.BlockSpec((tm, d), lambda j, i: (i, 0)),
            pl.BlockSpec((None, d, TF), lambda j, i: (l, 0, j)),
            pl.BlockSpec((None, d, TF), lambda j, i: (l, 0, j)),
        ],
        out_specs=pl.BlockSpec((tm, TF), lambda j, i: (i, j)),
        out_shape=jax.ShapeDtypeStruct((t, f), BF16),
        scratch_shapes=[pltpu.VMEM((d, TF), BF16), pltpu.VMEM((d, TF), BF16)],
        compiler_params=pltpu.CompilerParams(
            dimension_semantics=("arbitrary", "arbitrary"),
            vmem_limit_bytes=48 * MIB),
        name="ffn_up",
    )(h, w1, w3)


def _ffn_down_kernel(t_ref, x_ref, mod_ref, gpost_ref, w2_ref, o_ref):
    y = _dot(t_ref[...], w2_ref[...])
    m = mod_ref[...]
    o_ref[...] = x_ref[...] + m[5:6] * _rmsnorm(y, gpost_ref[...])


def _ffn_down_call(l, tt, xa, mods, gpost, w2, *, tiles_per_seq, n_seq):
    t, d = xa.shape
    f = w2.shape[0]
    seg = lambda i: jnp.minimum(i // tiles_per_seq, n_seq)
    layer = lambda i: (l, 0, 0)
    return pl.pallas_call(
        _ffn_down_kernel,
        grid=(t // TM,),
        in_specs=[
            pl.BlockSpec((TM, f), lambda i: (i, 0)),
            pl.BlockSpec((TM, d), lambda i: (i, 0)),
            pl.BlockSpec((None, None, MOD_ROWS, d), lambda i: (l, seg(i), 0, 0)),
            pl.BlockSpec((None, 1, d), layer),
            pl.BlockSpec((f, d), lambda i: (0, 0), pipeline_mode=pl.Buffered(1)),
        ],
        out_specs=pl.BlockSpec((TM, d), lambda i: (i, 0)),
        out_shape=jax.ShapeDtypeStruct((t, d), F32),
        compiler_params=pltpu.CompilerParams(
            dimension_semantics=("arbitrary",),
            vmem_limit_bytes=60 * MIB),
        name="ffn_down",
    )(tt, xa, mods, gpost, w2)


def _rope_tables(seq):
    pos = np.arange(seq)
    row = (pos // GRID_W).astype(np.float64)[:, None]
    colp = (pos % GRID_W).astype(np.float64)[:, None]

    def cs(n):
        freqs = np.power(np.float64(ROPE_THETA), -np.arange(n, dtype=np.float64) / n)[None, :]
        return (np.cos(row * freqs), np.sin(row * freqs), np.cos(colp * freqs), np.sin(colp * freqs))

    cr, sr, cc, sc = cs(HEAD_DIM // 4)
    cos128 = np.concatenate([cr, cr, cc, cc], axis=1)
    sin128 = np.concatenate([-sr, sr, -sc, sc], axis=1)
    cr, sr, cc, sc = cs(C_ROPE // 4)
    pad1 = np.ones((seq, LANES - C_ROPE), np.float32)
    pad0 = np.zeros((seq, LANES - C_ROPE), np.float32)
    cos64 = np.concatenate([cr, cr, cc, cc, pad1], axis=1)
    sin64 = np.concatenate([-sr, sr, -sc, sc, pad0], axis=1)
    ident_c = np.ones((TM, LANES), np.float32)
    ident_s = np.zeros((TM, LANES), np.float32)
    return tuple(jnp.asarray(np.concatenate([tab, ident], axis=0).astype(np.float32))
                 for tab, ident in ((cos128, ident_c), (sin128, ident_s), (cos64, ident_c), (sin64, ident_s)))


def kernel(x, c, ctx, c_ctx, w_mod, b_mod, g_mix_pre, g_mix_post, g_ffn_pre, g_ffn_post, w_in, sink_a,
           g_qn_b, g_kn_b, g_cq, w_uq, g_ckv, w_ukv, w_out, w_ffn1, w_ffn3, w_ffn2):
    batch, seq, d = x.shape
    ctx_len = ctx.shape[1]
    depth = w_mod.shape[0]
    n_lat, n_ctx = batch * seq, batch * ctx_len
    assert seq % TM == 0 and n_ctx % TM == 0 and seq % ctx_len == 0 and ctx_len % TQ == 0
    assert seq % TQ_GQA == 0 and seq % TQ_MLA == 0 and seq % TK == 0 and ctx_len % LANES == 0
    assert batch + 1 <= MOD_ROWS and seq % GRID_W == 0 and seq % (SUBTILES_WIN * TQ) == 0
    assert n_lat % (16 * N_UP_TILES) == 0 and (n_lat + n_ctx) % (16 * N_UP_TILES) == 0
    tiles_per_seq = seq // TM
    n_lat_tiles = n_lat // TM

    cv = jnp.zeros((MOD_ROWS, d), F32).at[:batch].set(c).at[batch].set(c_ctx)
    mod_all = _mod_call(cv, w_mod, b_mod)
    mods = mod_all[:, :batch + 1].reshape(depth, batch + 1, 6, d)
    mods = jnp.pad(mods, ((0, 0), (0, 0), (0, MOD_ROWS - 6), (0, 0)))

    tabs = _rope_tables(seq)

    d_main = w_in.shape[2] - C_ROPE
    assert d_main % LANES == 0
    win, wkr = w_in[:, :, :d_main].astype(BF16), w_in[:, :, d_main:].astype(BF16)
    wuq = w_uq.reshape(depth, C_Q_LORA, C_HEADS, C_NOPE + C_ROPE)
    wuq = jnp.pad(wuq, ((0, 0), (0, 0), (0, 0), (0, C_QK_PAD - C_NOPE - C_ROPE)))
    wuq = wuq.reshape(depth, C_Q_LORA, C_HEADS * C_QK_PAD).astype(BF16)
    wukv = w_ukv.reshape(depth, C_KV_LORA, C_HEADS, 2, C_NOPE).transpose(0, 1, 3, 2, 4)
    wukv = wukv.reshape(depth, C_KV_LORA, 2 * C_HEADS * C_NOPE).astype(BF16)
    vecs = lambda v: v.reshape(depth, 1, -1)
    g_mix_pre, g_mix_post, g_ffn_pre, g_ffn_post = map(vecs, (g_mix_pre, g_mix_post, g_ffn_pre, g_ffn_post))
    g_qn_b, g_kn_b, g_cq, g_ckv = map(vecs, (g_qn_b, g_kn_b, g_cq, g_ckv))

    xs = (x.reshape(n_lat, d), ctx.reshape(n_ctx, d))
    sc_ab = 1.0 / math.sqrt(HEAD_DIM)
    sc_c = 1.0 / math.sqrt(C_NOPE + C_ROPE)
    dims = dict(batch=batch, seq=seq, ctx_len=ctx_len)

    for l in range(depth):
        last = l == depth - 1
        qa, ka, va, qb, kb, vtb, qc, kc, vtc, wo = _qkv_call(
            l, xs, mods, g_mix_pre, win, wkr, tabs, g_qn_b, g_kn_b, g_cq, g_ckv, wuq, wukv, w_out,
            n_lat_tiles=n_lat_tiles, tiles_per_seq=tiles_per_seq, n_seq=batch)
        oa = _attn_window_call(l, qa, ka, va, sink_a, n_groups=A_KV, heads_per_group=A_HEADS // A_KV,
                               dk=HEAD_DIM, dv=HEAD_DIM, scale=sc_ab, **dims)
        ob = _attn_dense_call(qb, kb, vtb, n_groups=B_KV, heads_per_group=B_HEADS // B_KV, dk=HEAD_DIM,
                              dv=HEAD_DIM, scale=sc_ab, tq=TQ_GQA, **dims)
        oc, w2 = _attn_dense_call(qc, kc, vtc, n_groups=C_HEADS, heads_per_group=1, dk=C_QK_PAD,
                                  dv=C_V, scale=sc_c, tq=TQ_MLA, shift_lane=MLA_SHIFT_LANE,
                                  cast_weight=(l, w_ffn2), **dims)
        ctx_outs = None if last else _attn_ctx_call(l, sink_a, qa, ka, va, qb, kb, vtb, qc, kc, vtc,
                                                    scale_ab=sc_ab, scale_c=sc_c, **dims)
        n_tiles = n_lat_tiles if last else n_lat_tiles + n_ctx // TM
        xa, h = _outproj_call(l, (oa, ob, oc), ctx_outs, xs, mods, g_mix_post, g_ffn_pre, wo, n_tiles=n_tiles,
                              n_lat_tiles=n_lat_tiles, tiles_per_seq=tiles_per_seq, n_seq=batch)
        tt = _ffn_up_call(l, h, w_ffn1, w_ffn3)
        xa = _ffn_down_call(l, tt, xa, mods, g_ffn_post, w2, tiles_per_seq=tiles_per_seq, n_seq=batch)
        xs = (xa,)
    return xa.reshape(batch, seq, d)
```

```python
import functools
import math

import jax
import jax.numpy as jnp
import numpy as np
from jax import lax
from jax.experimental import pallas as pl
from jax.experimental.pallas import tpu as pltpu

GRID_W = 64
HEAD_DIM = 128
A_HEADS, A_KV = 6, 2
B_HEADS, B_KV = 6, 2
WINDOW = 128
C_HEADS = 4
C_Q_LORA, C_KV_LORA = 512, 256
C_NOPE, C_ROPE, C_V = 128, 64, 128
C_QK_PAD = 256
MLA_SHIFT_LANE = C_NOPE + C_ROPE
VT_EXT = HEAD_DIM + 16
ROPE_THETA = 10000.0
EPS = 1e-6
NEG = -1e30
LOG2E = math.log2(math.e)
SAFE_SHIFT_LOG2 = 50.0
BOUND_SLACK = 1.001

LANES = 128
MOD_ROWS = 8
TM = 512
TQ = 256
SUBTILES_WIN = 4
TQ_GQA = 256
TQ_MLA = 1024
TK = 512
TF = 512
N_UP_TILES = 8
TN_MOD = 1024
MIB = 1024 * 1024

F32 = jnp.float32
BF16 = jnp.bfloat16


def _rmsnorm(x, g):
    return x * lax.rsqrt(jnp.mean(x * x, axis=-1, keepdims=True) + EPS) * g


def _dot(a, b):
    return jnp.dot(a, b, preferred_element_type=F32)


def _dot_nt(a, b):
    return lax.dot_general(a, b, (((1,), (1,)), ((), ())), preferred_element_type=F32)


def _dot_tn(a, b):
    return lax.dot_general(a, b, (((0,), (0,)), ((), ())), preferred_element_type=F32)


def _col(i):
    return slice(i * LANES, (i + 1) * LANES)


def _mod_kernel(c_ref, w_ref, b_ref, o_ref):
    cv = c_ref[...]
    s = cv * jax.nn.sigmoid(cv)
    o_ref[...] = _dot(s.astype(BF16), w_ref[...].astype(BF16)) + b_ref[...]


def _mod_call(cv, w_mod, b_mod):
    depth, d, n = w_mod.shape
    return pl.pallas_call(
        _mod_kernel,
        grid=(depth, n // TN_MOD),
        in_specs=[
            pl.BlockSpec((MOD_ROWS, d), lambda l, j: (0, 0)),
            pl.BlockSpec((None, d, TN_MOD), lambda l, j: (l, 0, j)),
            pl.BlockSpec((None, 1, TN_MOD), lambda l, j: (l, 0, j)),
        ],
        out_specs=pl.BlockSpec((None, MOD_ROWS, TN_MOD), lambda l, j: (l, 0, j)),
        out_shape=jax.ShapeDtypeStruct((depth, MOD_ROWS, n), F32),
        compiler_params=pltpu.CompilerParams(
            dimension_semantics=("arbitrary", "arbitrary"),
            vmem_limit_bytes=40 * MIB),
        name="adaln_mod",
    )(cv, w_mod, b_mod.reshape(depth, 1, n))


def _qkv_kernel(*refs, n_lat_tiles, split_x):
    x_refs, refs = (refs[:2], refs[2:]) if split_x else (refs[:1], refs[1:])
    (mod_ref, gpre_ref, win_ref, cos_ref, sin_ref, cos64_ref, sin64_ref,
     gqn_ref, gkn_ref, gcq_ref, gckv_ref, wuq_ref, wukv_ref, wo_ref,
     qa_ref, ka_ref, va_ref, qb_ref, kb_ref, vtb_ref, qc_ref, kc_ref, vtc_ref, wob_ref, k2_ref) = refs
    wob_ref[...] = wo_ref[...].astype(BF16)
    m = mod_ref[...]
    tr = TM // 2
    lane = lax.broadcasted_iota(jnp.int32, (tr, LANES), 1)
    low32 = (lane & 32) == 0
    low16 = (lane & 16) == 0
    stat_lane = lax.broadcasted_iota(jnp.int32, (MOD_ROWS, LANES), 1)

    def max_sq_norm(*parts):
        sq = None
        for part in parts:
            pf = part.astype(F32)
            s = jnp.sum(pf * pf, axis=-1, keepdims=True)
            sq = s if sq is None else sq + s
        return jnp.max(sq, axis=0, keepdims=True)

    for half in range(2):
        rows = slice(half * tr, (half + 1) * tr)
        x = x_refs[0][rows, :]
        if split_x:
            x = jnp.where(pl.program_id(0) >= n_lat_tiles, x_refs[1][rows, :], x)
        h = (_rmsnorm(x, gpre_ref[...]) * (1.0 + m[1:2]) + m[0:1]).astype(BF16)
        p = _dot(h, win_ref[...])
        cos, sin = cos_ref[rows, :], sin_ref[rows, :]
        cos64, sin64 = cos64_ref[rows, :], sin64_ref[rows, :]

        def rope128(t):
            partner = jnp.where(low32, pltpu.roll(t, LANES - 32, 1), pltpu.roll(t, 32, 1))
            return t * cos + partner * sin

        def rope64(t):
            partner = jnp.where(low16, pltpu.roll(t, LANES - 16, 1), pltpu.roll(t, 16, 1))
            return t * cos64 + partner * sin64

        off = 0
        for i in range(A_HEADS):
            qa_ref[rows, _col(i)] = rope128(p[:, _col(off + i)]).astype(BF16)
        off += A_HEADS
        for i in range(A_KV):
            ka_ref[rows, _col(i)] = rope128(p[:, _col(off + i)]).astype(BF16)
        off += A_KV
        for i in range(A_KV):
            va_ref[rows, _col(i)] = p[:, _col(off + i)].astype(BF16)
        off += A_KV

        gqn, gkn = gqn_ref[...], gkn_ref[...]
        for i in range(B_HEADS):
            qb_ref[rows, _col(i)] = rope128(_rmsnorm(p[:, _col(off + i)], gqn)).astype(BF16)
        off += B_HEADS
        k2_stats = jnp.zeros((MOD_ROWS, LANES), F32)
        for i in range(B_KV):
            kb = rope128(_rmsnorm(p[:, _col(off + i)], gkn)).astype(BF16)
            kb_ref[rows, _col(i)] = kb
            k2_stats = jnp.where(stat_lane == i, max_sq_norm(kb), k2_stats)
        off += B_KV
        for i in range(B_KV):
            vtb_ref[i * VT_EXT:i * VT_EXT + HEAD_DIM, rows] = p[:, _col(off + i)].T.astype(BF16)
            vtb_ref[i * VT_EXT + HEAD_DIM:(i + 1) * VT_EXT, rows] = jnp.ones((VT_EXT - HEAD_DIM, tr), BF16)
        off += B_KV

        c0 = off * LANES
        cq = _rmsnorm(p[:, c0:c0 + C_Q_LORA], gcq_ref[...])
        q = _dot(cq.astype(BF16), wuq_ref[...])
        c0 += C_Q_LORA
        ckv = _rmsnorm(p[:, c0:c0 + C_KV_LORA], gckv_ref[...])
        kv = _dot(ckv.astype(BF16), wukv_ref[...])
        c0 += C_KV_LORA
        ckr = jnp.concatenate([p[:, c0:c0 + C_ROPE], jnp.zeros((tr, LANES - C_ROPE), F32)], axis=1)
        krope = jnp.where(lane == C_ROPE, 1.0, rope64(ckr)).astype(BF16)
        for i in range(C_HEADS):
            qc_ref[rows, _col(2 * i)] = q[:, _col(2 * i)].astype(BF16)
            qc_ref[rows, _col(2 * i + 1)] = rope64(q[:, _col(2 * i + 1)]).astype(BF16)
            knope = kv[:, _col(i)].astype(BF16)
            kc_ref[rows, _col(2 * i)] = knope
            kc_ref[rows, _col(2 * i + 1)] = krope
            k2_stats = jnp.where(stat_lane == B_KV + i, max_sq_norm(knope, krope), k2_stats)
            vtc_ref[_col(i), rows] = kv[:, _col(C_HEADS + i)].T.astype(BF16)
        k2_ref[half] = k2_stats


def _qkv_call(l, xs, mods, gpre, win, tabs, gqn, gkn, gcq, gckv, wuq, wukv, w_out, *,
              n_lat_tiles, tiles_per_seq, n_seq):
    split_x = len(xs) == 2
    t, d = sum(a.shape[0] for a in xs), xs[0].shape[1]
    n_tiles = t // TM
    d_in = win.shape[2]
    d_mix = w_out.shape[1]
    if split_x:
        x_specs = [pl.BlockSpec((TM, d), lambda i: (jnp.minimum(i, n_lat_tiles - 1), 0)),
                   pl.BlockSpec((TM, d), lambda i: (jnp.maximum(i - n_lat_tiles, 0), 0))]
    else:
        x_specs = [pl.BlockSpec((TM, d), lambda i: (i, 0))]
    assert n_tiles >= d_mix // LANES
    wo_blk = lambda i: jnp.minimum(i, d_mix // LANES - 1)

    def seg(i):
        return jnp.minimum(i // tiles_per_seq, n_seq)

    def rope_blk(i):
        return jnp.where(i < n_lat_tiles, i % tiles_per_seq, tiles_per_seq)

    layer = lambda i: (l, 0, 0)
    vec = lambda n: pl.BlockSpec((None, 1, n), layer)
    tab_spec = pl.BlockSpec((TM, LANES), lambda i: (rope_blk(i), 0))
    row_major = lambda w: (pl.BlockSpec((TM, w), lambda i: (i, 0)), jax.ShapeDtypeStruct((t, w), BF16))
    col_major = lambda w: (pl.BlockSpec((w, TM), lambda i: (0, i)), jax.ShapeDtypeStruct((w, t), BF16))
    outs = [row_major(A_HEADS * HEAD_DIM), row_major(A_KV * HEAD_DIM), row_major(A_KV * HEAD_DIM),
            row_major(B_HEADS * HEAD_DIM), row_major(B_KV * HEAD_DIM), col_major(B_KV * VT_EXT),
            row_major(C_HEADS * C_QK_PAD), row_major(C_HEADS * C_QK_PAD), col_major(C_HEADS * C_V),
            (pl.BlockSpec((LANES, d), lambda i: (wo_blk(i), 0)), jax.ShapeDtypeStruct((d_mix, d), BF16)),
            (pl.BlockSpec((2, MOD_ROWS, LANES), lambda i: (i, 0, 0)),
             jax.ShapeDtypeStruct((2 * n_tiles, MOD_ROWS, LANES), F32))]
    return pl.pallas_call(
        functools.partial(_qkv_kernel, n_lat_tiles=n_lat_tiles, split_x=split_x),
        grid=(n_tiles,),
        in_specs=x_specs + [
            pl.BlockSpec((None, None, MOD_ROWS, d), lambda i: (l, seg(i), 0, 0)),
            vec(d),
            pl.BlockSpec((None, d, d_in), layer, pipeline_mode=pl.Buffered(1)),
            tab_spec, tab_spec, tab_spec, tab_spec,
            vec(HEAD_DIM), vec(HEAD_DIM), vec(C_Q_LORA), vec(C_KV_LORA),
            pl.BlockSpec((None,) + wuq.shape[1:], layer, pipeline_mode=pl.Buffered(1)),
            pl.BlockSpec((None,) + wukv.shape[1:], layer, pipeline_mode=pl.Buffered(1)),
            pl.BlockSpec((None, LANES, d), lambda i: (l, wo_blk(i), 0)),
        ],
        out_specs=[o[0] for o in outs],
        out_shape=[o[1] for o in outs],
        compiler_params=pltpu.CompilerParams(
            dimension_semantics=("arbitrary",),
            vmem_limit_bytes=56 * MIB),
        name="qkv_prep",
    )(*xs, mods, gpre, win, *tabs, gqn, gkn, gcq, gckv, wuq, wukv, w_out)


def _attn_window_kernel(sink_ref, q_ref, kl_ref, vl_ref, kc_ref, vc_ref, o_ref, *,
                        layer, heads_per_group, dk, dv, scale, seq):
    g = pl.program_id(1)
    qi = pl.program_id(2)
    c2 = scale * LOG2E
    band = TQ + 2 * WINDOW
    heads = range(heads_per_group)
    sink2 = jnp.concatenate(
        [jnp.full((1, TQ), sink_ref[layer, g * heads_per_group + r] * LOG2E, F32) for r in heads], axis=1)
    for sub in range(SUBTILES_WIN):
        rows = slice(sub * TQ, (sub + 1) * TQ)
        q0 = (qi * SUBTILES_WIN + sub) * TQ
        q = jnp.concatenate([q_ref[rows, r * dk:(r + 1) * dk] for r in heads], axis=0)
        start = pl.multiple_of(jnp.clip(q0 - WINDOW, 0, seq - band), LANES)
        s_c = _dot_nt(kc_ref[...], q)
        s_l = _dot_nt(kl_ref[pl.ds(start, band), :], q)
        kpos = start + lax.broadcasted_iota(jnp.int32, (band, TQ), 0)
        qpos = q0 + lax.broadcasted_iota(jnp.int32, (band, TQ), 1)
        in_window = jnp.abs(kpos - qpos) <= WINDOW
        s_l = jnp.where(jnp.concatenate([in_window] * heads_per_group, axis=1), s_l, NEG)
        m = jnp.maximum(jnp.maximum(jnp.max(s_c, axis=0, keepdims=True), jnp.max(s_l, axis=0, keepdims=True)) * c2,
                        sink2)
        p_c = jnp.exp2(s_c * c2 - m)
        p_l = jnp.exp2(s_l * c2 - m)
        denom = (jnp.sum(p_c, axis=0, keepdims=True) + jnp.sum(p_l, axis=0, keepdims=True)
                 + jnp.exp2(sink2 - m))
        acc = _dot_tn(vc_ref[...], p_c.astype(BF16)) + _dot_tn(vl_ref[pl.ds(start, band), :], p_l.astype(BF16))
        o = acc / denom
        for r in heads:
            o_ref[rows, r * dv:(r + 1) * dv] = o[:, r * TQ:(r + 1) * TQ].T.astype(BF16)


def _attn_window_call(l, q, k, v, sink, *, n_groups, heads_per_group, dk, dv, scale, batch, seq, ctx_len):
    tq = SUBTILES_WIN * TQ
    n_q = seq // tq
    ctx_blk0 = batch * seq // ctx_len
    return pl.pallas_call(
        functools.partial(_attn_window_kernel, layer=l, heads_per_group=heads_per_group, dk=dk, dv=dv,
                          scale=scale, seq=seq),
        grid=(batch, n_groups, n_q),
        in_specs=[
            pl.BlockSpec(memory_space=pltpu.SMEM),
            pl.BlockSpec((tq, heads_per_group * dk), lambda b, g, qi: (b * n_q + qi, g)),
            pl.BlockSpec((seq, dk), lambda b, g, qi: (b, g)),
            pl.BlockSpec((seq, dv), lambda b, g, qi: (b, g)),
            pl.BlockSpec((ctx_len, dk), lambda b, g, qi: (ctx_blk0 + b, g)),
            pl.BlockSpec((ctx_len, dv), lambda b, g, qi: (ctx_blk0 + b, g)),
        ],
        out_specs=pl.BlockSpec((tq, heads_per_group * dv), lambda b, g, qi: (b * n_q + qi, g)),
        out_shape=jax.ShapeDtypeStruct((batch * seq, n_groups * heads_per_group * dv), BF16),
        compiler_params=pltpu.CompilerParams(
            dimension_semantics=("arbitrary", "arbitrary", "arbitrary"),
            vmem_limit_bytes=48 * MIB),
        name="attn_window",
    )(sink, q, k, v, k, v)


def _attn_dense_kernel(*refs, heads_per_group, dk, dv, scale, tq, shift_lane, cast_side_job, k2_col0):
    if cast_side_job:
        k2_ref, q_ref, kl_ref, vtl_ref, kc_ref, vtc_ref, w_ref, o_ref, wb_ref = refs
        wb_ref[...] = w_ref[...].astype(BF16)
    else:
        k2_ref, q_ref, kl_ref, vtl_ref, kc_ref, vtc_ref, o_ref = refs
    c2 = scale * LOG2E
    seq = kl_ref.shape[0]
    k2 = k2_ref[pl.program_id(0), k2_col0 + pl.program_id(1)]

    if heads_per_group > 1:
        q = jnp.concatenate([q_ref[:, r * dk:(r + 1) * dk] for r in range(heads_per_group)], axis=0)
    else:
        q = q_ref[...]
    qf = q.astype(F32)
    q2 = jnp.sum(qf * qf, axis=-1, keepdims=True)
    bound = jnp.sqrt(q2 * k2) * BOUND_SLACK
    bound_is_safe = jnp.max(bound) * c2 <= SAFE_SHIFT_LOG2

    def finish(acc, denom):
        o = acc / denom
        for r in range(heads_per_group):
            o_ref[:, r * dv:(r + 1) * dv] = o[:, r * tq:(r + 1) * tq].T.astype(BF16)

    @pl.when(bound_is_safe)
    def _():
        if shift_lane is None:
            shift = bound.T * c2
            p_c = jnp.exp2(_dot_nt(kc_ref[...], q) * c2 - shift)
            p_l = jnp.exp2(_dot_nt(kl_ref[...], q) * c2 - shift)
        else:
            lane = lax.broadcasted_iota(jnp.int32, q.shape, 1)
            q_shifted = jnp.where(lane == shift_lane, -bound, qf).astype(BF16)
            p_c = jnp.exp2(_dot_nt(kc_ref[...], q_shifted) * c2)
            p_l = jnp.exp2(_dot_nt(kl_ref[...], q_shifted) * c2)
        acc = _dot(vtc_ref[...], p_c.astype(BF16)) + _dot(vtl_ref[...], p_l.astype(BF16))
        if vtl_ref.shape[0] > dv:
            finish(acc[:dv], acc[dv:dv + 1])
        else:
            finish(acc, jnp.sum(p_c, axis=0, keepdims=True) + jnp.sum(p_l, axis=0, keepdims=True))

    @pl.when(jnp.logical_not(bound_is_safe))
    def _():
        chunks = [(lambda: kc_ref[...], lambda: vtc_ref[0:dv, :])]
        for j in range(seq // TK):
            chunks.append((functools.partial(lambda j: kl_ref[j * TK:(j + 1) * TK, :], j),
                           functools.partial(lambda j: vtl_ref[0:dv, j * TK:(j + 1) * TK], j)))
        n = len(chunks)
        s_next = _dot_nt(chunks[0][0](), q)
        m = denom = acc = None
        for j in range(n):
            s = s_next
            if j + 1 < n:
                s_next = _dot_nt(chunks[j + 1][0](), q)
            m_j = jnp.max(s, axis=0, keepdims=True) * c2
            m_new = m_j if j == 0 else jnp.maximum(m, m_j)
            p = jnp.exp2(s * c2 - m_new)
            pv = _dot(chunks[j][1](), p.astype(BF16))
            if j == 0:
                denom = jnp.sum(p, axis=0, keepdims=True)
                acc = pv
            else:
                alpha = jnp.exp2(m - m_new)
                denom = denom * alpha + jnp.sum(p, axis=0, keepdims=True)
                acc = acc * alpha + pv
            m = m_new
        finish(acc, denom)


def _attn_dense_call(k2, k2_col0, q, k, vt, *, n_groups, heads_per_group, dk, dv, scale, batch, seq, ctx_len, tq,
                     shift_lane=None, cast_weight=None):
    n_q = seq // tq
    ctx_blk0 = batch * seq // ctx_len
    vt_rows = vt.shape[0] // n_groups
    in_specs = [
        pl.BlockSpec(memory_space=pltpu.SMEM),
        pl.BlockSpec((tq, heads_per_group * dk), lambda b, g, qi: (b * n_q + qi, g)),
        pl.BlockSpec((seq, dk), lambda b, g, qi: (b, g)),
        pl.BlockSpec((vt_rows, seq), lambda b, g, qi: (g, b)),
        pl.BlockSpec((ctx_len, dk), lambda b, g, qi: (ctx_blk0 + b, g)),
        pl.BlockSpec((vt_rows, ctx_len), lambda b, g, qi: (g, ctx_blk0 + b)),
    ]
    args = [k2, q, k, vt, k, vt]
    out_specs = [pl.BlockSpec((tq, heads_per_group * dv), lambda b, g, qi: (b * n_q + qi, g))]
    out_shape = [jax.ShapeDtypeStruct((batch * seq, n_groups * heads_per_group * dv), BF16)]
    if cast_weight is not None:
        l, w = cast_weight
        n_steps = batch * n_groups * n_q
        rows, cols = w.shape[1] // n_steps, w.shape[2]
        assert rows * n_steps == w.shape[1] and rows % 16 == 0
        step = lambda b, g, qi: (b * n_groups + g) * n_q + qi
        in_specs.append(pl.BlockSpec((None, rows, cols), lambda b, g, qi: (l, step(b, g, qi), 0)))
        args.append(w)
        out_specs.append(pl.BlockSpec((rows, cols), lambda b, g, qi: (step(b, g, qi), 0)))
        out_shape.append(jax.ShapeDtypeStruct(w.shape[1:], BF16))
    outs = pl.pallas_call(
        functools.partial(_attn_dense_kernel, heads_per_group=heads_per_group, dk=dk, dv=dv, scale=scale, tq=tq,
                          shift_lane=shift_lane, cast_side_job=cast_weight is not None, k2_col0=k2_col0),
        grid=(batch, n_groups, n_q),
        in_specs=in_specs,
        out_specs=out_specs,
        out_shape=out_shape,
        compiler_params=pltpu.CompilerParams(
            dimension_semantics=("arbitrary", "arbitrary", "arbitrary"),
            vmem_limit_bytes=48 * MIB),
        name=f"attn_dense_dk{dk}",
    )(*args)
    return outs if cast_weight is not None else outs[0]


def _attn_ctx_kernel(sink_ref, qa_ref, ka_ref, va_ref, qb_ref, kb_ref, vtb_ref, qc_ref, kc_ref, vtc_ref,
                     oa_ref, ob_ref, oc_ref, *, layer, scale_ab, scale_c):
    def head(q, k, scale, pv, sink2=None):
        c2 = scale * LOG2E
        s = _dot_nt(q, k)
        m = jnp.max(s, axis=-1, keepdims=True) * c2
        if sink2 is not None:
            m = jnp.maximum(m, sink2)
        p = jnp.exp2(s * c2 - m)
        denom = jnp.sum(p, axis=-1, keepdims=True)
        if sink2 is not None:
            denom = denom + jnp.exp2(sink2 - m)
        return (pv(p.astype(BF16)) / denom).astype(BF16)

    for hd in range(A_HEADS):
        g = hd // (A_HEADS // A_KV)
        oa_ref[:, _col(hd)] = head(qa_ref[:, _col(hd)], ka_ref[:, _col(g)], scale_ab,
                                   lambda p: _dot(p, va_ref[:, _col(g)]), sink_ref[layer, hd] * LOG2E)
    for hd in range(B_HEADS):
        g = hd // (B_HEADS // B_KV)
        ob_ref[:, _col(hd)] = head(qb_ref[:, _col(hd)], kb_ref[:, _col(g)], scale_ab,
                                   lambda p: _dot_nt(p, vtb_ref[g * VT_EXT:g * VT_EXT + HEAD_DIM, :]))
    for hd in range(C_HEADS):
        qk = slice(hd * C_QK_PAD, (hd + 1) * C_QK_PAD)
        oc_ref[:, _col(hd)] = head(qc_ref[:, qk], kc_ref[:, qk], scale_c,
                                   lambda p: _dot_nt(p, vtc_ref[_col(hd), :]))


def _attn_ctx_call(l, sink, qa, ka, va, qb, kb, vtb, qc, kc, vtc, *, scale_ab, scale_c, batch, seq, ctx_len):
    blk0 = batch * seq // ctx_len
    rows = lambda a: pl.BlockSpec((ctx_len, a.shape[1]), lambda b: (blk0 + b, 0))
    cols = lambda a: pl.BlockSpec((a.shape[0], ctx_len), lambda b: (0, blk0 + b))
    out = lambda w: (pl.BlockSpec((ctx_len, w), lambda b: (b, 0)),
                     jax.ShapeDtypeStruct((batch * ctx_len, w), BF16))
    outs = [out(A_HEADS * HEAD_DIM), out(B_HEADS * HEAD_DIM), out(C_HEADS * C_V)]
    return pl.pallas_call(
        functools.partial(_attn_ctx_kernel, layer=l, scale_ab=scale_ab, scale_c=scale_c),
        grid=(batch,),
        in_specs=[pl.BlockSpec(memory_space=pltpu.SMEM), rows(qa), rows(ka), rows(va),
                  rows(qb), rows(kb), cols(vtb), rows(qc), rows(kc), cols(vtc)],
        out_specs=[o[0] for o in outs],
        out_shape=[o[1] for o in outs],
        compiler_params=pltpu.CompilerParams(
            dimension_semantics=("arbitrary",),
            vmem_limit_bytes=32 * MIB),
        name="attn_ctx",
    )(sink, qa, ka, va, qb, kb, vtb, qc, kc, vtc)


def _outproj_kernel(*refs, n_lat_tiles, has_ctx, split_x):
    lat_refs, refs = refs[:3], refs[3:]
    if has_ctx:
        ctx_refs, refs = refs[:3], refs[3:]
    x_refs, refs = (refs[:2], refs[2:]) if split_x else (refs[:1], refs[1:])
    mod_ref, gpost_ref, gffn_ref, wo_ref, xo_ref, h_ref = refs
    is_ctx_tile = pl.program_id(0) >= n_lat_tiles
    m = mod_ref[...]
    for half in range(2):
        rows = slice(half * (TM // 2), (half + 1) * (TM // 2))
        o, w0 = None, 0
        for g, lat_ref in enumerate(lat_refs):
            og = lat_ref[rows, :]
            if has_ctx:
                og = jnp.where(is_ctx_tile, ctx_refs[g][rows, :], og)
            width = lat_ref.shape[1]
            part = _dot(og, wo_ref[w0:w0 + width, :])
            o = part if o is None else o + part
            w0 += width
        x = x_refs[0][rows, :]
        if split_x:
            x = jnp.where(is_ctx_tile, x_refs[1][rows, :], x)
        x_new = x + m[2:3] * _rmsnorm(o, gpost_ref[...])
        xo_ref[rows, :] = x_new
        h_ref[rows, :] = (_rmsnorm(x_new, gffn_ref[...]) * (1.0 + m[4:5]) + m[3:4]).astype(BF16)


def _outproj_call(l, lat_outs, ctx_outs, xs, mods, gpost, gffn, wo, *, n_tiles, n_lat_tiles, tiles_per_seq, n_seq):
    t = n_tiles * TM
    d = xs[0].shape[1]
    seg = lambda i: jnp.minimum(i // tiles_per_seq, n_seq)
    layer = lambda i: (l, 0, 0)
    has_ctx = ctx_outs is not None
    split_x = len(xs) == 2
    lat_rows = lambda a: pl.BlockSpec((TM, a.shape[1]), lambda i: (jnp.minimum(i, n_lat_tiles - 1), 0))
    ctx_rows = lambda a: pl.BlockSpec((TM, a.shape[1]), lambda i: (jnp.maximum(i - n_lat_tiles, 0), 0))
    in_specs = [lat_rows(a) for a in lat_outs]
    args = list(lat_outs)
    if has_ctx:
        in_specs += [ctx_rows(a) for a in ctx_outs]
        args += list(ctx_outs)
    in_specs += [lat_rows(xs[0]), ctx_rows(xs[1])] if split_x else [pl.BlockSpec((TM, d), lambda i: (i, 0))]
    args += list(xs)
    in_specs += [
        pl.BlockSpec((None, None, MOD_ROWS, d), lambda i: (l, seg(i), 0, 0)),
        pl.BlockSpec((None, 1, d), layer),
        pl.BlockSpec((None, 1, d), layer),
        pl.BlockSpec(wo.shape, lambda i: (0, 0), pipeline_mode=pl.Buffered(1)),
    ]
    args += [mods, gpost, gffn, wo]
    return pl.pallas_call(
        functools.partial(_outproj_kernel, n_lat_tiles=n_lat_tiles, has_ctx=has_ctx, split_x=split_x),
        grid=(n_tiles,),
        in_specs=in_specs,
        out_specs=[pl.BlockSpec((TM, d), lambda i: (i, 0)), pl.BlockSpec((TM, d), lambda i: (i, 0))],
        out_shape=[jax.ShapeDtypeStruct((t, d), F32), jax.ShapeDtypeStruct((t, d), BF16)],
        compiler_params=pltpu.CompilerParams(
            dimension_semantics=("arbitrary",),
            vmem_limit_bytes=48 * MIB),
        name="outproj",
    )(*args)


def _ffn_up_kernel(h_ref, w1_ref, w3_ref, t_ref, w1_scr, w3_scr):
    @pl.when(pl.program_id(1) == 0)
    def _():
        w1_scr[...] = w1_ref[...].astype(BF16)
        w3_scr[...] = w3_ref[...].astype(BF16)

    h = h_ref[...]
    a = _dot(h, w1_scr[...])
    b = _dot(h, w3_scr[...])
    t_ref[...] = (a * jax.nn.sigmoid(a) * b).astype(BF16)


def _ffn_up_call(l, h, w1, w3):
    t, d = h.shape
    f = w1.shape[2]
    tm = t // N_UP_TILES
    return pl.pallas_call(
        _ffn_up_kernel,
        grid=(f // TF, N_UP_TILES),
        in_specs=[
            pl.BlockSpec((tm, d), lambda j, i: (i, 0)),
            pl.BlockSpec((None, d, TF), lambda j, i: (l, 0, j)),
            pl.BlockSpec((None, d, TF), lambda j, i: (l, 0, j)),
        ],
        out_specs=pl.BlockSpec((tm, TF), lambda j, i: (i, j)),
        out_shape=jax.ShapeDtypeStruct((t, f), BF16),
        scratch_shapes=[pltpu.VMEM((d, TF), BF16), pltpu.VMEM((d, TF), BF16)],
        compiler_params=pltpu.CompilerParams(
            dimension_semantics=("arbitrary", "arbitrary"),
            vmem_limit_bytes=48 * MIB),
        name="ffn_up",
    )(h, w1, w3)


def _ffn_down_kernel(t_ref, x_ref, mod_ref, gpost_ref, w2_ref, o_ref):
    y = _dot(t_ref[...], w2_ref[...])
    m = mod_ref[...]
    o_ref[...] = x_ref[...] + m[5:6] * _rmsnorm(y, gpost_ref[...])


def _ffn_down_call(l, tt, xa, mods, gpost, w2, *, tiles_per_seq, n_seq):
    t, d = xa.shape
    f = w2.shape[0]
    seg = lambda i: jnp.minimum(i // tiles_per_seq, n_seq)
    layer = lambda i: (l, 0, 0)
    return pl.pallas_call(
        _ffn_down_kernel,
        grid=(t // TM,),
        in_specs=[
            pl.BlockSpec((TM, f), lambda i: (i, 0)),
            pl.BlockSpec((TM, d), lambda i: (i, 0)),
            pl.BlockSpec((None, None, MOD_ROWS, d), lambda i: (l, seg(i), 0, 0)),
            pl.BlockSpec((None, 1, d), layer),
            pl.BlockSpec((f, d), lambda i: (0, 0), pipeline_mode=pl.Buffered(1)),
        ],
        out_specs=pl.BlockSpec((TM, d), lambda i: (i, 0)),
        out_shape=jax.ShapeDtypeStruct((t, d), F32),
        compiler_params=pltpu.CompilerParams(
            dimension_semantics=("arbitrary",),
            vmem_limit_bytes=60 * MIB),
        name="ffn_down",
    )(tt, xa, mods, gpost, w2)


def _rope_tables(seq):
    pos = np.arange(seq)
    row = (pos // GRID_W).astype(np.float64)[:, None]
    colp = (pos % GRID_W).astype(np.float64)[:, None]

    def cs(n):
        freqs = np.power(np.float64(ROPE_THETA), -np.arange(n, dtype=np.float64) / n)[None, :]
        return (np.cos(row * freqs), np.sin(row * freqs), np.cos(colp * freqs), np.sin(colp * freqs))

    cr, sr, cc, sc = cs(HEAD_DIM // 4)
    cos128 = np.concatenate([cr, cr, cc, cc], axis=1)
    sin128 = np.concatenate([-sr, sr, -sc, sc], axis=1)
    cr, sr, cc, sc = cs(C_ROPE // 4)
    pad1 = np.ones((seq, LANES - C_ROPE), np.float32)
    pad0 = np.zeros((seq, LANES - C_ROPE), np.float32)
    cos64 = np.concatenate([cr, cr, cc, cc, pad1], axis=1)
    sin64 = np.concatenate([-sr, sr, -sc, sc, pad0], axis=1)
    ident_c = np.ones((TM, LANES), np.float32)
    ident_s = np.zeros((TM, LANES), np.float32)
    return tuple(jnp.asarray(np.concatenate([tab, ident], axis=0).astype(np.float32))
                 for tab, ident in ((cos128, ident_c), (sin128, ident_s), (cos64, ident_c), (sin64, ident_s)))


def kernel(x, c, ctx, c_ctx, w_mod, b_mod, g_mix_pre, g_mix_post, g_ffn_pre, g_ffn_post, w_in, sink_a,
           g_qn_b, g_kn_b, g_cq, w_uq, g_ckv, w_ukv, w_out, w_ffn1, w_ffn3, w_ffn2):
    batch, seq, d = x.shape
    ctx_len = ctx.shape[1]
    depth = w_mod.shape[0]
    n_lat, n_ctx = batch * seq, batch * ctx_len
    assert seq % TM == 0 and n_ctx % TM == 0 and seq % ctx_len == 0 and ctx_len % TQ == 0
    assert seq % TQ_GQA == 0 and seq % TQ_MLA == 0 and seq % TK == 0 and ctx_len % LANES == 0
    assert batch + 1 <= MOD_ROWS and seq % GRID_W == 0 and seq % (SUBTILES_WIN * TQ) == 0
    assert n_lat % (16 * N_UP_TILES) == 0 and (n_lat + n_ctx) % (16 * N_UP_TILES) == 0
    assert ctx_len == TM // 2
    tiles_per_seq = seq // TM
    n_lat_tiles = n_lat // TM

    cv = jnp.zeros((MOD_ROWS, d), F32).at[:batch].set(c).at[batch].set(c_ctx)
    mod_all = _mod_call(cv, w_mod, b_mod)
    mods = mod_all[:, :batch + 1].reshape(depth, batch + 1, 6, d)
    mods = jnp.pad(mods, ((0, 0), (0, 0), (0, MOD_ROWS - 6), (0, 0)))

    tabs = _rope_tables(seq)

    win = w_in.astype(BF16)
    wuq = w_uq.reshape(depth, C_Q_LORA, C_HEADS, C_NOPE + C_ROPE)
    wuq = jnp.pad(wuq, ((0, 0), (0, 0), (0, 0), (0, C_QK_PAD - C_NOPE - C_ROPE)))
    wuq = wuq.reshape(depth, C_Q_LORA, C_HEADS * C_QK_PAD).astype(BF16)
    wukv = w_ukv.reshape(depth, C_KV_LORA, C_HEADS, 2, C_NOPE).transpose(0, 1, 3, 2, 4)
    wukv = wukv.reshape(depth, C_KV_LORA, 2 * C_HEADS * C_NOPE).astype(BF16)
    vecs = lambda v: v.reshape(depth, 1, -1)
    g_mix_pre, g_mix_post, g_ffn_pre, g_ffn_post = map(vecs, (g_mix_pre, g_mix_post, g_ffn_pre, g_ffn_post))
    g_qn_b, g_kn_b, g_cq, g_ckv = map(vecs, (g_qn_b, g_kn_b, g_cq, g_ckv))

    xs = (x.reshape(n_lat, d), ctx.reshape(n_ctx, d))
    sc_ab = 1.0 / math.sqrt(HEAD_DIM)
    sc_c = 1.0 / math.sqrt(C_NOPE + C_ROPE)
    dims = dict(batch=batch, seq=seq, ctx_len=ctx_len)

    for l in range(depth):
        last = l == depth - 1
        qa, ka, va, qb, kb, vtb, qc, kc, vtc, wo, k2_halves = _qkv_call(
            l, xs, mods, g_mix_pre, win, tabs, g_qn_b, g_kn_b, g_cq, g_ckv, wuq, wukv, w_out,
            n_lat_tiles=n_lat_tiles, tiles_per_seq=tiles_per_seq, n_seq=batch)
        k2_halves = k2_halves[:, 0, :]
        k2 = jnp.maximum(jnp.max(k2_halves[:2 * n_lat_tiles].reshape(batch, -1, LANES), axis=1),
                         k2_halves[2 * n_lat_tiles:])
        oa = _attn_window_call(l, qa, ka, va, sink_a, n_groups=A_KV, heads_per_group=A_HEADS // A_KV,
                               dk=HEAD_DIM, dv=HEAD_DIM, scale=sc_ab, **dims)
        ob = _attn_dense_call(k2, 0, qb, kb, vtb, n_groups=B_KV, heads_per_group=B_HEADS // B_KV, dk=HEAD_DIM,
                              dv=HEAD_DIM, scale=sc_ab, tq=TQ_GQA, **dims)
        oc, w2 = _attn_dense_call(k2, B_KV, qc, kc, vtc, n_groups=C_HEADS, heads_per_group=1, dk=C_QK_PAD,
                                  dv=C_V, scale=sc_c, tq=TQ_MLA, shift_lane=MLA_SHIFT_LANE,
                                  cast_weight=(l, w_ffn2), **dims)
        ctx_outs = None if last else _attn_ctx_call(l, sink_a, qa, ka, va, qb, kb, vtb, qc, kc, vtc,
                                                    scale_ab=sc_ab, scale_c=sc_c, **dims)
        n_tiles = n_lat_tiles if last else n_lat_tiles + n_ctx // TM
        xa, h = _outproj_call(l, (oa, ob, oc), ctx_outs, xs, mods, g_mix_post, g_ffn_pre, wo, n_tiles=n_tiles,
                              n_lat_tiles=n_lat_tiles, tiles_per_seq=tiles_per_seq, n_seq=batch)
        tt = _ffn_up_call(l, h, w_ffn1, w_ffn3)
        xa = _ffn_down_call(l, tt, xa, mods, g_ffn_post, w2, tiles_per_seq=tiles_per_seq, n_seq=batch)
        xs = (xa,)
    return xa.reshape(batch, seq, d)
```

```python
import functools
import math

import jax
import jax.numpy as jnp
import numpy as np
from jax import lax
from jax.experimental import pallas as pl
from jax.experimental.pallas import tpu as pltpu

GRID_W = 64
HEAD_DIM = 128
A_HEADS, A_KV = 6, 2
B_HEADS, B_KV = 6, 2
WINDOW = 128
C_HEADS = 4
C_Q_LORA, C_KV_LORA = 512, 256
C_NOPE, C_ROPE, C_V = 128, 64, 128
C_QK_PAD = 256
MLA_SHIFT_LANE = C_NOPE + C_ROPE
VT_EXT = HEAD_DIM + 16
ROPE_THETA = 10000.0
EPS = 1e-6
NEG = -1e30
LOG2E = math.log2(math.e)
SAFE_SHIFT_LOG2 = 50.0
BOUND_SLACK = 1.001

LANES = 128
MOD_ROWS = 8
TM = 512
TQ = 256
SUBTILES_WIN = 8
TQ_GQA = 512
TQ_MLA = 1024
TK = 512
TF = 512
N_UP_TILES = 8
TN_MOD = 1024
MIB = 1024 * 1024

F32 = jnp.float32
BF16 = jnp.bfloat16


def _rmsnorm(x, g):
    return x * lax.rsqrt(jnp.mean(x * x, axis=-1, keepdims=True) + EPS) * g


def _dot(a, b):
    return jnp.dot(a, b, preferred_element_type=F32)


def _dot_nt(a, b):
    return lax.dot_general(a, b, (((1,), (1,)), ((), ())), preferred_element_type=F32)


def _dot_tn(a, b):
    return lax.dot_general(a, b, (((0,), (0,)), ((), ())), preferred_element_type=F32)


def _col(i):
    return slice(i * LANES, (i + 1) * LANES)


def _mod_kernel(c_ref, w_ref, b_ref, o_ref):
    cv = c_ref[...]
    s = cv * jax.nn.sigmoid(cv)
    o_ref[...] = _dot(s.astype(BF16), w_ref[...].astype(BF16)) + b_ref[...]


def _mod_call(cv, w_mod, b_mod):
    depth, d, n = w_mod.shape
    return pl.pallas_call(
        _mod_kernel,
        grid=(depth, n // TN_MOD),
        in_specs=[
            pl.BlockSpec((MOD_ROWS, d), lambda l, j: (0, 0)),
            pl.BlockSpec((None, d, TN_MOD), lambda l, j: (l, 0, j)),
            pl.BlockSpec((None, 1, TN_MOD), lambda l, j: (l, 0, j)),
        ],
        out_specs=pl.BlockSpec((None, MOD_ROWS, TN_MOD), lambda l, j: (l, 0, j)),
        out_shape=jax.ShapeDtypeStruct((depth, MOD_ROWS, n), F32),
        compiler_params=pltpu.CompilerParams(
            dimension_semantics=("arbitrary", "arbitrary"),
            vmem_limit_bytes=40 * MIB),
        name="adaln_mod",
    )(cv, w_mod, b_mod.reshape(depth, 1, n))


def _qkv_kernel(*refs, n_lat_tiles, split_x):
    x_refs, refs = (refs[:2], refs[2:]) if split_x else (refs[:1], refs[1:])
    (mod_ref, gpre_ref, win_ref, cos_ref, sin_ref, cos64_ref, sin64_ref,
     gqn_ref, gkn_ref, gcq_ref, gckv_ref, wuq_ref, wukv_ref, wo_ref,
     qa_ref, ka_ref, va_ref, qb_ref, kb_ref, vtb_ref, qc_ref, kc_ref, vtc_ref, wob_ref, k2_ref) = refs
    wob_ref[...] = wo_ref[...].astype(BF16)
    m = mod_ref[...]
    tr = TM // 2
    lane = lax.broadcasted_iota(jnp.int32, (tr, LANES), 1)
    low32 = (lane & 32) == 0
    low16 = (lane & 16) == 0
    stat_lane = lax.broadcasted_iota(jnp.int32, (MOD_ROWS, LANES), 1)

    def max_sq_norm(*parts):
        sq = None
        for part in parts:
            pf = part.astype(F32)
            s = jnp.sum(pf * pf, axis=-1, keepdims=True)
            sq = s if sq is None else sq + s
        return jnp.max(sq, axis=0, keepdims=True)

    for half in range(2):
        rows = slice(half * tr, (half + 1) * tr)
        x = x_refs[0][rows, :]
        if split_x:
            x = jnp.where(pl.program_id(0) >= n_lat_tiles, x_refs[1][rows, :], x)
        h = (_rmsnorm(x, gpre_ref[...]) * (1.0 + m[1:2]) + m[0:1]).astype(BF16)
        p = _dot(h, win_ref[...])
        cos, sin = cos_ref[rows, :], sin_ref[rows, :]
        cos64, sin64 = cos64_ref[rows, :], sin64_ref[rows, :]

        def rope128(t):
            partner = jnp.where(low32, pltpu.roll(t, LANES - 32, 1), pltpu.roll(t, 32, 1))
            return t * cos + partner * sin

        def rope64(t):
            partner = jnp.where(low16, pltpu.roll(t, LANES - 16, 1), pltpu.roll(t, 16, 1))
            return t * cos64 + partner * sin64

        off = 0
        for i in range(A_HEADS):
            qa_ref[rows, _col(i)] = rope128(p[:, _col(off + i)]).astype(BF16)
        off += A_HEADS
        for i in range(A_KV):
            ka_ref[rows, _col(i)] = rope128(p[:, _col(off + i)]).astype(BF16)
        off += A_KV
        for i in range(A_KV):
            va_ref[rows, _col(i)] = p[:, _col(off + i)].astype(BF16)
        off += A_KV

        gqn, gkn = gqn_ref[...], gkn_ref[...]
        for i in range(B_HEADS):
            qb_ref[rows, _col(i)] = rope128(_rmsnorm(p[:, _col(off + i)], gqn)).astype(BF16)
        off += B_HEADS
        k2_stats = jnp.zeros((MOD_ROWS, LANES), F32)
        for i in range(B_KV):
            kb = rope128(_rmsnorm(p[:, _col(off + i)], gkn)).astype(BF16)
            kb_ref[rows, _col(i)] = kb
            k2_stats = jnp.where(stat_lane == i, max_sq_norm(kb), k2_stats)
        off += B_KV
        for i in range(B_KV):
            vtb_ref[i * VT_EXT:i * VT_EXT + HEAD_DIM, rows] = p[:, _col(off + i)].T.astype(BF16)
            vtb_ref[i * VT_EXT + HEAD_DIM:(i + 1) * VT_EXT, rows] = jnp.ones((VT_EXT - HEAD_DIM, tr), BF16)
        off += B_KV

        c0 = off * LANES
        cq = _rmsnorm(p[:, c0:c0 + C_Q_LORA], gcq_ref[...])
        q = _dot(cq.astype(BF16), wuq_ref[...])
        c0 += C_Q_LORA
        ckv = _rmsnorm(p[:, c0:c0 + C_KV_LORA], gckv_ref[...])
        kv = _dot(ckv.astype(BF16), wukv_ref[...])
        c0 += C_KV_LORA
        ckr = jnp.concatenate([p[:, c0:c0 + C_ROPE], jnp.zeros((tr, LANES - C_ROPE), F32)], axis=1)
        krope = jnp.where(lane == C_ROPE, 1.0, rope64(ckr)).astype(BF16)
        for i in range(C_HEADS):
            qc_ref[rows, _col(2 * i)] = q[:, _col(2 * i)].astype(BF16)
            qc_ref[rows, _col(2 * i + 1)] = rope64(q[:, _col(2 * i + 1)]).astype(BF16)
            knope = kv[:, _col(i)].astype(BF16)
            kc_ref[rows, _col(2 * i)] = knope
            kc_ref[rows, _col(2 * i + 1)] = krope
            k2_stats = jnp.where(stat_lane == B_KV + i, max_sq_norm(knope, krope), k2_stats)
            vtc_ref[_col(i), rows] = kv[:, _col(C_HEADS + i)].T.astype(BF16)
        k2_ref[half] = k2_stats


def _qkv_call(l, xs, mods, gpre, win, tabs, gqn, gkn, gcq, gckv, wuq, wukv, w_out, *,
              n_lat_tiles, tiles_per_seq, n_seq):
    split_x = len(xs) == 2
    t, d = sum(a.shape[0] for a in xs), xs[0].shape[1]
    n_tiles = t // TM
    d_in = win.shape[2]
    d_mix = w_out.shape[1]
    if split_x:
        x_specs = [pl.BlockSpec((TM, d), lambda i: (jnp.minimum(i, n_lat_tiles - 1), 0)),
                   pl.BlockSpec((TM, d), lambda i: (jnp.maximum(i - n_lat_tiles, 0), 0))]
    else:
        x_specs = [pl.BlockSpec((TM, d), lambda i: (i, 0))]
    assert n_tiles >= d_mix // LANES
    wo_blk = lambda i: jnp.minimum(i, d_mix // LANES - 1)

    def seg(i):
        return jnp.minimum(i // tiles_per_seq, n_seq)

    def rope_blk(i):
        return jnp.where(i < n_lat_tiles, i % tiles_per_seq, tiles_per_seq)

    layer = lambda i: (l, 0, 0)
    vec = lambda n: pl.BlockSpec((None, 1, n), layer)
    tab_spec = pl.BlockSpec((TM, LANES), lambda i: (rope_blk(i), 0))
    row_major = lambda w: (pl.BlockSpec((TM, w), lambda i: (i, 0)), jax.ShapeDtypeStruct((t, w), BF16))
    col_major = lambda w: (pl.BlockSpec((w, TM), lambda i: (0, i)), jax.ShapeDtypeStruct((w, t), BF16))
    outs = [row_major(A_HEADS * HEAD_DIM), row_major(A_KV * HEAD_DIM), row_major(A_KV * HEAD_DIM),
            row_major(B_HEADS * HEAD_DIM), row_major(B_KV * HEAD_DIM), col_major(B_KV * VT_EXT),
            row_major(C_HEADS * C_QK_PAD), row_major(C_HEADS * C_QK_PAD), col_major(C_HEADS * C_V),
            (pl.BlockSpec((LANES, d), lambda i: (wo_blk(i), 0)), jax.ShapeDtypeStruct((d_mix, d), BF16)),
            (pl.BlockSpec((2, MOD_ROWS, LANES), lambda i: (i, 0, 0)),
             jax.ShapeDtypeStruct((2 * n_tiles, MOD_ROWS, LANES), F32))]
    return pl.pallas_call(
        functools.partial(_qkv_kernel, n_lat_tiles=n_lat_tiles, split_x=split_x),
        grid=(n_tiles,),
        in_specs=x_specs + [
            pl.BlockSpec((None, None, MOD_ROWS, d), lambda i: (l, seg(i), 0, 0)),
            vec(d),
            pl.BlockSpec((None, d, d_in), layer, pipeline_mode=pl.Buffered(1)),
            tab_spec, tab_spec, tab_spec, tab_spec,
            vec(HEAD_DIM), vec(HEAD_DIM), vec(C_Q_LORA), vec(C_KV_LORA),
            pl.BlockSpec((None,) + wuq.shape[1:], layer, pipeline_mode=pl.Buffered(1)),
            pl.BlockSpec((None,) + wukv.shape[1:], layer, pipeline_mode=pl.Buffered(1)),
            pl.BlockSpec((None, LANES, d), lambda i: (l, wo_blk(i), 0)),
        ],
        out_specs=[o[0] for o in outs],
        out_shape=[o[1] for o in outs],
        compiler_params=pltpu.CompilerParams(
            dimension_semantics=("arbitrary",),
            vmem_limit_bytes=56 * MIB),
        name="qkv_prep",
    )(*xs, mods, gpre, win, *tabs, gqn, gkn, gcq, gckv, wuq, wukv, w_out)


def _attn_window_kernel(sink_ref, q_ref, kl_ref, vl_ref, kc_ref, vc_ref, o_ref, *,
                        layer, heads_per_group, dk, dv, scale, seq):
    g = pl.program_id(1)
    qi = pl.program_id(2)
    c2 = scale * LOG2E
    band = TQ + 2 * WINDOW
    heads = range(heads_per_group)
    sink2 = jnp.concatenate(
        [jnp.full((1, TQ), sink_ref[layer, g * heads_per_group + r] * LOG2E, F32) for r in heads], axis=1)
    for sub in range(SUBTILES_WIN):
        rows = slice(sub * TQ, (sub + 1) * TQ)
        q0 = (qi * SUBTILES_WIN + sub) * TQ
        q = jnp.concatenate([q_ref[rows, r * dk:(r + 1) * dk] for r in heads], axis=0)
        start = pl.multiple_of(jnp.clip(q0 - WINDOW, 0, seq - band), LANES)
        s_c = _dot_nt(kc_ref[...], q)
        s_l = _dot_nt(kl_ref[pl.ds(start, band), :], q)
        kpos = start + lax.broadcasted_iota(jnp.int32, (band, TQ), 0)
        qpos = q0 + lax.broadcasted_iota(jnp.int32, (band, TQ), 1)
        in_window = jnp.abs(kpos - qpos) <= WINDOW
        s_l = jnp.where(jnp.concatenate([in_window] * heads_per_group, axis=1), s_l, NEG)
        m = jnp.maximum(jnp.maximum(jnp.max(s_c, axis=0, keepdims=True), jnp.max(s_l, axis=0, keepdims=True)) * c2,
                        sink2)
        p_c = jnp.exp2(s_c * c2 - m)
        p_l = jnp.exp2(s_l * c2 - m)
        denom = (jnp.sum(p_c, axis=0, keepdims=True) + jnp.sum(p_l, axis=0, keepdims=True)
                 + jnp.exp2(sink2 - m))
        acc = _dot_tn(vc_ref[...], p_c.astype(BF16)) + _dot_tn(vl_ref[pl.ds(start, band), :], p_l.astype(BF16))
        o = acc / denom
        for r in heads:
            o_ref[rows, r * dv:(r + 1) * dv] = o[:, r * TQ:(r + 1) * TQ].T.astype(BF16)


def _attn_window_call(l, q, k, v, sink, *, n_groups, heads_per_group, dk, dv, scale, batch, seq, ctx_len):
    tq = SUBTILES_WIN * TQ
    n_q = seq // tq
    ctx_blk0 = batch * seq // ctx_len
    return pl.pallas_call(
        functools.partial(_attn_window_kernel, layer=l, heads_per_group=heads_per_group, dk=dk, dv=dv,
                          scale=scale, seq=seq),
        grid=(batch, n_groups, n_q),
        in_specs=[
            pl.BlockSpec(memory_space=pltpu.SMEM),
            pl.BlockSpec((tq, heads_per_group * dk), lambda b, g, qi: (b * n_q + qi, g)),
            pl.BlockSpec((seq, dk), lambda b, g, qi: (b, g)),
            pl.BlockSpec((seq, dv), lambda b, g, qi: (b, g)),
            pl.BlockSpec((ctx_len, dk), lambda b, g, qi: (ctx_blk0 + b, g)),
            pl.BlockSpec((ctx_len, dv), lambda b, g, qi: (ctx_blk0 + b, g)),
        ],
        out_specs=pl.BlockSpec((tq, heads_per_group * dv), lambda b, g, qi: (b * n_q + qi, g)),
        out_shape=jax.ShapeDtypeStruct((batch * seq, n_groups * heads_per_group * dv), BF16),
        compiler_params=pltpu.CompilerParams(
            dimension_semantics=("arbitrary", "arbitrary", "arbitrary"),
            vmem_limit_bytes=48 * MIB),
        name="attn_window",
    )(sink, q, k, v, k, v)


def _attn_dense_kernel(*refs, heads_per_group, dk, dv, scale, tq, shift_lane, cast_side_job, k2_col0):
    if cast_side_job:
        k2_ref, q_ref, kl_ref, vtl_ref, kc_ref, vtc_ref, w_ref, o_ref, wb_ref = refs
        wb_ref[...] = w_ref[...].astype(BF16)
    else:
        k2_ref, q_ref, kl_ref, vtl_ref, kc_ref, vtc_ref, o_ref = refs
    c2 = scale * LOG2E
    seq = kl_ref.shape[0]
    k2 = k2_ref[pl.program_id(0), k2_col0 + pl.program_id(1)]

    if heads_per_group > 1:
        q = jnp.concatenate([q_ref[:, r * dk:(r + 1) * dk] for r in range(heads_per_group)], axis=0)
    else:
        q = q_ref[...]
    qf = q.astype(F32)
    q2 = jnp.sum(qf * qf, axis=-1, keepdims=True)
    bound = jnp.sqrt(q2 * k2) * BOUND_SLACK
    bound_is_safe = jnp.max(bound) * c2 <= SAFE_SHIFT_LOG2

    def finish(acc, denom):
        o = acc / denom
        for r in range(heads_per_group):
            o_ref[:, r * dv:(r + 1) * dv] = o[:, r * tq:(r + 1) * tq].T.astype(BF16)

    @pl.when(bound_is_safe)
    def _():
        if shift_lane is None:
            shift = bound.T * c2
            p_c = jnp.exp2(_dot_nt(kc_ref[...], q) * c2 - shift)
            p_l = jnp.exp2(_dot_nt(kl_ref[...], q) * c2 - shift)
        else:
            lane = lax.broadcasted_iota(jnp.int32, q.shape, 1)
            q_shifted = jnp.where(lane == shift_lane, -bound, qf).astype(BF16)
            p_c = jnp.exp2(_dot_nt(kc_ref[...], q_shifted) * c2)
            p_l = jnp.exp2(_dot_nt(kl_ref[...], q_shifted) * c2)
        acc = _dot(vtc_ref[...], p_c.astype(BF16)) + _dot(vtl_ref[...], p_l.astype(BF16))
        if vtl_ref.shape[0] > dv:
            finish(acc[:dv], acc[dv:dv + 1])
        else:
            finish(acc, jnp.sum(p_c, axis=0, keepdims=True) + jnp.sum(p_l, axis=0, keepdims=True))

    @pl.when(jnp.logical_not(bound_is_safe))
    def _():
        chunks = [(lambda: kc_ref[...], lambda: vtc_ref[0:dv, :])]
        for j in range(seq // TK):
            chunks.append((functools.partial(lambda j: kl_ref[j * TK:(j + 1) * TK, :], j),
                           functools.partial(lambda j: vtl_ref[0:dv, j * TK:(j + 1) * TK], j)))
        n = len(chunks)
        s_next = _dot_nt(chunks[0][0](), q)
        m = denom = acc = None
        for j in range(n):
            s = s_next
            if j + 1 < n:
                s_next = _dot_nt(chunks[j + 1][0](), q)
            m_j = jnp.max(s, axis=0, keepdims=True) * c2
            m_new = m_j if j == 0 else jnp.maximum(m, m_j)
            p = jnp.exp2(s * c2 - m_new)
            pv = _dot(chunks[j][1](), p.astype(BF16))
            if j == 0:
                denom = jnp.sum(p, axis=0, keepdims=True)
                acc = pv
            else:
                alpha = jnp.exp2(m - m_new)
                denom = denom * alpha + jnp.sum(p, axis=0, keepdims=True)
                acc = acc * alpha + pv
            m = m_new
        finish(acc, denom)


def _attn_dense_call(k2, k2_col0, q, k, vt, *, n_groups, heads_per_group, dk, dv, scale, batch, seq, ctx_len, tq,
                     shift_lane=None, cast_weight=None):
    n_q = seq // tq
    ctx_blk0 = batch * seq // ctx_len
    vt_rows = vt.shape[0] // n_groups
    in_specs = [
        pl.BlockSpec(memory_space=pltpu.SMEM),
        pl.BlockSpec((tq, heads_per_group * dk), lambda b, g, qi: (b * n_q + qi, g)),
        pl.BlockSpec((seq, dk), lambda b, g, qi: (b, g)),
        pl.BlockSpec((vt_rows, seq), lambda b, g, qi: (g, b)),
        pl.BlockSpec((ctx_len, dk), lambda b, g, qi: (ctx_blk0 + b, g)),
        pl.BlockSpec((vt_rows, ctx_len), lambda b, g, qi: (g, ctx_blk0 + b)),
    ]
    args = [k2, q, k, vt, k, vt]
    out_specs = [pl.BlockSpec((tq, heads_per_group * dv), lambda b, g, qi: (b * n_q + qi, g))]
    out_shape = [jax.ShapeDtypeStruct((batch * seq, n_groups * heads_per_group * dv), BF16)]
    if cast_weight is not None:
        l, w = cast_weight
        n_steps = batch * n_groups * n_q
        rows, cols = w.shape[1] // n_steps, w.shape[2]
        assert rows * n_steps == w.shape[1] and rows % 16 == 0
        step = lambda b, g, qi: (b * n_groups + g) * n_q + qi
        in_specs.append(pl.BlockSpec((None, rows, cols), lambda b, g, qi: (l, step(b, g, qi), 0)))
        args.append(w)
        out_specs.append(pl.BlockSpec((rows, cols), lambda b, g, qi: (step(b, g, qi), 0)))
        out_shape.append(jax.ShapeDtypeStruct(w.shape[1:], BF16))
    outs = pl.pallas_call(
        functools.partial(_attn_dense_kernel, heads_per_group=heads_per_group, dk=dk, dv=dv, scale=scale, tq=tq,
                          shift_lane=shift_lane, cast_side_job=cast_weight is not None, k2_col0=k2_col0),
        grid=(batch, n_groups, n_q),
        in_specs=in_specs,
        out_specs=out_specs,
        out_shape=out_shape,
        compiler_params=pltpu.CompilerParams(
            dimension_semantics=("arbitrary", "arbitrary", "arbitrary"),
            vmem_limit_bytes=48 * MIB),
        name=f"attn_dense_dk{dk}",
    )(*args)
    return outs if cast_weight is not None else outs[0]


def _attn_ctx_kernel(sink_ref, qa_ref, ka_ref, va_ref, qb_ref, kb_ref, vtb_ref, qc_ref, kc_ref, vtc_ref,
                     oa_ref, ob_ref, oc_ref, *, layer, scale_ab, scale_c):
    def head(q, k, scale, pv, sink2=None):
        c2 = scale * LOG2E
        s = _dot_nt(q, k)
        m = jnp.max(s, axis=-1, keepdims=True) * c2
        if sink2 is not None:
            m = jnp.maximum(m, sink2)
        p = jnp.exp2(s * c2 - m)
        denom = jnp.sum(p, axis=-1, keepdims=True)
        if sink2 is not None:
            denom = denom + jnp.exp2(sink2 - m)
        return (pv(p.astype(BF16)) / denom).astype(BF16)

    for hd in range(A_HEADS):
        g = hd // (A_HEADS // A_KV)
        oa_ref[:, _col(hd)] = head(qa_ref[:, _col(hd)], ka_ref[:, _col(g)], scale_ab,
                                   lambda p: _dot(p, va_ref[:, _col(g)]), sink_ref[layer, hd] * LOG2E)
    for hd in range(B_HEADS):
        g = hd // (B_HEADS // B_KV)
        ob_ref[:, _col(hd)] = head(qb_ref[:, _col(hd)], kb_ref[:, _col(g)], scale_ab,
                                   lambda p: _dot_nt(p, vtb_ref[g * VT_EXT:g * VT_EXT + HEAD_DIM, :]))
    for hd in range(C_HEADS):
        qk = slice(hd * C_QK_PAD, (hd + 1) * C_QK_PAD)
        oc_ref[:, _col(hd)] = head(qc_ref[:, qk], kc_ref[:, qk], scale_c,
                                   lambda p: _dot_nt(p, vtc_ref[_col(hd), :]))


def _attn_ctx_call(l, sink, qa, ka, va, qb, kb, vtb, qc, kc, vtc, *, scale_ab, scale_c, batch, seq, ctx_len):
    blk0 = batch * seq // ctx_len
    rows = lambda a: pl.BlockSpec((ctx_len, a.shape[1]), lambda b: (blk0 + b, 0))
    cols = lambda a: pl.BlockSpec((a.shape[0], ctx_len), lambda b: (0, blk0 + b))
    out = lambda w: (pl.BlockSpec((ctx_len, w), lambda b: (b, 0)),
                     jax.ShapeDtypeStruct((batch * ctx_len, w), BF16))
    outs = [out(A_HEADS * HEAD_DIM), out(B_HEADS * HEAD_DIM), out(C_HEADS * C_V)]
    return pl.pallas_call(
        functools.partial(_attn_ctx_kernel, layer=l, scale_ab=scale_ab, scale_c=scale_c),
        grid=(batch,),
        in_specs=[pl.BlockSpec(memory_space=pltpu.SMEM), rows(qa), rows(ka), rows(va),
                  rows(qb), rows(kb), cols(vtb), rows(qc), rows(kc), cols(vtc)],
        out_specs=[o[0] for o in outs],
        out_shape=[o[1] for o in outs],
        compiler_params=pltpu.CompilerParams(
            dimension_semantics=("arbitrary",),
            vmem_limit_bytes=32 * MIB),
        name="attn_ctx",
    )(sink, qa, ka, va, qb, kb, vtb, qc, kc, vtc)


def _outproj_kernel(*refs, n_lat_tiles, has_ctx, split_x):
    lat_refs, refs = refs[:3], refs[3:]
    if has_ctx:
        ctx_refs, refs = refs[:3], refs[3:]
    x_refs, refs = (refs[:2], refs[2:]) if split_x else (refs[:1], refs[1:])
    mod_ref, gpost_ref, gffn_ref, wo_ref, xo_ref, h_ref = refs
    is_ctx_tile = pl.program_id(0) >= n_lat_tiles
    m = mod_ref[...]
    for half in range(2):
        rows = slice(half * (TM // 2), (half + 1) * (TM // 2))
        o, w0 = None, 0
        for g, lat_ref in enumerate(lat_refs):
            og = lat_ref[rows, :]
            if has_ctx:
                og = jnp.where(is_ctx_tile, ctx_refs[g][rows, :], og)
            width = lat_ref.shape[1]
            part = _dot(og, wo_ref[w0:w0 + width, :])
            o = part if o is None else o + part
            w0 += width
        x = x_refs[0][rows, :]
        if split_x:
            x = jnp.where(is_ctx_tile, x_refs[1][rows, :], x)
        x_new = x + m[2:3] * _rmsnorm(o, gpost_ref[...])
        xo_ref[rows, :] = x_new
        h_ref[rows, :] = (_rmsnorm(x_new, gffn_ref[...]) * (1.0 + m[4:5]) + m[3:4]).astype(BF16)


def _outproj_call(l, lat_outs, ctx_outs, xs, mods, gpost, gffn, wo, *, n_tiles, n_lat_tiles, tiles_per_seq, n_seq):
    t = n_tiles * TM
    d = xs[0].shape[1]
    seg = lambda i: jnp.minimum(i // tiles_per_seq, n_seq)
    layer = lambda i: (l, 0, 0)
    has_ctx = ctx_outs is not None
    split_x = len(xs) == 2
    lat_rows = lambda a: pl.BlockSpec((TM, a.shape[1]), lambda i: (jnp.minimum(i, n_lat_tiles - 1), 0))
    ctx_rows = lambda a: pl.BlockSpec((TM, a.shape[1]), lambda i: (jnp.maximum(i - n_lat_tiles, 0), 0))
    in_specs = [lat_rows(a) for a in lat_outs]
    args = list(lat_outs)
    if has_ctx:
        in_specs += [ctx_rows(a) for a in ctx_outs]
        args += list(ctx_outs)
    in_specs += [lat_rows(xs[0]), ctx_rows(xs[1])] if split_x else [pl.BlockSpec((TM, d), lambda i: (i, 0))]
    args += list(xs)
    in_specs += [
        pl.BlockSpec((None, None, MOD_ROWS, d), lambda i: (l, seg(i), 0, 0)),
        pl.BlockSpec((None, 1, d), layer),
        pl.BlockSpec((None, 1, d), layer),
        pl.BlockSpec(wo.shape, lambda i: (0, 0), pipeline_mode=pl.Buffered(1)),
    ]
    args += [mods, gpost, gffn, wo]
    return pl.pallas_call(
        functools.partial(_outproj_kernel, n_lat_tiles=n_lat_tiles, has_ctx=has_ctx, split_x=split_x),
        grid=(n_tiles,),
        in_specs=in_specs,
        out_specs=[pl.BlockSpec((TM, d), lambda i: (i, 0)), pl.BlockSpec((TM, d), lambda i: (i, 0))],
        out_shape=[jax.ShapeDtypeStruct((t, d), F32), jax.ShapeDtypeStruct((t, d), BF16)],
        compiler_params=pltpu.CompilerParams(
            dimension_semantics=("arbitrary",),
            vmem_limit_bytes=48 * MIB),
        name="outproj",
    )(*args)


def _ffn_up_kernel(h_ref, w1_ref, w3_ref, t_ref, w1_scr, w3_scr):
    @pl.when(pl.program_id(1) == 0)
    def _():
        w1_scr[...] = w1_ref[...].astype(BF16)
        w3_scr[...] = w3_ref[...].astype(BF16)

    h = h_ref[...]
    a = _dot(h, w1_scr[...])
    b = _dot(h, w3_scr[...])
    t_ref[...] = (a * jax.nn.sigmoid(a) * b).astype(BF16)


def _ffn_up_call(l, h, w1, w3):
    t, d = h.shape
    f = w1.shape[2]
    tm = t // N_UP_TILES
    return pl.pallas_call(
        _ffn_up_kernel,
        grid=(f // TF, N_UP_TILES),
        in_specs=[
            pl.BlockSpec((tm, d), lambda j, i: (i, 0)),
            pl.BlockSpec((None, d, TF), lambda j, i: (l, 0, j)),
            pl.BlockSpec((None, d, TF), lambda j, i: (l, 0, j)),
        ],
        out_specs=pl.BlockSpec((tm, TF), lambda j, i: (i, j)),
        out_shape=jax.ShapeDtypeStruct((t, f), BF16),
        scratch_shapes=[pltpu.VMEM((d, TF), BF16), pltpu.VMEM((d, TF), BF16)],
        compiler_params=pltpu.CompilerParams(
            dimension_semantics=("arbitrary", "arbitrary"),
            vmem_limit_bytes=48 * MIB),
        name="ffn_up",
    )(h, w1, w3)


def _ffn_down_kernel(t_ref, x_ref, mod_ref, gpost_ref, w2_ref, o_ref):
    y = _dot(t_ref[...], w2_ref[...])
    m = mod_ref[...]
    o_ref[...] = x_ref[...] + m[5:6] * _rmsnorm(y, gpost_ref[...])


def _ffn_down_call(l, tt, xa, mods, gpost, w2, *, tiles_per_seq, n_seq):
    t, d = xa.shape
    f = w2.shape[0]
    seg = lambda i: jnp.minimum(i // tiles_per_seq, n_seq)
    layer = lambda i: (l, 0, 0)
    return pl.pallas_call(
        _ffn_down_kernel,
        grid=(t // TM,),
        in_specs=[
            pl.BlockSpec((TM, f), lambda i: (i, 0)),
            pl.BlockSpec((TM, d), lambda i: (i, 0)),
            pl.BlockSpec((None, None, MOD_ROWS, d), lambda i: (l, seg(i), 0, 0)),
            pl.BlockSpec((None, 1, d), layer),
            pl.BlockSpec((f, d), lambda i: (0, 0), pipeline_mode=pl.Buffered(1)),
        ],
        out_specs=pl.BlockSpec((TM, d), lambda i: (i, 0)),
        out_shape=jax.ShapeDtypeStruct((t, d), F32),
        compiler_params=pltpu.CompilerParams(
            dimension_semantics=("arbitrary",),
            vmem_limit_bytes=60 * MIB),
        name="ffn_down",
    )(tt, xa, mods, gpost, w2)


def _rope_tables(seq):
    pos = np.arange(seq)
    row = (pos // GRID_W).astype(np.float64)[:, None]
    colp = (pos % GRID_W).astype(np.float64)[:, None]

    def cs(n):
        freqs = np.power(np.float64(ROPE_THETA), -np.arange(n, dtype=np.float64) / n)[None, :]
        return (np.cos(row * freqs), np.sin(row * freqs), np.cos(colp * freqs), np.sin(colp * freqs))

    cr, sr, cc, sc = cs(HEAD_DIM // 4)
    cos128 = np.concatenate([cr, cr, cc, cc], axis=1)
    sin128 = np.concatenate([-sr, sr, -sc, sc], axis=1)
    cr, sr, cc, sc = cs(C_ROPE // 4)
    pad1 = np.ones((seq, LANES - C_ROPE), np.float32)
    pad0 = np.zeros((seq, LANES - C_ROPE), np.float32)
    cos64 = np.concatenate([cr, cr, cc, cc, pad1], axis=1)
    sin64 = np.concatenate([-sr, sr, -sc, sc, pad0], axis=1)
    ident_c = np.ones((TM, LANES), np.float32)
    ident_s = np.zeros((TM, LANES), np.float32)
    return tuple(jnp.asarray(np.concatenate([tab, ident], axis=0).astype(np.float32))
                 for tab, ident in ((cos128, ident_c), (sin128, ident_s), (cos64, ident_c), (sin64, ident_s)))


def kernel(x, c, ctx, c_ctx, w_mod, b_mod, g_mix_pre, g_mix_post, g_ffn_pre, g_ffn_post, w_in, sink_a,
           g_qn_b, g_kn_b, g_cq, w_uq, g_ckv, w_ukv, w_out, w_ffn1, w_ffn3, w_ffn2):
    batch, seq, d = x.shape
    ctx_len = ctx.shape[1]
    depth = w_mod.shape[0]
    n_lat, n_ctx = batch * seq, batch * ctx_len
    assert seq % TM == 0 and n_ctx % TM == 0 and seq % ctx_len == 0 and ctx_len % TQ == 0
    assert seq % TQ_GQA == 0 and seq % TQ_MLA == 0 and seq % TK == 0 and ctx_len % LANES == 0
    assert batch + 1 <= MOD_ROWS and seq % GRID_W == 0 and seq % (SUBTILES_WIN * TQ) == 0
    assert n_lat % (16 * N_UP_TILES) == 0 and (n_lat + n_ctx) % (16 * N_UP_TILES) == 0
    assert ctx_len == TM // 2
    tiles_per_seq = seq // TM
    n_lat_tiles = n_lat // TM

    cv = jnp.zeros((MOD_ROWS, d), F32).at[:batch].set(c).at[batch].set(c_ctx)
    mod_all = _mod_call(cv, w_mod, b_mod)
    mods = mod_all[:, :batch + 1].reshape(depth, batch + 1, 6, d)
    mods = jnp.pad(mods, ((0, 0), (0, 0), (0, MOD_ROWS - 6), (0, 0)))

    tabs = _rope_tables(seq)

    win = w_in.astype(BF16)
    wuq = w_uq.reshape(depth, C_Q_LORA, C_HEADS, C_NOPE + C_ROPE)
    wuq = jnp.pad(wuq, ((0, 0), (0, 0), (0, 0), (0, C_QK_PAD - C_NOPE - C_ROPE)))
    wuq = wuq.reshape(depth, C_Q_LORA, C_HEADS * C_QK_PAD).astype(BF16)
    wukv = w_ukv.reshape(depth, C_KV_LORA, C_HEADS, 2, C_NOPE).transpose(0, 1, 3, 2, 4)
    wukv = wukv.reshape(depth, C_KV_LORA, 2 * C_HEADS * C_NOPE).astype(BF16)
    vecs = lambda v: v.reshape(depth, 1, -1)
    g_mix_pre, g_mix_post, g_ffn_pre, g_ffn_post = map(vecs, (g_mix_pre, g_mix_post, g_ffn_pre, g_ffn_post))
    g_qn_b, g_kn_b, g_cq, g_ckv = map(vecs, (g_qn_b, g_kn_b, g_cq, g_ckv))

    xs = (x.reshape(n_lat, d), ctx.reshape(n_ctx, d))
    sc_ab = 1.0 / math.sqrt(HEAD_DIM)
    sc_c = 1.0 / math.sqrt(C_NOPE + C_ROPE)
    dims = dict(batch=batch, seq=seq, ctx_len=ctx_len)

    for l in range(depth):
        last = l == depth - 1
        qa, ka, va, qb, kb, vtb, qc, kc, vtc, wo, k2_halves = _qkv_call(
            l, xs, mods, g_mix_pre, win, tabs, g_qn_b, g_kn_b, g_cq, g_ckv, wuq, wukv, w_out,
            n_lat_tiles=n_lat_tiles, tiles_per_seq=tiles_per_seq, n_seq=batch)
        k2_halves = k2_halves[:, 0, :]
        k2 = jnp.maximum(jnp.max(k2_halves[:2 * n_lat_tiles].reshape(batch, -1, LANES), axis=1),
                         k2_halves[2 * n_lat_tiles:])
        oa = _attn_window_call(l, qa, ka, va, sink_a, n_groups=A_KV, heads_per_group=A_HEADS // A_KV,
                               dk=HEAD_DIM, dv=HEAD_DIM, scale=sc_ab, **dims)
        ob = _attn_dense_call(k2, 0, qb, kb, vtb, n_groups=B_KV, heads_per_group=B_HEADS // B_KV, dk=HEAD_DIM,
                              dv=HEAD_DIM, scale=sc_ab, tq=TQ_GQA, **dims)
        oc, w2 = _attn_dense_call(k2, B_KV, qc, kc, vtc, n_groups=C_HEADS, heads_per_group=1, dk=C_QK_PAD,
                                  dv=C_V, scale=sc_c, tq=TQ_MLA, shift_lane=MLA_SHIFT_LANE,
                                  cast_weight=(l, w_ffn2), **dims)
        ctx_outs = None if last else _attn_ctx_call(l, sink_a, qa, ka, va, qb, kb, vtb, qc, kc, vtc,
                                                    scale_ab=sc_ab, scale_c=sc_c, **dims)
        n_tiles = n_lat_tiles if last else n_lat_tiles + n_ctx // TM
        xa, h = _outproj_call(l, (oa, ob, oc), ctx_outs, xs, mods, g_mix_post, g_ffn_pre, wo, n_tiles=n_tiles,
                              n_lat_tiles=n_lat_tiles, tiles_per_seq=tiles_per_seq, n_seq=batch)
        tt = _ffn_up_call(l, h, w_ffn1, w_ffn3)
        xa = _ffn_down_call(l, tt, xa, mods, g_ffn_post, w2, tiles_per_seq=tiles_per_seq, n_seq=batch)
        xs = (xa,)
    return xa.reshape(batch, seq, d)
```

```python
import functools
import math

import jax
import jax.numpy as jnp
import numpy as np
from jax import lax
from jax.experimental import pallas as pl
from jax.experimental.pallas import tpu as pltpu

GRID_W = 64
HEAD_DIM = 128
A_HEADS, A_KV = 6, 2
B_HEADS, B_KV = 6, 2
WINDOW = 128
C_HEADS = 4
C_Q_LORA, C_KV_LORA = 512, 256
C_NOPE, C_ROPE, C_V = 128, 64, 128
C_QK_PAD = 256
MLA_SHIFT_LANE = C_NOPE + C_ROPE
VT_EXT = HEAD_DIM + 16
ROPE_THETA = 10000.0
EPS = 1e-6
NEG = -1e30
LOG2E = math.log2(math.e)
SAFE_SHIFT_LOG2 = 50.0
BOUND_SLACK = 1.001

LANES = 128
MOD_ROWS = 8
TM = 512
TQ = 256
SUBTILES_WIN = 8
TQ_GQA = 512
TQ_MLA = 1024
TK = 512
TF = 512
N_UP_TILES = 8
TN_MOD = 1024
MIB = 1024 * 1024

F32 = jnp.float32
BF16 = jnp.bfloat16


def _rmsnorm(x, g):
    return x * lax.rsqrt(jnp.mean(x * x, axis=-1, keepdims=True) + EPS) * g


def _dot(a, b):
    return jnp.dot(a, b, preferred_element_type=F32)


def _dot_nt(a, b):
    return lax.dot_general(a, b, (((1,), (1,)), ((), ())), preferred_element_type=F32)


def _dot_tn(a, b):
    return lax.dot_general(a, b, (((0,), (0,)), ((), ())), preferred_element_type=F32)


def _col(i):
    return slice(i * LANES, (i + 1) * LANES)


def _mod_kernel(c_ref, w_ref, b_ref, o_ref):
    cv = c_ref[...]
    s = cv * jax.nn.sigmoid(cv)
    o_ref[...] = _dot(s.astype(BF16), w_ref[...].astype(BF16)) + b_ref[...]


def _mod_call(cv, w_mod, b_mod):
    depth, d, n = w_mod.shape
    return pl.pallas_call(
        _mod_kernel,
        grid=(depth, n // TN_MOD),
        in_specs=[
            pl.BlockSpec((MOD_ROWS, d), lambda l, j: (0, 0)),
            pl.BlockSpec((None, d, TN_MOD), lambda l, j: (l, 0, j)),
            pl.BlockSpec((None, 1, TN_MOD), lambda l, j: (l, 0, j)),
        ],
        out_specs=pl.BlockSpec((None, MOD_ROWS, TN_MOD), lambda l, j: (l, 0, j)),
        out_shape=jax.ShapeDtypeStruct((depth, MOD_ROWS, n), F32),
        compiler_params=pltpu.CompilerParams(
            dimension_semantics=("arbitrary", "arbitrary"),
            vmem_limit_bytes=40 * MIB),
        name="adaln_mod",
    )(cv, w_mod, b_mod.reshape(depth, 1, n))


def _qkv_kernel(*refs, n_lat_tiles, split_x):
    x_refs, refs = (refs[:2], refs[2:]) if split_x else (refs[:1], refs[1:])
    (mod_ref, gpre_ref, win_ref, cos_ref, sin_ref, cos64_ref, sin64_ref,
     gqn_ref, gkn_ref, gcq_ref, gckv_ref, wuq_ref, wukv_ref, wo_ref,
     qa_ref, ka_ref, va_ref, qb_ref, kb_ref, vtb_ref, qc_ref, kc_ref, vtc_ref, wob_ref, k2_ref) = refs
    wob_ref[...] = wo_ref[...].astype(BF16)
    m = mod_ref[...]
    tr = TM // 2
    lane = lax.broadcasted_iota(jnp.int32, (tr, LANES), 1)
    low32 = (lane & 32) == 0
    low16 = (lane & 16) == 0
    stat_lane = lax.broadcasted_iota(jnp.int32, (MOD_ROWS, LANES), 1)

    def max_sq_norm(*parts):
        sq = None
        for part in parts:
            pf = part.astype(F32)
            s = jnp.sum(pf * pf, axis=-1, keepdims=True)
            sq = s if sq is None else sq + s
        return jnp.max(sq, axis=0, keepdims=True)

    for half in range(2):
        rows = slice(half * tr, (half + 1) * tr)
        x = x_refs[0][rows, :]
        if split_x:
            x = jnp.where(pl.program_id(0) >= n_lat_tiles, x_refs[1][rows, :], x)
        h = (_rmsnorm(x, gpre_ref[...]) * (1.0 + m[1:2]) + m[0:1]).astype(BF16)
        p = _dot(h, win_ref[...])
        cos, sin = cos_ref[rows, :], sin_ref[rows, :]
        cos64, sin64 = cos64_ref[rows, :], sin64_ref[rows, :]

        def rope128(t):
            partner = jnp.where(low32, pltpu.roll(t, LANES - 32, 1), pltpu.roll(t, 32, 1))
            return t * cos + partner * sin

        def rope64(t):
            partner = jnp.where(low16, pltpu.roll(t, LANES - 16, 1), pltpu.roll(t, 16, 1))
            return t * cos64 + partner * sin64

        off = 0
        for i in range(A_HEADS):
            qa_ref[rows, _col(i)] = rope128(p[:, _col(off + i)]).astype(BF16)
        off += A_HEADS
        k2_stats = jnp.zeros((MOD_ROWS, LANES), F32)
        for i in range(A_KV):
            ka = rope128(p[:, _col(off + i)]).astype(BF16)
            ka_ref[rows, _col(i)] = ka
            k2_stats = jnp.where(stat_lane == B_KV + C_HEADS + i, max_sq_norm(ka), k2_stats)
        off += A_KV
        for i in range(A_KV):
            va_ref[rows, _col(i)] = p[:, _col(off + i)].astype(BF16)
        off += A_KV

        gqn, gkn = gqn_ref[...], gkn_ref[...]
        for i in range(B_HEADS):
            qb_ref[rows, _col(i)] = rope128(_rmsnorm(p[:, _col(off + i)], gqn)).astype(BF16)
        off += B_HEADS
        for i in range(B_KV):
            kb = rope128(_rmsnorm(p[:, _col(off + i)], gkn)).astype(BF16)
            kb_ref[rows, _col(i)] = kb
            k2_stats = jnp.where(stat_lane == i, max_sq_norm(kb), k2_stats)
        off += B_KV
        for i in range(B_KV):
            vtb_ref[i * VT_EXT:i * VT_EXT + HEAD_DIM, rows] = p[:, _col(off + i)].T.astype(BF16)
            vtb_ref[i * VT_EXT + HEAD_DIM:(i + 1) * VT_EXT, rows] = jnp.ones((VT_EXT - HEAD_DIM, tr), BF16)
        off += B_KV

        c0 = off * LANES
        cq = _rmsnorm(p[:, c0:c0 + C_Q_LORA], gcq_ref[...])
        q = _dot(cq.astype(BF16), wuq_ref[...])
        c0 += C_Q_LORA
        ckv = _rmsnorm(p[:, c0:c0 + C_KV_LORA], gckv_ref[...])
        kv = _dot(ckv.astype(BF16), wukv_ref[...])
        c0 += C_KV_LORA
        ckr = jnp.concatenate([p[:, c0:c0 + C_ROPE], jnp.zeros((tr, LANES - C_ROPE), F32)], axis=1)
        krope = jnp.where(lane == C_ROPE, 1.0, rope64(ckr)).astype(BF16)
        for i in range(C_HEADS):
            qc_ref[rows, _col(2 * i)] = q[:, _col(2 * i)].astype(BF16)
            qc_ref[rows, _col(2 * i + 1)] = rope64(q[:, _col(2 * i + 1)]).astype(BF16)
            knope = kv[:, _col(i)].astype(BF16)
            kc_ref[rows, _col(2 * i)] = knope
            kc_ref[rows, _col(2 * i + 1)] = krope
            k2_stats = jnp.where(stat_lane == B_KV + i, max_sq_norm(knope, krope), k2_stats)
            vtc_ref[_col(i), rows] = kv[:, _col(C_HEADS + i)].T.astype(BF16)
        k2_ref[half] = k2_stats


def _qkv_call(l, xs, mods, gpre, win, tabs, gqn, gkn, gcq, gckv, wuq, wukv, w_out, *,
              n_lat_tiles, tiles_per_seq, n_seq):
    split_x = len(xs) == 2
    t, d = sum(a.shape[0] for a in xs), xs[0].shape[1]
    n_tiles = t // TM
    d_in = win.shape[2]
    d_mix = w_out.shape[1]
    if split_x:
        x_specs = [pl.BlockSpec((TM, d), lambda i: (jnp.minimum(i, n_lat_tiles - 1), 0)),
                   pl.BlockSpec((TM, d), lambda i: (jnp.maximum(i - n_lat_tiles, 0), 0))]
    else:
        x_specs = [pl.BlockSpec((TM, d), lambda i: (i, 0))]
    assert n_tiles >= d_mix // LANES
    wo_blk = lambda i: jnp.minimum(i, d_mix // LANES - 1)

    def seg(i):
        return jnp.minimum(i // tiles_per_seq, n_seq)

    def rope_blk(i):
        return jnp.where(i < n_lat_tiles, i % tiles_per_seq, tiles_per_seq)

    layer = lambda i: (l, 0, 0)
    vec = lambda n: pl.BlockSpec((None, 1, n), layer)
    tab_spec = pl.BlockSpec((TM, LANES), lambda i: (rope_blk(i), 0))
    row_major = lambda w: (pl.BlockSpec((TM, w), lambda i: (i, 0)), jax.ShapeDtypeStruct((t, w), BF16))
    col_major = lambda w: (pl.BlockSpec((w, TM), lambda i: (0, i)), jax.ShapeDtypeStruct((w, t), BF16))
    outs = [row_major(A_HEADS * HEAD_DIM), row_major(A_KV * HEAD_DIM), row_major(A_KV * HEAD_DIM),
            row_major(B_HEADS * HEAD_DIM), row_major(B_KV * HEAD_DIM), col_major(B_KV * VT_EXT),
            row_major(C_HEADS * C_QK_PAD), row_major(C_HEADS * C_QK_PAD), col_major(C_HEADS * C_V),
            (pl.BlockSpec((LANES, d), lambda i: (wo_blk(i), 0)), jax.ShapeDtypeStruct((d_mix, d), BF16)),
            (pl.BlockSpec((2, MOD_ROWS, LANES), lambda i: (i, 0, 0)),
             jax.ShapeDtypeStruct((2 * n_tiles, MOD_ROWS, LANES), F32))]
    return pl.pallas_call(
        functools.partial(_qkv_kernel, n_lat_tiles=n_lat_tiles, split_x=split_x),
        grid=(n_tiles,),
        in_specs=x_specs + [
            pl.BlockSpec((None, None, MOD_ROWS, d), lambda i: (l, seg(i), 0, 0)),
            vec(d),
            pl.BlockSpec((None, d, d_in), layer, pipeline_mode=pl.Buffered(1)),
            tab_spec, tab_spec, tab_spec, tab_spec,
            vec(HEAD_DIM), vec(HEAD_DIM), vec(C_Q_LORA), vec(C_KV_LORA),
            pl.BlockSpec((None,) + wuq.shape[1:], layer, pipeline_mode=pl.Buffered(1)),
            pl.BlockSpec((None,) + wukv.shape[1:], layer, pipeline_mode=pl.Buffered(1)),
            pl.BlockSpec((None, LANES, d), lambda i: (l, wo_blk(i), 0)),
        ],
        out_specs=[o[0] for o in outs],
        out_shape=[o[1] for o in outs],
        compiler_params=pltpu.CompilerParams(
            dimension_semantics=("arbitrary",),
            vmem_limit_bytes=56 * MIB),
        name="qkv_prep",
    )(*xs, mods, gpre, win, *tabs, gqn, gkn, gcq, gckv, wuq, wukv, w_out)


def _attn_window_kernel(sink_ref, k2_ref, q_ref, kl_ref, vl_ref, kc_ref, vc_ref, o_ref, *,
                        layer, heads_per_group, dk, dv, scale, seq, k2_col0):
    g = pl.program_id(1)
    qi = pl.program_id(2)
    c2 = scale * LOG2E
    band = TQ + 2 * WINDOW
    heads = range(heads_per_group)
    sink2 = jnp.concatenate(
        [jnp.full((1, TQ), sink_ref[layer, g * heads_per_group + r] * LOG2E, F32) for r in heads], axis=1)
    k2 = k2_ref[pl.program_id(0), k2_col0 + g]
    q2_max = None
    for r in heads:
        qf = q_ref[:, r * dk:(r + 1) * dk].astype(F32)
        q2_r = jnp.max(jnp.sum(qf * qf, axis=-1, keepdims=True))
        q2_max = q2_r if q2_max is None else jnp.maximum(q2_max, q2_r)
    bound_is_safe = jnp.sqrt(q2_max * k2) * (BOUND_SLACK * c2) <= SAFE_SHIFT_LOG2

    @pl.when(bound_is_safe)
    def _():
        _window_subtiles(sink2, k2, q_ref, kl_ref, vl_ref, kc_ref, vc_ref, o_ref, qi, heads_per_group, dk, dv,
                         c2, seq, use_bound=True)

    @pl.when(jnp.logical_not(bound_is_safe))
    def _():
        _window_subtiles(sink2, k2, q_ref, kl_ref, vl_ref, kc_ref, vc_ref, o_ref, qi, heads_per_group, dk, dv,
                         c2, seq, use_bound=False)


def _window_subtiles(sink2, k2, q_ref, kl_ref, vl_ref, kc_ref, vc_ref, o_ref, qi, heads_per_group, dk, dv,
                     c2, seq, *, use_bound):
    band = TQ + 2 * WINDOW
    heads = range(heads_per_group)
    for sub in range(SUBTILES_WIN):
        rows = slice(sub * TQ, (sub + 1) * TQ)
        q0 = (qi * SUBTILES_WIN + sub) * TQ
        q = jnp.concatenate([q_ref[rows, r * dk:(r + 1) * dk] for r in heads], axis=0)
        start = pl.multiple_of(jnp.clip(q0 - WINDOW, 0, seq - band), LANES)
        s_c = _dot_nt(kc_ref[...], q)
        s_l = _dot_nt(kl_ref[pl.ds(start, band), :], q)
        kpos = start + lax.broadcasted_iota(jnp.int32, (band, TQ), 0)
        qpos = q0 + lax.broadcasted_iota(jnp.int32, (band, TQ), 1)
        in_window = jnp.abs(kpos - qpos) <= WINDOW
        s_l = jnp.where(jnp.concatenate([in_window] * heads_per_group, axis=1), s_l, NEG)
        if use_bound:
            qf = q.astype(F32)
            bound = jnp.sqrt(jnp.sum(qf * qf, axis=-1, keepdims=True) * k2) * (BOUND_SLACK * c2)
            m = jnp.maximum(bound.T, sink2)
        else:
            m = jnp.maximum(
                jnp.maximum(jnp.max(s_c, axis=0, keepdims=True), jnp.max(s_l, axis=0, keepdims=True)) * c2, sink2)
        p_c = jnp.exp2(s_c * c2 - m)
        p_l = jnp.exp2(s_l * c2 - m)
        denom = (jnp.sum(p_c, axis=0, keepdims=True) + jnp.sum(p_l, axis=0, keepdims=True)
                 + jnp.exp2(sink2 - m))
        acc = _dot_tn(vc_ref[...], p_c.astype(BF16)) + _dot_tn(vl_ref[pl.ds(start, band), :], p_l.astype(BF16))
        o = acc / denom
        for r in heads:
            o_ref[rows, r * dv:(r + 1) * dv] = o[:, r * TQ:(r + 1) * TQ].T.astype(BF16)


def _attn_window_call(l, k2, k2_col0, q, k, v, sink, *, n_groups, heads_per_group, dk, dv, scale, batch, seq,
                      ctx_len):
    tq = SUBTILES_WIN * TQ
    n_q = seq // tq
    ctx_blk0 = batch * seq // ctx_len
    return pl.pallas_call(
        functools.partial(_attn_window_kernel, layer=l, heads_per_group=heads_per_group, dk=dk, dv=dv,
                          scale=scale, seq=seq, k2_col0=k2_col0),
        grid=(batch, n_groups, n_q),
        in_specs=[
            pl.BlockSpec(memory_space=pltpu.SMEM),
            pl.BlockSpec(memory_space=pltpu.SMEM),
            pl.BlockSpec((tq, heads_per_group * dk), lambda b, g, qi: (b * n_q + qi, g)),
            pl.BlockSpec((seq, dk), lambda b, g, qi: (b, g)),
            pl.BlockSpec((seq, dv), lambda b, g, qi: (b, g)),
            pl.BlockSpec((ctx_len, dk), lambda b, g, qi: (ctx_blk0 + b, g)),
            pl.BlockSpec((ctx_len, dv), lambda b, g, qi: (ctx_blk0 + b, g)),
        ],
        out_specs=pl.BlockSpec((tq, heads_per_group * dv), lambda b, g, qi: (b * n_q + qi, g)),
        out_shape=jax.ShapeDtypeStruct((batch * seq, n_groups * heads_per_group * dv), BF16),
        compiler_params=pltpu.CompilerParams(
            dimension_semantics=("arbitrary", "arbitrary", "arbitrary"),
            vmem_limit_bytes=48 * MIB),
        name="attn_window",
    )(sink, k2, q, k, v, k, v)


def _attn_dense_kernel(*refs, heads_per_group, dk, dv, scale, tq, shift_lane, cast_side_job, k2_col0):
    if cast_side_job:
        k2_ref, q_ref, kl_ref, vtl_ref, kc_ref, vtc_ref, w_ref, o_ref, wb_ref = refs
        wb_ref[...] = w_ref[...].astype(BF16)
    else:
        k2_ref, q_ref, kl_ref, vtl_ref, kc_ref, vtc_ref, o_ref = refs
    c2 = scale * LOG2E
    seq = kl_ref.shape[0]
    k2 = k2_ref[pl.program_id(0), k2_col0 + pl.program_id(1)]

    if heads_per_group > 1:
        q = jnp.concatenate([q_ref[:, r * dk:(r + 1) * dk] for r in range(heads_per_group)], axis=0)
    else:
        q = q_ref[...]
    qf = q.astype(F32)
    q2 = jnp.sum(qf * qf, axis=-1, keepdims=True)
    bound = jnp.sqrt(q2 * k2) * BOUND_SLACK
    bound_is_safe = jnp.max(bound) * c2 <= SAFE_SHIFT_LOG2

    def finish(acc, denom):
        o = acc / denom
        for r in range(heads_per_group):
            o_ref[:, r * dv:(r + 1) * dv] = o[:, r * tq:(r + 1) * tq].T.astype(BF16)

    @pl.when(bound_is_safe)
    def _():
        if shift_lane is None:
            shift = bound.T * c2
            p_c = jnp.exp2(_dot_nt(kc_ref[...], q) * c2 - shift)
            p_l = jnp.exp2(_dot_nt(kl_ref[...], q) * c2 - shift)
        else:
            lane = lax.broadcasted_iota(jnp.int32, q.shape, 1)
            q_shifted = jnp.where(lane == shift_lane, -bound, qf).astype(BF16)
            p_c = jnp.exp2(_dot_nt(kc_ref[...], q_shifted) * c2)
            p_l = jnp.exp2(_dot_nt(kl_ref[...], q_shifted) * c2)
        acc = _dot(vtc_ref[...], p_c.astype(BF16)) + _dot(vtl_ref[...], p_l.astype(BF16))
        if vtl_ref.shape[0] > dv:
            finish(acc[:dv], acc[dv:dv + 1])
        else:
            finish(acc, jnp.sum(p_c, axis=0, keepdims=True) + jnp.sum(p_l, axis=0, keepdims=True))

    @pl.when(jnp.logical_not(bound_is_safe))
    def _():
        chunks = [(lambda: kc_ref[...], lambda: vtc_ref[0:dv, :])]
        for j in range(seq // TK):
            chunks.append((functools.partial(lambda j: kl_ref[j * TK:(j + 1) * TK, :], j),
                           functools.partial(lambda j: vtl_ref[0:dv, j * TK:(j + 1) * TK], j)))
        n = len(chunks)
        s_next = _dot_nt(chunks[0][0](), q)
        m = denom = acc = None
        for j in range(n):
            s = s_next
            if j + 1 < n:
                s_next = _dot_nt(chunks[j + 1][0](), q)
            m_j = jnp.max(s, axis=0, keepdims=True) * c2
            m_new = m_j if j == 0 else jnp.maximum(m, m_j)
            p = jnp.exp2(s * c2 - m_new)
            pv = _dot(chunks[j][1](), p.astype(BF16))
            if j == 0:
                denom = jnp.sum(p, axis=0, keepdims=True)
                acc = pv
            else:
                alpha = jnp.exp2(m - m_new)
                denom = denom * alpha + jnp.sum(p, axis=0, keepdims=True)
                acc = acc * alpha + pv
            m = m_new
        finish(acc, denom)


def _attn_dense_call(k2, k2_col0, q, k, vt, *, n_groups, heads_per_group, dk, dv, scale, batch, seq, ctx_len, tq,
                     shift_lane=None, cast_weight=None):
    n_q = seq // tq
    ctx_blk0 = batch * seq // ctx_len
    vt_rows = vt.shape[0] // n_groups
    in_specs = [
        pl.BlockSpec(memory_space=pltpu.SMEM),
        pl.BlockSpec((tq, heads_per_group * dk), lambda b, g, qi: (b * n_q + qi, g)),
        pl.BlockSpec((seq, dk), lambda b, g, qi: (b, g)),
        pl.BlockSpec((vt_rows, seq), lambda b, g, qi: (g, b)),
        pl.BlockSpec((ctx_len, dk), lambda b, g, qi: (ctx_blk0 + b, g)),
        pl.BlockSpec((vt_rows, ctx_len), lambda b, g, qi: (g, ctx_blk0 + b)),
    ]
    args = [k2, q, k, vt, k, vt]
    out_specs = [pl.BlockSpec((tq, heads_per_group * dv), lambda b, g, qi: (b * n_q + qi, g))]
    out_shape = [jax.ShapeDtypeStruct((batch * seq, n_groups * heads_per_group * dv), BF16)]
    if cast_weight is not None:
        l, w = cast_weight
        n_steps = batch * n_groups * n_q
        rows, cols = w.shape[1] // n_steps, w.shape[2]
        assert rows * n_steps == w.shape[1] and rows % 16 == 0
        step = lambda b, g, qi: (b * n_groups + g) * n_q + qi
        in_specs.append(pl.BlockSpec((None, rows, cols), lambda b, g, qi: (l, step(b, g, qi), 0)))
        args.append(w)
        out_specs.append(pl.BlockSpec((rows, cols), lambda b, g, qi: (step(b, g, qi), 0)))
        out_shape.append(jax.ShapeDtypeStruct(w.shape[1:], BF16))
    outs = pl.pallas_call(
        functools.partial(_attn_dense_kernel, heads_per_group=heads_per_group, dk=dk, dv=dv, scale=scale, tq=tq,
                          shift_lane=shift_lane, cast_side_job=cast_weight is not None, k2_col0=k2_col0),
        grid=(batch, n_groups, n_q),
        in_specs=in_specs,
        out_specs=out_specs,
        out_shape=out_shape,
        compiler_params=pltpu.CompilerParams(
            dimension_semantics=("arbitrary", "arbitrary", "arbitrary"),
            vmem_limit_bytes=48 * MIB),
        name=f"attn_dense_dk{dk}",
    )(*args)
    return outs if cast_weight is not None else outs[0]


def _attn_ctx_kernel(sink_ref, qa_ref, ka_ref, va_ref, qb_ref, kb_ref, vtb_ref, qc_ref, kc_ref, vtc_ref,
                     oa_ref, ob_ref, oc_ref, *, layer, scale_ab, scale_c):
    def head(q, k, scale, pv, sink2=None):
        c2 = scale * LOG2E
        s = _dot_nt(q, k)
        m = jnp.max(s, axis=-1, keepdims=True) * c2
        if sink2 is not None:
            m = jnp.maximum(m, sink2)
        p = jnp.exp2(s * c2 - m)
        denom = jnp.sum(p, axis=-1, keepdims=True)
        if sink2 is not None:
            denom = denom + jnp.exp2(sink2 - m)
        return (pv(p.astype(BF16)) / denom).astype(BF16)

    for hd in range(A_HEADS):
        g = hd // (A_HEADS // A_KV)
        oa_ref[:, _col(hd)] = head(qa_ref[:, _col(hd)], ka_ref[:, _col(g)], scale_ab,
                                   lambda p: _dot(p, va_ref[:, _col(g)]), sink_ref[layer, hd] * LOG2E)
    for hd in range(B_HEADS):
        g = hd // (B_HEADS // B_KV)
        ob_ref[:, _col(hd)] = head(qb_ref[:, _col(hd)], kb_ref[:, _col(g)], scale_ab,
                                   lambda p: _dot_nt(p, vtb_ref[g * VT_EXT:g * VT_EXT + HEAD_DIM, :]))
    for hd in range(C_HEADS):
        qk = slice(hd * C_QK_PAD, (hd + 1) * C_QK_PAD)
        oc_ref[:, _col(hd)] = head(qc_ref[:, qk], kc_ref[:, qk], scale_c,
                                   lambda p: _dot_nt(p, vtc_ref[_col(hd), :]))


def _attn_ctx_call(l, sink, qa, ka, va, qb, kb, vtb, qc, kc, vtc, *, scale_ab, scale_c, batch, seq, ctx_len):
    blk0 = batch * seq // ctx_len
    rows = lambda a: pl.BlockSpec((ctx_len, a.shape[1]), lambda b: (blk0 + b, 0))
    cols = lambda a: pl.BlockSpec((a.shape[0], ctx_len), lambda b: (0, blk0 + b))
    out = lambda w: (pl.BlockSpec((ctx_len, w), lambda b: (b, 0)),
                     jax.ShapeDtypeStruct((batch * ctx_len, w), BF16))
    outs = [out(A_HEADS * HEAD_DIM), out(B_HEADS * HEAD_DIM), out(C_HEADS * C_V)]
    return pl.pallas_call(
        functools.partial(_attn_ctx_kernel, layer=l, scale_ab=scale_ab, scale_c=scale_c),
        grid=(batch,),
        in_specs=[pl.BlockSpec(memory_space=pltpu.SMEM), rows(qa), rows(ka), rows(va),
                  rows(qb), rows(kb), cols(vtb), rows(qc), rows(kc), cols(vtc)],
        out_specs=[o[0] for o in outs],
        out_shape=[o[1] for o in outs],
        compiler_params=pltpu.CompilerParams(
            dimension_semantics=("arbitrary",),
            vmem_limit_bytes=32 * MIB),
        name="attn_ctx",
    )(sink, qa, ka, va, qb, kb, vtb, qc, kc, vtc)


def _outproj_kernel(*refs, n_lat_tiles, has_ctx, split_x):
    lat_refs, refs = refs[:3], refs[3:]
    if has_ctx:
        ctx_refs, refs = refs[:3], refs[3:]
    x_refs, refs = (refs[:2], refs[2:]) if split_x else (refs[:1], refs[1:])
    mod_ref, gpost_ref, gffn_ref, wo_ref, xo_ref, h_ref = refs
    is_ctx_tile = pl.program_id(0) >= n_lat_tiles
    m = mod_ref[...]
    for half in range(2):
        rows = slice(half * (TM // 2), (half + 1) * (TM // 2))
        o, w0 = None, 0
        for g, lat_ref in enumerate(lat_refs):
            og = lat_ref[rows, :]
            if has_ctx:
                og = jnp.where(is_ctx_tile, ctx_refs[g][rows, :], og)
            width = lat_ref.shape[1]
            part = _dot(og, wo_ref[w0:w0 + width, :])
            o = part if o is None else o + part
            w0 += width
        x = x_refs[0][rows, :]
        if split_x:
            x = jnp.where(is_ctx_tile, x_refs[1][rows, :], x)
        x_new = x + m[2:3] * _rmsnorm(o, gpost_ref[...])
        xo_ref[rows, :] = x_new
        h_ref[rows, :] = (_rmsnorm(x_new, gffn_ref[...]) * (1.0 + m[4:5]) + m[3:4]).astype(BF16)


def _outproj_call(l, lat_outs, ctx_outs, xs, mods, gpost, gffn, wo, *, n_tiles, n_lat_tiles, tiles_per_seq, n_seq):
    t = n_tiles * TM
    d = xs[0].shape[1]
    seg = lambda i: jnp.minimum(i // tiles_per_seq, n_seq)
    layer = lambda i: (l, 0, 0)
    has_ctx = ctx_outs is not None
    split_x = len(xs) == 2
    lat_rows = lambda a: pl.BlockSpec((TM, a.shape[1]), lambda i: (jnp.minimum(i, n_lat_tiles - 1), 0))
    ctx_rows = lambda a: pl.BlockSpec((TM, a.shape[1]), lambda i: (jnp.maximum(i - n_lat_tiles, 0), 0))
    in_specs = [lat_rows(a) for a in lat_outs]
    args = list(lat_outs)
    if has_ctx:
        in_specs += [ctx_rows(a) for a in ctx_outs]
        args += list(ctx_outs)
    in_specs += [lat_rows(xs[0]), ctx_rows(xs[1])] if split_x else [pl.BlockSpec((TM, d), lambda i: (i, 0))]
    args += list(xs)
    in_specs += [
        pl.BlockSpec((None, None, MOD_ROWS, d), lambda i: (l, seg(i), 0, 0)),
        pl.BlockSpec((None, 1, d), layer),
        pl.BlockSpec((None, 1, d), layer),
        pl.BlockSpec(wo.shape, lambda i: (0, 0), pipeline_mode=pl.Buffered(1)),
    ]
    args += [mods, gpost, gffn, wo]
    return pl.pallas_call(
        functools.partial(_outproj_kernel, n_lat_tiles=n_lat_tiles, has_ctx=has_ctx, split_x=split_x),
        grid=(n_tiles,),
        in_specs=in_specs,
        out_specs=[pl.BlockSpec((TM, d), lambda i: (i, 0)), pl.BlockSpec((TM, d), lambda i: (i, 0))],
        out_shape=[jax.ShapeDtypeStruct((t, d), F32), jax.ShapeDtypeStruct((t, d), BF16)],
        compiler_params=pltpu.CompilerParams(
            dimension_semantics=("arbitrary",),
            vmem_limit_bytes=48 * MIB),
        name="outproj",
    )(*args)


def _ffn_up_kernel(h_ref, w1_ref, w3_ref, t_ref, w1_scr, w3_scr):
    @pl.when(pl.program_id(1) == 0)
    def _():
        w1_scr[...] = w1_ref[...].astype(BF16)
        w3_scr[...] = w3_ref[...].astype(BF16)

    h = h_ref[...]
    a = _dot(h, w1_scr[...])
    b = _dot(h, w3_scr[...])
    t_ref[...] = (a * jax.nn.sigmoid(a) * b).astype(BF16)


def _ffn_up_call(l, h, w1, w3):
    t, d = h.shape
    f = w1.shape[2]
    tm = t // N_UP_TILES
    return pl.pallas_call(
        _ffn_up_kernel,
        grid=(f // TF, N_UP_TILES),
        in_specs=[
            pl.BlockSpec((tm, d), lambda j, i: (i, 0)),
            pl.BlockSpec((None, d, TF), lambda j, i: (l, 0, j)),
            pl.BlockSpec((None, d, TF), lambda j, i: (l, 0, j)),
        ],
        out_specs=pl.BlockSpec((tm, TF), lambda j, i: (i, j)),
        out_shape=jax.ShapeDtypeStruct((t, f), BF16),
        scratch_shapes=[pltpu.VMEM((d, TF), BF16), pltpu.VMEM((d, TF), BF16)],
        compiler_params=pltpu.CompilerParams(
            dimension_semantics=("arbitrary", "arbitrary"),
            vmem_limit_bytes=48 * MIB),
        name="ffn_up",
    )(h, w1, w3)


def _ffn_down_kernel(t_ref, x_ref, mod_ref, gpost_ref, w2_ref, o_ref):
    y = _dot(t_ref[...], w2_ref[...])
    m = mod_ref[...]
    o_ref[...] = x_ref[...] + m[5:6] * _rmsnorm(y, gpost_ref[...])


def _ffn_down_call(l, tt, xa, mods, gpost, w2, *, tiles_per_seq, n_seq):
    t, d = xa.shape
    f = w2.shape[0]
    seg = lambda i: jnp.minimum(i // tiles_per_seq, n_seq)
    layer = lambda i: (l, 0, 0)
    return pl.pallas_call(
        _ffn_down_kernel,
        grid=(t // TM,),
        in_specs=[
            pl.BlockSpec((TM, f), lambda i: (i, 0)),
            pl.BlockSpec((TM, d), lambda i: (i, 0)),
            pl.BlockSpec((None, None, MOD_ROWS, d), lambda i: (l, seg(i), 0, 0)),
            pl.BlockSpec((None, 1, d), layer),
            pl.BlockSpec((f, d), lambda i: (0, 0), pipeline_mode=pl.Buffered(1)),
        ],
        out_specs=pl.BlockSpec((TM, d), lambda i: (i, 0)),
        out_shape=jax.ShapeDtypeStruct((t, d), F32),
        compiler_params=pltpu.CompilerParams(
            dimension_semantics=("arbitrary",),
            vmem_limit_bytes=60 * MIB),
        name="ffn_down",
    )(tt, xa, mods, gpost, w2)


def _rope_tables(seq):
    pos = np.arange(seq)
    row = (pos // GRID_W).astype(np.float64)[:, None]
    colp = (pos % GRID_W).astype(np.float64)[:, None]

    def cs(n):
        freqs = np.power(np.float64(ROPE_THETA), -np.arange(n, dtype=np.float64) / n)[None, :]
        return (np.cos(row * freqs), np.sin(row * freqs), np.cos(colp * freqs), np.sin(colp * freqs))

    cr, sr, cc, sc = cs(HEAD_DIM // 4)
    cos128 = np.concatenate([cr, cr, cc, cc], axis=1)
    sin128 = np.concatenate([-sr, sr, -sc, sc], axis=1)
    cr, sr, cc, sc = cs(C_ROPE // 4)
    pad1 = np.ones((seq, LANES - C_ROPE), np.float32)
    pad0 = np.zeros((seq, LANES - C_ROPE), np.float32)
    cos64 = np.concatenate([cr, cr, cc, cc, pad1], axis=1)
    sin64 = np.concatenate([-sr, sr, -sc, sc, pad0], axis=1)
    ident_c = np.ones((TM, LANES), np.float32)
    ident_s = np.zeros((TM, LANES), np.float32)
    return tuple(jnp.asarray(np.concatenate([tab, ident], axis=0).astype(np.float32))
                 for tab, ident in ((cos128, ident_c), (sin128, ident_s), (cos64, ident_c), (sin64, ident_s)))


def kernel(x, c, ctx, c_ctx, w_mod, b_mod, g_mix_pre, g_mix_post, g_ffn_pre, g_ffn_post, w_in, sink_a,
           g_qn_b, g_kn_b, g_cq, w_uq, g_ckv, w_ukv, w_out, w_ffn1, w_ffn3, w_ffn2):
    batch, seq, d = x.shape
    ctx_len = ctx.shape[1]
    depth = w_mod.shape[0]
    n_lat, n_ctx = batch * seq, batch * ctx_len
    assert seq % TM == 0 and n_ctx % TM == 0 and seq % ctx_len == 0 and ctx_len % TQ == 0
    assert seq % TQ_GQA == 0 and seq % TQ_MLA == 0 and seq % TK == 0 and ctx_len % LANES == 0
    assert batch + 1 <= MOD_ROWS and seq % GRID_W == 0 and seq % (SUBTILES_WIN * TQ) == 0
    assert n_lat % (16 * N_UP_TILES) == 0 and (n_lat + n_ctx) % (16 * N_UP_TILES) == 0
    assert ctx_len == TM // 2
    tiles_per_seq = seq // TM
    n_lat_tiles = n_lat // TM

    cv = jnp.zeros((MOD_ROWS, d), F32).at[:batch].set(c).at[batch].set(c_ctx)
    mod_all = _mod_call(cv, w_mod, b_mod)
    mods = mod_all[:, :batch + 1].reshape(depth, batch + 1, 6, d)
    mods = jnp.pad(mods, ((0, 0), (0, 0), (0, MOD_ROWS - 6), (0, 0)))

    tabs = _rope_tables(seq)

    win = w_in.astype(BF16)
    wuq = w_uq.reshape(depth, C_Q_LORA, C_HEADS, C_NOPE + C_ROPE)
    wuq = jnp.pad(wuq, ((0, 0), (0, 0), (0, 0), (0, C_QK_PAD - C_NOPE - C_ROPE)))
    wuq = wuq.reshape(depth, C_Q_LORA, C_HEADS * C_QK_PAD).astype(BF16)
    wukv = w_ukv.reshape(depth, C_KV_LORA, C_HEADS, 2, C_NOPE).transpose(0, 1, 3, 2, 4)
    wukv = wukv.reshape(depth, C_KV_LORA, 2 * C_HEADS * C_NOPE).astype(BF16)
    vecs = lambda v: v.reshape(depth, 1, -1)
    g_mix_pre, g_mix_post, g_ffn_pre, g_ffn_post = map(vecs, (g_mix_pre, g_mix_post, g_ffn_pre, g_ffn_post))
    g_qn_b, g_kn_b, g_cq, g_ckv = map(vecs, (g_qn_b, g_kn_b, g_cq, g_ckv))

    xs = (x.reshape(n_lat, d), ctx.reshape(n_ctx, d))
    sc_ab = 1.0 / math.sqrt(HEAD_DIM)
    sc_c = 1.0 / math.sqrt(C_NOPE + C_ROPE)
    dims = dict(batch=batch, seq=seq, ctx_len=ctx_len)

    for l in range(depth):
        last = l == depth - 1
        qa, ka, va, qb, kb, vtb, qc, kc, vtc, wo, k2_halves = _qkv_call(
            l, xs, mods, g_mix_pre, win, tabs, g_qn_b, g_kn_b, g_cq, g_ckv, wuq, wukv, w_out,
            n_lat_tiles=n_lat_tiles, tiles_per_seq=tiles_per_seq, n_seq=batch)
        k2_halves = k2_halves[:, 0, :]
        k2 = jnp.maximum(jnp.max(k2_halves[:2 * n_lat_tiles].reshape(batch, -1, LANES), axis=1),
                         k2_halves[2 * n_lat_tiles:])
        oa = _attn_window_call(l, k2, B_KV + C_HEADS, qa, ka, va, sink_a, n_groups=A_KV, heads_per_group=A_HEADS // A_KV,
                               dk=HEAD_DIM, dv=HEAD_DIM, scale=sc_ab, **dims)
        ob = _attn_dense_call(k2, 0, qb, kb, vtb, n_groups=B_KV, heads_per_group=B_HEADS // B_KV, dk=HEAD_DIM,
                              dv=HEAD_DIM, scale=sc_ab, tq=TQ_GQA, **dims)
        oc, w2 = _attn_dense_call(k2, B_KV, qc, kc, vtc, n_groups=C_HEADS, heads_per_group=1, dk=C_QK_PAD,
                                  dv=C_V, scale=sc_c, tq=TQ_MLA, shift_lane=MLA_SHIFT_LANE,
                                  cast_weight=(l, w_ffn2), **dims)
        ctx_outs = None if last else _attn_ctx_call(l, sink_a, qa, ka, va, qb, kb, vtb, qc, kc, vtc,
                                                    scale_ab=sc_ab, scale_c=sc_c, **dims)
        n_tiles = n_lat_tiles if last else n_lat_tiles + n_ctx // TM
        xa, h = _outproj_call(l, (oa, ob, oc), ctx_outs, xs, mods, g_mix_post, g_ffn_pre, wo, n_tiles=n_tiles,
                              n_lat_tiles=n_lat_tiles, tiles_per_seq=tiles_per_seq, n_seq=batch)
        tt = _ffn_up_call(l, h, w_ffn1, w_ffn3)
        xa = _ffn_down_call(l, tt, xa, mods, g_ffn_post, w2, tiles_per_seq=tiles_per_seq, n_seq=batch)
        xs = (xa,)
    return xa.reshape(batch, seq, d)
```
